```python
import math
import jax, jax.numpy as jnp
from jax import lax
import numpy as np

D_MODEL = 2048
BATCH = 4
SEQ = 2048
DEPTH = 1

MIX_WIDTH = D_MODEL
DA_WIDTH = MIX_WIDTH // 2
RET_WIDTH = MIX_WIDTH - DA_WIDTH
DA_HEAD_DIM = 64
DA_V_DIM = 2 * DA_HEAD_DIM
DA_HEADS = DA_WIDTH // DA_V_DIM
RET_HEADS = 4
RET_V_DIM = RET_WIDTH // RET_HEADS
RET_K_DIM = RET_V_DIM // 2
RET_CHUNK = 128
Q_BLOCK = 128
ROPE_THETA = 10000.0
PLE_DIM = 256
N_GROUPS = 4
EXPERTS_PER_GROUP = 8
N_EXPERTS = N_GROUPS * EXPERTS_PER_GROUP
TOP_K = 2
EXPERT_FF = D_MODEL // 4
DEEPNORM_ALPHA = (2 * DEPTH) ** 0.25
DEEPNORM_BETA = (8 * DEPTH) ** -0.25
LN_EPS = 1e-5
SPLITS = (DA_HEADS * 2 * DA_HEAD_DIM, DA_HEADS * 2 * DA_HEAD_DIM, DA_HEADS * DA_V_DIM,
          RET_HEADS * RET_K_DIM, RET_HEADS * RET_K_DIM, RET_HEADS * RET_V_DIM, RET_WIDTH)
IN_WIDTH = sum(SPLITS)

kernel_name = "hymba_diffattn_retnet_hiermoe_deepnorm_ple"


def _layer_norm(x, g, b):
    xf = x.astype(jnp.float32)
    mu = jnp.mean(xf, -1, keepdims=True)
    var = jnp.mean(jnp.square(xf - mu), -1, keepdims=True)
    y = (xf - mu) * lax.rsqrt(var + LN_EPS) * g.astype(jnp.float32) + b.astype(jnp.float32)
    return y.astype(x.dtype)


def _rope(t, positions):
    d = t.shape[-1]
    inv = jnp.power(ROPE_THETA, -jnp.arange(0, d, 2, dtype=jnp.float32) / d)
    ang = positions.astype(jnp.float32)[..., None] * inv
    ang = ang.reshape(ang.shape[:2] + (1,) * (t.ndim - 3) + ang.shape[-1:])
    cos, sin = jnp.cos(ang), jnp.sin(ang)
    t1, t2 = jnp.split(t.astype(jnp.float32), 2, axis=-1)
    return jnp.concatenate([t1 * cos - t2 * sin, t2 * cos + t1 * sin], -1).astype(t.dtype)


def _diff_attention(q, k, v, lam, subln_w, lam_init):
    B, S, H, _, dh = q.shape
    dv = v.shape[-1]
    nb = S // Q_BLOCK
    scale = dh ** -0.5
    qb = q.reshape(B, nb, Q_BLOCK, H, 2, dh).transpose(1, 0, 3, 4, 2, 5)
    kt = k.transpose(0, 2, 3, 1, 4)
    vt = v.transpose(0, 2, 1, 3)
    kpos = jnp.arange(S)
    neg = jnp.finfo(jnp.float32).min

    def block(args):
        qblk, i = args
        s = jnp.einsum('bhtqd,bhtkd->bhtqk', qblk, kt).astype(jnp.float32) * scale
        qpos = i * Q_BLOCK + jnp.arange(Q_BLOCK)
        s = jnp.where(kpos[None, :] <= qpos[:, None], s, neg)
        a = jax.nn.softmax(s, axis=-1)
        a = a[:, :, 0] - lam * a[:, :, 1]
        return jnp.einsum('bhqk,bhkd->bhqd', a.astype(v.dtype), vt)

    o = lax.map(block, (qb, jnp.arange(nb)))
    o = o.transpose(1, 0, 3, 2, 4).reshape(B, S, H, dv).astype(jnp.float32)
    o = o * lax.rsqrt(jnp.mean(jnp.square(o), -1, keepdims=True) + LN_EPS)
    o = o * subln_w.astype(jnp.float32) * (1.0 - lam_init)
    return o.reshape(B, S, H * dv).astype(v.dtype)


def _retention(q, k, v, g):
    B, S, H, dk = q.shape
    dv = v.shape[-1]
    C = RET_CHUNK
    nc = S // C
    k = k * (dk ** -0.5)
    to_chunks = lambda t: t.reshape(B, nc, C, H, t.shape[-1]).transpose(0, 3, 1, 2, 4)
    qc, kc, vc = to_chunks(q), to_chunks(k), to_chunks(v)
    lg = jnp.log(1.0 - jnp.power(2.0, -5.0 - jnp.arange(H, dtype=jnp.float32)))
    n = jnp.arange(C, dtype=jnp.float32)
    rel = n[:, None] - n[None, :]
    decay = jnp.where(rel >= 0, jnp.exp(rel[None] * lg[:, None, None]), 0.0)
    inner = jnp.einsum('bhcnd,bhcmd->bhcnm', qc, kc).astype(jnp.float32) * decay[None, :, None]
    inner_o = jnp.einsum('bhcnm,bhcme->bhcne', inner, vc.astype(jnp.float32))
    zeta = jnp.exp((C - 1.0 - n)[None] * lg[:, None])
    xi = jnp.exp((n + 1.0)[None] * lg[:, None])
    kv = jnp.einsum('bhcmd,bhcme->bhcde',
                    kc.astype(jnp.float32) * zeta[None, :, None, :, None],
                    vc.astype(jnp.float32))
    chunk_decay = jnp.exp(C * lg)[None, :, None, None]

    def step(state, kv_c):
        return chunk_decay * state + kv_c, state

    _, r_prev = lax.scan(step, jnp.zeros((B, H, dk, dv), jnp.float32),
                         kv.transpose(2, 0, 1, 3, 4))
    cross = jnp.einsum('bhcnd,cbhde->bhcne', qc.astype(jnp.float32), r_prev)
    y = inner_o + cross * xi[None, :, None, :, None]
    y = y.transpose(0, 2, 3, 1, 4).reshape(B, S, H, dv)
    mu = jnp.mean(y, -1, keepdims=True)
    var = jnp.mean(jnp.square(y - mu), -1, keepdims=True)
    y = ((y - mu) * lax.rsqrt(var + LN_EPS)).reshape(B, S, H * dv)
    return (jax.nn.silu(g.astype(jnp.float32)) * y).astype(v.dtype)


def _hybrid_mixer(x, positions, w_in, w_out, lq1, lk1, lq2, lk2, subln_w, lam_init):
    B, S, _ = x.shape
    q_da, k_da, v_da, q_r, k_r, v_r, g_r = jnp.split(x @ w_in, np.cumsum(SPLITS)[:-1], axis=-1)
    q_da = _rope(q_da.reshape(B, S, DA_HEADS, 2, DA_HEAD_DIM), positions)
    k_da = _rope(k_da.reshape(B, S, DA_HEADS, 2, DA_HEAD_DIM), positions)
    v_da = v_da.reshape(B, S, DA_HEADS, DA_V_DIM)
    f32 = jnp.float32
    lam = (jnp.exp(jnp.sum(lq1.astype(f32) * lk1.astype(f32)))
           - jnp.exp(jnp.sum(lq2.astype(f32) * lk2.astype(f32))) + lam_init)
    o_da = _diff_attention(q_da, k_da, v_da, lam, subln_w, lam_init)
    q_r = _rope(q_r.reshape(B, S, RET_HEADS, RET_K_DIM), positions)
    k_r = _rope(k_r.reshape(B, S, RET_HEADS, RET_K_DIM), positions)
    v_r = v_r.reshape(B, S, RET_HEADS, RET_V_DIM)
    o_r = _retention(q_r, k_r, v_r, g_r)
    return jnp.concatenate([o_da, o_r], axis=-1) @ w_out


def _hier_moe(x, w_rg, b_rg, w_re, b_re, w_gate, w_up, w_down):
    B, S, D = x.shape
    xf = x.reshape(B * S, D)
    gl = (xf @ w_rg).astype(jnp.float32) + b_rg.astype(jnp.float32)
    gp = jax.nn.softmax(gl, axis=-1)
    g_idx = jnp.argmax(gl, axis=-1)
    g_w = jnp.take_along_axis(gp, g_idx[:, None], axis=1)
    el = ((xf @ w_re).astype(jnp.float32) + b_re.astype(jnp.float32)).reshape(-1, N_GROUPS, EXPERTS_PER_GROUP)
    el = jnp.take_along_axis(el, g_idx[:, None, None], axis=1)[:, 0]
    tv, ti = lax.top_k(el, TOP_K)
    fw = jax.nn.softmax(tv, axis=-1) * g_w
    eid = g_idx[:, None] * EXPERTS_PER_GROUP + ti
    combine = jnp.sum(jax.nn.one_hot(eid, N_EXPERTS, dtype=jnp.float32) * fw[..., None], axis=1)

    def expert_step(y, params):
        wg, wu, wd, c = params
        h = jax.nn.silu(xf @ wg) * (xf @ wu)
        return y + (h @ wd).astype(jnp.float32) * c[:, None], None

    y, _ = lax.scan(expert_step, jnp.zeros(xf.shape, jnp.float32),
                    (w_gate, w_up, w_down, combine.T))
    return y.reshape(B, S, D).astype(x.dtype)


def setup_inputs(seed: int = 0) -> dict:
    key = jax.random.key(seed)
    ks = jax.random.split(key, 24)
    nrm = lambda k, shape, s: jax.random.normal(k, shape, jnp.float32) * s
    beta = DEEPNORM_BETA
    col_scale = jnp.concatenate([
        jnp.ones((SPLITS[0] + SPLITS[1],), jnp.float32),
        jnp.full((SPLITS[2],), beta, jnp.float32),
        jnp.ones((SPLITS[3] + SPLITS[4],), jnp.float32),
        jnp.full((SPLITS[5],), beta, jnp.float32),
        jnp.ones((SPLITS[6],), jnp.float32)])
    return {
        "x": nrm(ks[0], (BATCH, SEQ, D_MODEL), 1.0),
        "p": nrm(ks[1], (DEPTH, BATCH, SEQ, PLE_DIM), 1.0),
        "positions": jnp.broadcast_to(jnp.arange(SEQ, dtype=jnp.int32), (BATCH, SEQ)),
        "w_in": nrm(ks[2], (DEPTH, D_MODEL, IN_WIDTH), D_MODEL ** -0.5) * col_scale,
        "w_out": nrm(ks[3], (DEPTH, MIX_WIDTH, D_MODEL), beta * MIX_WIDTH ** -0.5),
        "da_lambda_q1": nrm(ks[4], (DEPTH, DA_HEAD_DIM), 0.1),
        "da_lambda_k1": nrm(ks[5], (DEPTH, DA_HEAD_DIM), 0.1),
        "da_lambda_q2": nrm(ks[6], (DEPTH, DA_HEAD_DIM), 0.1),
        "da_lambda_k2": nrm(ks[7], (DEPTH, DA_HEAD_DIM), 0.1),
        "da_subln_w": 1.0 + nrm(ks[8], (DEPTH, DA_V_DIM), 0.02),
        "ln1_g": 1.0 + nrm(ks[9], (DEPTH, D_MODEL), 0.02),
        "ln1_b": nrm(ks[10], (DEPTH, D_MODEL), 0.02),
        "w_router_group": nrm(ks[11], (DEPTH, D_MODEL, N_GROUPS), D_MODEL ** -0.5),
        "b_router_group": nrm(ks[12], (DEPTH, N_GROUPS), 0.01),
        "w_router_expert": nrm(ks[13], (DEPTH, D_MODEL, N_EXPERTS), D_MODEL ** -0.5),
        "b_router_expert": nrm(ks[14], (DEPTH, N_EXPERTS), 0.01),
        "w_exp_gate": nrm(ks[15], (DEPTH, N_EXPERTS, D_MODEL, EXPERT_FF), D_MODEL ** -0.5),
        "w_exp_up": nrm(ks[16], (DEPTH, N_EXPERTS, D_MODEL, EXPERT_FF), D_MODEL ** -0.5),
        "w_exp_down": nrm(ks[17], (DEPTH, N_EXPERTS, EXPERT_FF, D_MODEL), beta * EXPERT_FF ** -0.5),
        "ln2_g": 1.0 + nrm(ks[18], (DEPTH, D_MODEL), 0.02),
        "ln2_b": nrm(ks[19], (DEPTH, D_MODEL), 0.02),
        "w_ple_gate": nrm(ks[20], (DEPTH, D_MODEL, D_MODEL), D_MODEL ** -0.5),
        "w_ple_proj": nrm(ks[21], (DEPTH, PLE_DIM, D_MODEL), PLE_DIM ** -0.5),
    }


def reference(x, p, positions, w_in, w_out, da_lambda_q1, da_lambda_k1, da_lambda_q2,
              da_lambda_k2, da_subln_w, ln1_g, ln1_b, w_router_group, b_router_group,
              w_router_expert, b_router_expert, w_exp_gate, w_exp_up, w_exp_down,
              ln2_g, ln2_b, w_ple_gate, w_ple_proj):
    for l in range(DEPTH):
        lam_init = 0.8 - 0.6 * math.exp(-0.3 * l)
        h = _hybrid_mixer(x, positions, w_in[l], w_out[l], da_lambda_q1[l], da_lambda_k1[l],
                          da_lambda_q2[l], da_lambda_k2[l], da_subln_w[l], lam_init)
        x = _layer_norm(DEEPNORM_ALPHA * x + h, ln1_g[l], ln1_b[l])
        m = _hier_moe(x, w_router_group[l], b_router_group[l], w_router_expert[l],
                      b_router_expert[l], w_exp_gate[l], w_exp_up[l], w_exp_down[l])
        x = _layer_norm(DEEPNORM_ALPHA * x + m, ln2_g[l], ln2_b[l])
        x = x + jax.nn.sigmoid(x @ w_ple_gate[l]) * (p[l] @ w_ple_proj[l])
    return x
```

```python
import functools
import math

import numpy as np
import jax
import jax.numpy as jnp
from jax import lax
from jax.experimental import pallas as pl
from jax.experimental.pallas import tpu as pltpu

D_MODEL = 2048
DA_HEADS = 8
DA_HEAD_DIM = 64
DA_V_DIM = 128
RET_HEADS = 4
RET_K_DIM = 128
RET_V_DIM = 256
RET_CHUNK = 128
ROPE_THETA = 10000.0
PLE_DIM = 256
N_GROUPS = 4
EXPERTS_PER_GROUP = 8
N_EXPERTS = 32
EXPERT_FF = 512
DEPTH = 1
DEEPNORM_ALPHA = (2 * DEPTH) ** 0.25
LN_EPS = 1e-5
IN_WIDTH = 6144
DA_WIDTH = 1024
RET_WIDTH = 1024

OFF_QDA, OFF_KDA, OFF_VDA, OFF_QR, OFF_KR, OFF_VR, OFF_GR = 0, 1024, 2048, 3072, 3584, 4096, 5120

LANES = 128
SUBLANES = 8
VMEM_LIMIT = 56 * 1024 * 1024

BF16 = jnp.bfloat16
F32 = jnp.float32

TM_PROJ = 512
TN_PROJ = 512
TQ = 256
TM_ROW = 256
TM_EXP = 256
TM_DISP = 512
PK_W = D_MODEL // 2
PK_R = PK_W // LANES
Y_R = D_MODEL // LANES


def _cparams(sem, vmem=None):
    return pltpu.CompilerParams(dimension_semantics=sem, vmem_limit_bytes=vmem)


def _rope_table_kernel(pos_ref, inv_ref, sgn_ref, tab_ref):
    pos = pos_ref[...].astype(F32)
    ang = pos * inv_ref[...]
    c = jnp.cos(ang)
    s = jnp.sin(ang)
    lane = lax.broadcasted_iota(jnp.int32, ang.shape, 1)
    low = lane < 64
    cr = pltpu.roll(c, 64, 1)
    sr = pltpu.roll(s, 64, 1)
    sgn = sgn_ref[...]
    tab_ref[:, 0:128] = jnp.where(low, cr, c)
    tab_ref[:, 128:256] = jnp.where(low, sr, s) * sgn[0:1, :]
    tab_ref[:, 256:384] = jnp.where(low, c, cr)
    tab_ref[:, 384:512] = jnp.where(low, s, sr) * sgn[1:2, :]


def _rope_tables(positions):
    n = positions.size
    tm = 1024
    i128 = np.power(ROPE_THETA, -np.arange(0, 128, 2, dtype=np.float64) / 128)
    i64 = np.power(ROPE_THETA, -np.arange(0, 64, 2, dtype=np.float64) / 64)
    inv = np.concatenate([i128, i64, i64]).astype(np.float32)[None, :]
    lane = np.arange(128)
    sgn = np.stack([np.where(lane % 64 < 32, -1.0, 1.0), np.where(lane < 64, -1.0, 1.0)]).astype(np.float32)
    sgn = np.concatenate([sgn, np.zeros((6, 128), np.float32)])
    return pl.pallas_call(
        _rope_table_kernel,
        grid=(n // tm,),
        in_specs=[pl.BlockSpec((tm, 1), lambda i: (i, 0)),
                  pl.BlockSpec((1, 128), lambda i: (0, 0)),
                  pl.BlockSpec((8, 128), lambda i: (0, 0))],
        out_specs=pl.BlockSpec((tm, 512), lambda i: (i, 0)),
        out_shape=jax.ShapeDtypeStruct((n, 512), F32),
        compiler_params=_cparams(("arbitrary",)),
        name="rope_tables",
    )(positions.reshape(n, 1), jnp.asarray(inv), jnp.asarray(sgn))


def _rope_cols(acc, cos, sin, half, scale):
    outs = []
    lane = lax.broadcasted_iota(jnp.int32, (acc.shape[0], LANES), 1)
    for c in range(acc.shape[1] // LANES):
        t = acc[:, c * LANES:(c + 1) * LANES]
        if half == 64:
            partner = pltpu.roll(t, 64, 1)
        else:
            partner = jnp.where((lane & 32) == 0, pltpu.roll(t, 96, 1), pltpu.roll(t, 32, 1))
        o = t * cos + partner * sin
        if scale != 1.0:
            o = o * scale
        outs.append(o)
    return jnp.concatenate(outs, axis=1)


def _in_proj_kernel(x_ref, w_ref, tab_ref, o_ref, xb_ref):
    j = pl.program_id(1)

    @pl.when(j == 0)
    def _():
        xb_ref[...] = x_ref[...].astype(BF16)

    acc = jnp.dot(xb_ref[...], w_ref[...], preferred_element_type=F32)
    n_da = (OFF_VDA) // TN_PROJ
    j_qr = OFF_QR // TN_PROJ
    j_kr = OFF_KR // TN_PROJ

    @pl.when(j < n_da)
    def _():
        o_ref[...] = _rope_cols(acc, tab_ref[:, 0:128], tab_ref[:, 128:256], 32, 1.0).astype(BF16)

    @pl.when(j == j_qr)
    def _():
        o_ref[...] = _rope_cols(acc, tab_ref[:, 256:384], tab_ref[:, 384:512], 64, 1.0).astype(BF16)

    @pl.when(j == j_kr)
    def _():
        o_ref[...] = _rope_cols(acc, tab_ref[:, 256:384], tab_ref[:, 384:512], 64,
                                RET_K_DIM ** -0.5).astype(BF16)

    @pl.when((j >= n_da) & (j != j_qr) & (j != j_kr))
    def _():
        o_ref[...] = acc.astype(BF16)


def _in_proj(x2, w_in_b, tab):
    n = x2.shape[0]
    return pl.pallas_call(
        _in_proj_kernel,
        grid=(n // TM_PROJ, IN_WIDTH // TN_PROJ),
        in_specs=[pl.BlockSpec((TM_PROJ, D_MODEL), lambda i, j: (i, 0)),
                  pl.BlockSpec((D_MODEL, TN_PROJ), lambda i, j: (0, j)),
                  pl.BlockSpec((TM_PROJ, 512), lambda i, j: (i, 0))],
        out_specs=pl.BlockSpec((TM_PROJ, TN_PROJ), lambda i, j: (i, j)),
        out_shape=jax.ShapeDtypeStruct((n, IN_WIDTH), BF16),
        scratch_shapes=[pltpu.VMEM((TM_PROJ, D_MODEL), BF16)],
        compiler_params=_cparams(("arbitrary", "arbitrary"), VMEM_LIMIT),
        name="in_proj",
    )(x2, w_in_b, tab)


def _diff_attn_kernel(lam_init, q_ref, k_ref, v_ref, lq1_ref, lk1_ref, lq2_ref, lk2_ref, sw_ref,
                      o_ref, m_ref, l_ref, acc_ref):
    qi = pl.program_id(2)
    q = q_ref[...]
    lane = lax.broadcasted_iota(jnp.int32, q.shape, 1)
    zero = jnp.zeros_like(q)
    scale = jnp.asarray(DA_HEAD_DIM ** -0.5, BF16)
    qs = jnp.concatenate([jnp.where(lane < 64, q, zero), jnp.where(lane >= 64, q, zero)], axis=0) * scale

    m_ref[...] = jnp.full(m_ref.shape, -jnp.inf, F32)
    l_ref[...] = jnp.zeros(l_ref.shape, F32)
    acc_ref[...] = jnp.zeros(acc_ref.shape, F32)

    def step(kb, masked):
        start = pl.multiple_of(kb * TQ, TQ)
        k = k_ref[pl.ds(start, TQ), :]
        v = v_ref[pl.ds(start, TQ), :]
        s = lax.dot_general(qs, k, (((1,), (1,)), ((), ())), preferred_element_type=F32)
        if masked:
            row = lax.broadcasted_iota(jnp.int32, s.shape, 0)
            col = lax.broadcasted_iota(jnp.int32, s.shape, 1)
            row = jnp.where(row >= TQ, row - TQ, row)
            s = jnp.where(col <= row, s, jnp.finfo(F32).min)
        m_old = m_ref[...]
        m_new = jnp.maximum(m_old, jnp.max(s, axis=-1, keepdims=True))
        p = jnp.exp(s - m_new)
        alpha = jnp.exp(m_old - m_new)
        l_ref[...] = alpha * l_ref[...] + jnp.sum(p, axis=-1, keepdims=True)
        acc_ref[...] = alpha * acc_ref[...] + jnp.dot(p.astype(BF16), v, preferred_element_type=F32)
        m_ref[...] = m_new

    def body(kb, carry):
        step(kb, False)
        return carry

    lax.fori_loop(0, qi, body, 0)
    step(qi, True)

    lam = (jnp.exp(jnp.sum(lq1_ref[...] * lk1_ref[...], axis=-1, keepdims=True))
           - jnp.exp(jnp.sum(lq2_ref[...] * lk2_ref[...], axis=-1, keepdims=True)) + lam_init)
    a = acc_ref[...] / l_ref[...]
    o = a[0:TQ, :] - lam * a[TQ:2 * TQ, :]
    o = o * lax.rsqrt(jnp.mean(o * o, axis=-1, keepdims=True) + LN_EPS)
    o = o * sw_ref[...] * (1.0 - lam_init)
    o_ref[...] = o.astype(BF16)


def _diff_attn(qkv, lq1, lk1, lq2, lk2, subln_w, lam_init, batch, seq):
    n = qkv.shape[0]
    nq = seq // TQ
    kb0 = OFF_KDA // LANES
    vb0 = OFF_VDA // LANES
    vec = lambda a: a.reshape(1, -1).astype(F32)
    small = lambda w: pl.BlockSpec((1, w), lambda b, h, i: (0, 0))
    return pl.pallas_call(
        functools.partial(_diff_attn_kernel, lam_init),
        grid=(batch, DA_HEADS, nq),
        in_specs=[pl.BlockSpec((TQ, LANES), lambda b, h, i: (b * nq + i, h)),
                  pl.BlockSpec((seq, LANES), lambda b, h, i: (b, kb0 + h)),
                  pl.BlockSpec((seq, LANES), lambda b, h, i: (b, vb0 + h)),
                  small(64), small(64), small(64), small(64), small(128)],
        out_specs=pl.BlockSpec((TQ, LANES), lambda b, h, i: (b * nq + i, h)),
        out_shape=jax.ShapeDtypeStruct((n, DA_WIDTH), BF16),
        scratch_shapes=[pltpu.VMEM((2 * TQ, 1), F32), pltpu.VMEM((2 * TQ, 1), F32),
                        pltpu.VMEM((2 * TQ, LANES), F32)],
        compiler_params=_cparams(("arbitrary", "arbitrary", "arbitrary")),
        name="diff_attn",
    )(qkv, qkv, qkv, vec(lq1), vec(lk1), vec(lq2), vec(lk2), vec(subln_w))


def _retention_kernel(q_ref, k_ref, v_ref, g_ref, lg_ref, o_ref, state_ref):
    C = RET_CHUNK
    lg = lg_ref[0, 0:1, :]
    lg_col = lg[:, 0:1]
    n_i = lax.broadcasted_iota(jnp.int32, (C, C), 0)
    m_i = lax.broadcasted_iota(jnp.int32, (C, C), 1)
    rel = (n_i - m_i).astype(F32)
    decay = jnp.where(rel >= 0, jnp.exp(rel * lg_col), 0.0)
    n_col = lax.broadcasted_iota(jnp.int32, (C, 1), 0).astype(F32)
    zeta = jnp.exp((C - 1.0 - n_col) * lg_col)
    xi = jnp.exp((n_col + 1.0) * lg_col)
    chunk_decay = jnp.exp(C * lg_col)
    state_ref[...] = jnp.zeros(state_ref.shape, F32)

    def body(c, carry):
        start = pl.multiple_of(c * C, C)
        q = q_ref[pl.ds(start, C), :]
        k = k_ref[pl.ds(start, C), :]
        v = v_ref[pl.ds(start, C), :]
        g = g_ref[pl.ds(start, C), :].astype(F32)
        qk = lax.dot_general(q, k, (((1,), (1,)), ((), ())), preferred_element_type=F32)
        inner = (qk * decay).astype(BF16)
        inner_o = jnp.dot(inner, v, preferred_element_type=F32)
        state = state_ref[...]
        cross = jnp.dot(q, state.astype(BF16), preferred_element_type=F32)
        y = inner_o + cross * xi
        kz_t = (k.astype(F32) * zeta).T.astype(BF16)
        kv = jnp.dot(kz_t, v, preferred_element_type=F32)
        state_ref[...] = chunk_decay * state + kv
        mu = jnp.mean(y, axis=-1, keepdims=True)
        yc = y - mu
        var = jnp.mean(yc * yc, axis=-1, keepdims=True)
        yn = yc * lax.rsqrt(var + LN_EPS)
        o_ref[pl.ds(start, C), :] = (g * jax.nn.sigmoid(g) * yn).astype(BF16)
        return carry

    lax.fori_loop(0, q_ref.shape[0] // C, body, 0)


def _retention(qkv, batch, seq):
    n = qkv.shape[0]
    lg = np.log(1.0 - np.power(2.0, -5.0 - np.arange(RET_HEADS, dtype=np.float64)))
    lg_tab = np.broadcast_to(lg[:, None, None], (RET_HEADS, SUBLANES, LANES)).astype(np.float32)
    qb0 = OFF_QR // RET_K_DIM
    kb0 = OFF_KR // RET_K_DIM
    vb0 = OFF_VR // RET_V_DIM
    gb0 = OFF_GR // RET_V_DIM
    return pl.pallas_call(
        _retention_kernel,
        grid=(batch, RET_HEADS),
        in_specs=[pl.BlockSpec((seq, RET_K_DIM), lambda b, h: (b, qb0 + h)),
                  pl.BlockSpec((seq, RET_K_DIM), lambda b, h: (b, kb0 + h)),
                  pl.BlockSpec((seq, RET_V_DIM), lambda b, h: (b, vb0 + h)),
                  pl.BlockSpec((seq, RET_V_DIM), lambda b, h: (b, gb0 + h)),
                  pl.BlockSpec((1, SUBLANES, LANES), lambda b, h: (h, 0, 0))],
        out_specs=pl.BlockSpec((seq, RET_V_DIM), lambda b, h: (b, h)),
        out_shape=jax.ShapeDtypeStruct((n, RET_WIDTH), BF16),
        scratch_shapes=[pltpu.VMEM((RET_K_DIM, RET_V_DIM), F32)],
        compiler_params=_cparams(("arbitrary", "arbitrary")),
        name="retention",
    )(qkv, qkv, qkv, qkv, jnp.asarray(lg_tab))


def _layer_norm(z, g, b):
    mu = jnp.mean(z, axis=-1, keepdims=True)
    zc = z - mu
    var = jnp.mean(zc * zc, axis=-1, keepdims=True)
    return zc * lax.rsqrt(var + LN_EPS) * g + b


def _split_bf16(a):
    hi = a.astype(BF16)
    lo = (a - hi.astype(F32)).astype(BF16)
    return hi, lo


def _pack_bf16_pairs(a):
    bits = pltpu.bitcast(a.astype(BF16).astype(F32), jnp.uint32)
    return (bits[:, 0:PK_W] >> 16) | (bits[:, PK_W:2 * PK_W] & jnp.uint32(0xFFFF0000))


def _unpack_bf16_pairs(w):
    lo = pltpu.bitcast(w << 16, F32).astype(BF16)
    hi = pltpu.bitcast(w & jnp.uint32(0xFFFF0000), F32).astype(BF16)
    return lo, hi


def _out_proj_kernel(oda_ref, or_ref, x_ref, w_ref, g_ref, b_ref, wr_ref, br_ref,
                     x1_ref, x1p_ref, route_ref, cnt_ref, run_ref):
    i = pl.program_id(0)
    tm = x_ref.shape[0]

    @pl.when(i == 0)
    def _():
        run_ref[...] = jnp.zeros(run_ref.shape, F32)

    h = jnp.dot(oda_ref[...], w_ref[0:DA_WIDTH, :], preferred_element_type=F32)
    h = h + jnp.dot(or_ref[...], w_ref[DA_WIDTH:DA_WIDTH + RET_WIDTH, :], preferred_element_type=F32)
    x1 = _layer_norm(DEEPNORM_ALPHA * x_ref[...] + h, g_ref[...], b_ref[...])
    x1_ref[...] = x1
    packed = _pack_bf16_pairs(x1)
    for c in range(PK_R):
        x1p_ref[pl.ds(c, tm, stride=PK_R), :] = packed[:, c * LANES:(c + 1) * LANES]

    xh, xl = _split_bf16(x1)
    wh, wl = _split_bf16(wr_ref[...])
    logits = (jnp.dot(xh, wh, preferred_element_type=F32) + jnp.dot(xl, wh, preferred_element_type=F32)
              + jnp.dot(xh, wl, preferred_element_type=F32)) + br_ref[...]
    lane = lax.broadcasted_iota(jnp.int32, logits.shape, 1).astype(F32)
    neg = jnp.float32(-jnp.inf)
    big = jnp.float32(1 << 20)
    is_g = lane < N_GROUPS
    gl = jnp.where(is_g, logits, neg)
    gmax = jnp.max(gl, axis=-1, keepdims=True)
    g_idx = jnp.min(jnp.where(is_g & (gl == gmax), lane, big), axis=-1, keepdims=True)
    g_w = 1.0 / jnp.sum(jnp.where(is_g, jnp.exp(gl - gmax), 0.0), axis=-1, keepdims=True)
    e_lane = lane - N_GROUPS
    in_grp = (e_lane >= g_idx * EXPERTS_PER_GROUP) & (e_lane < (g_idx + 1.0) * EXPERTS_PER_GROUP)
    el = jnp.where(in_grp, logits, neg)
    v1 = jnp.max(el, axis=-1, keepdims=True)
    l1 = jnp.min(jnp.where(in_grp & (el == v1), lane, big), axis=-1, keepdims=True)
    el2 = jnp.where(lane == l1, neg, el)
    v2 = jnp.max(el2, axis=-1, keepdims=True)
    l2 = jnp.min(jnp.where(in_grp & (lane != l1) & (el2 == v2), lane, big), axis=-1, keepdims=True)
    t = jnp.exp(v2 - v1)
    fw1 = g_w / (1.0 + t)
    fw2 = g_w * t / (1.0 + t)
    e1 = l1 - N_GROUPS
    e2 = l2 - N_GROUPS

    oh1 = (lane == e1).astype(F32)
    oh2 = (lane == e2).astype(F32)
    cnt = oh1 + oh2
    r_i = lax.broadcasted_iota(jnp.int32, (tm, tm), 0)
    c_i = lax.broadcasted_iota(jnp.int32, (tm, tm), 1)
    tri = (c_i < r_i).astype(BF16)
    before = jnp.dot(tri, cnt.astype(BF16), preferred_element_type=F32) + run_ref[...]
    rank1 = jnp.sum(oh1 * before, axis=-1, keepdims=True)
    rank2 = jnp.sum(oh2 * before, axis=-1, keepdims=True)
    run_new = run_ref[...] + jnp.sum(cnt, axis=0, keepdims=True)
    run_ref[...] = run_new
    cnt_ref[...] = jnp.broadcast_to(run_new, cnt_ref.shape)

    route = jnp.where(lane == 0, e1, 0.0)
    route = jnp.where(lane == 1, e2, route)
    route = jnp.where(lane == 2, rank1, route)
    route = jnp.where(lane == 3, rank2, route)
    route = jnp.where(lane == 4, fw1, route)
    route = jnp.where(lane == 5, fw2, route)
    route_ref[...] = route


def _out_proj(oda, o_r, x2, w_out_b, ln_g, ln_b, w_router, b_router):
    n = x2.shape[0]
    tm = TM_ROW
    row = lambda w: pl.BlockSpec((1, w), lambda i: (0, 0))
    return pl.pallas_call(
        _out_proj_kernel,
        grid=(n // tm,),
        in_specs=[pl.BlockSpec((tm, DA_WIDTH), lambda i: (i, 0)),
                  pl.BlockSpec((tm, RET_WIDTH), lambda i: (i, 0)),
                  pl.BlockSpec((tm, D_MODEL), lambda i: (i, 0)),
                  pl.BlockSpec((D_MODEL, D_MODEL), lambda i: (0, 0)),
                  row(D_MODEL), row(D_MODEL),
                  pl.BlockSpec((D_MODEL, LANES), lambda i: (0, 0)),
                  row(LANES)],
        out_specs=[pl.BlockSpec((tm, D_MODEL), lambda i: (i, 0)),
                   pl.BlockSpec((tm * PK_R, LANES), lambda i: (i, 0)),
                   pl.BlockSpec((tm, LANES), lambda i: (i, 0)),
                   pl.BlockSpec((SUBLANES, LANES), lambda i: (0, 0))],
        out_shape=[jax.ShapeDtypeStruct((n, D_MODEL), F32),
                   jax.ShapeDtypeStruct((n * PK_R, LANES), jnp.uint32),
                   jax.ShapeDtypeStruct((n, LANES), F32),
                   jax.ShapeDtypeStruct((SUBLANES, LANES), F32)],
        scratch_shapes=[pltpu.VMEM((1, LANES), F32)],
        compiler_params=_cparams(("arbitrary",), VMEM_LIMIT),
        name="out_proj_ln1_router",
    )(oda, o_r, x2, w_out_b, ln_g, ln_b, w_router, b_router)


def _dispatch_kernel(pos1_ref, pos2_ref, x1p_ref, xs_in_ref, xs_ref, sem):
    del xs_in_ref
    i = pl.program_id(0)
    tm = pos1_ref.shape[0]

    def row_copy(tok, dst_row):
        return pltpu.make_async_copy(x1p_ref.at[pl.ds(tok * PK_R, PK_R), :],
                                     xs_ref.at[pl.ds(dst_row * PK_R, PK_R), :], sem)

    def issue(r, carry):
        tok = i * tm + r
        row_copy(tok, pos1_ref[r]).start()
        row_copy(tok, pos2_ref[r]).start()
        return carry

    lax.fori_loop(0, tm, issue, 0)
    pltpu.make_async_copy(x1p_ref.at[pl.ds(0, 2 * tm * PK_R), :],
                          xs_ref.at[pl.ds(0, 2 * tm * PK_R), :], sem).wait()


def _dispatch(pos1, pos2, x1p, p_rows):
    n = pos1.shape[0]
    tm = TM_DISP
    xs0 = jnp.zeros((p_rows * PK_R, LANES), jnp.uint32)
    smem = pl.BlockSpec((tm,), lambda i: (i,), memory_space=pltpu.SMEM)
    return pl.pallas_call(
        _dispatch_kernel,
        grid=(n // tm,),
        in_specs=[smem, smem, pl.BlockSpec(memory_space=pl.ANY), pl.BlockSpec(memory_space=pl.ANY)],
        out_specs=pl.BlockSpec(memory_space=pl.ANY),
        out_shape=jax.ShapeDtypeStruct((p_rows * PK_R, LANES), jnp.uint32),
        scratch_shapes=[pltpu.SemaphoreType.DMA],
        input_output_aliases={3: 0},
        compiler_params=_cparams(("arbitrary",)),
        name="moe_dispatch",
    )(pos1, pos2, x1p, xs0)


def _expert_ffn_kernel(te_ref, nt_ref, xs_ref, wg_ref, wu_ref, wd_ref, ys_ref, wgb_ref, wub_ref, wdb_ref):
    i = pl.program_id(0)
    tm = TM_EXP
    active = i < nt_ref[0]
    prev = te_ref[jnp.maximum(i - 1, 0)]
    new_expert = (i == 0) | (te_ref[i] != prev)

    @pl.when(active & new_expert)
    def _():
        wgb_ref[...] = wg_ref[0].astype(BF16)
        wub_ref[...] = wu_ref[0].astype(BF16)
        wdb_ref[...] = wd_ref[0].astype(BF16)

    @pl.when(active)
    def _():
        lo, hi = [], []
        for c in range(PK_R):
            a, b = _unpack_bf16_pairs(xs_ref[pl.ds(c, tm, stride=PK_R), :])
            lo.append(a)
            hi.append(b)
        x = jnp.concatenate(lo + hi, axis=1)
        g = jnp.dot(x, wgb_ref[...], preferred_element_type=F32)
        u = jnp.dot(x, wub_ref[...], preferred_element_type=F32)
        hmid = (g * jax.nn.sigmoid(g) * u).astype(BF16)
        y = jnp.dot(hmid, wdb_ref[...], preferred_element_type=F32)
        for c in range(Y_R):
            ys_ref[pl.ds(c, tm, stride=Y_R), :] = y[:, c * LANES:(c + 1) * LANES]

    @pl.when(jnp.logical_not(active))
    def _():
        ys_ref[...] = jnp.zeros(ys_ref.shape, F32)


def _expert_ffn(tile_expert, n_tiles, xs, w_gate, w_up, w_down, t_max):
    tm = TM_EXP
    clamp = lambda i, te, nt: jnp.minimum(i, nt[0] - 1)
    grid_spec = pltpu.PrefetchScalarGridSpec(
        num_scalar_prefetch=2,
        grid=(t_max,),
        in_specs=[pl.BlockSpec((tm * PK_R, LANES), lambda i, te, nt: (clamp(i, te, nt), 0)),
                  pl.BlockSpec((1, D_MODEL, EXPERT_FF), lambda i, te, nt: (te[i], 0, 0)),
                  pl.BlockSpec((1, D_MODEL, EXPERT_FF), lambda i, te, nt: (te[i], 0, 0)),
                  pl.BlockSpec((1, EXPERT_FF, D_MODEL), lambda i, te, nt: (te[i], 0, 0))],
        out_specs=pl.BlockSpec((tm * Y_R, LANES), lambda i, te, nt: (i, 0)),
        scratch_shapes=[pltpu.VMEM((D_MODEL, EXPERT_FF), BF16),
                        pltpu.VMEM((D_MODEL, EXPERT_FF), BF16),
                        pltpu.VMEM((EXPERT_FF, D_MODEL), BF16)],
    )
    return pl.pallas_call(
        _expert_ffn_kernel,
        grid_spec=grid_spec,
        out_shape=jax.ShapeDtypeStruct((t_max * tm * Y_R, LANES), F32),
        compiler_params=_cparams(("arbitrary",), VMEM_LIMIT),
        name="moe_expert_ffn",
    )(tile_expert, n_tiles, xs, w_gate, w_up, w_down)


def _combine_kernel(pos1_ref, pos2_ref, x1_ref, route_ref, ys_ref, p_ref, wpg_ref, wpp_ref, g_ref, b_ref,
                    o_ref, y1_ref, y2_ref, sem):
    tm = x1_ref.shape[0]

    def row_copy(src_row, dst, r):
        return pltpu.make_async_copy(ys_ref.at[pl.ds(src_row * Y_R, Y_R), :],
                                     dst.at[pl.ds(r * Y_R, Y_R), :], sem)

    def issue(r, carry):
        row_copy(pos1_ref[r], y1_ref, r).start()
        row_copy(pos2_ref[r], y2_ref, r).start()
        return carry

    lax.fori_loop(0, tm, issue, 0)
    pltpu.make_async_copy(ys_ref.at[pl.ds(0, tm * Y_R), :], y1_ref, sem).wait()
    pltpu.make_async_copy(ys_ref.at[pl.ds(0, tm * Y_R), :], y2_ref, sem).wait()

    route = route_ref[...]
    fw1 = route[:, 4:5]
    fw2 = route[:, 5:6]
    cols = []
    for c in range(Y_R):
        cols.append(fw1 * y1_ref[pl.ds(c, tm, stride=Y_R), :] + fw2 * y2_ref[pl.ds(c, tm, stride=Y_R), :])
    m = jnp.concatenate(cols, axis=1)
    x2 = _layer_norm(DEEPNORM_ALPHA * x1_ref[...] + m, g_ref[...], b_ref[...])
    gate = jax.nn.sigmoid(jnp.dot(x2.astype(BF16), wpg_ref[...], preferred_element_type=F32))
    proj = jnp.dot(p_ref[...].astype(BF16), wpp_ref[...], preferred_element_type=F32)
    o_ref[...] = x2 + gate * proj


def _combine(pos1, pos2, x1, route, ys, p2, w_ple_gate_b, w_ple_proj_b, ln_g, ln_b):
    n = x1.shape[0]
    tm = TM_ROW
    smem = pl.BlockSpec((tm,), lambda i: (i,), memory_space=pltpu.SMEM)
    row = lambda w: pl.BlockSpec((1, w), lambda i: (0, 0))
    return pl.pallas_call(
        _combine_kernel,
        grid=(n // tm,),
        in_specs=[smem, smem,
                  pl.BlockSpec((tm, D_MODEL), lambda i: (i, 0)),
                  pl.BlockSpec((tm, LANES), lambda i: (i, 0)),
                  pl.BlockSpec(memory_space=pl.ANY),
                  pl.BlockSpec((tm, PLE_DIM), lambda i: (i, 0)),
                  pl.BlockSpec((D_MODEL, D_MODEL), lambda i: (0, 0)),
                  pl.BlockSpec((PLE_DIM, D_MODEL), lambda i: (0, 0)),
                  row(D_MODEL), row(D_MODEL)],
        out_specs=pl.BlockSpec((tm, D_MODEL), lambda i: (i, 0)),
        out_shape=jax.ShapeDtypeStruct((n, D_MODEL), F32),
        scratch_shapes=[pltpu.VMEM((tm * Y_R, LANES), F32), pltpu.VMEM((tm * Y_R, LANES), F32),
                        pltpu.SemaphoreType.DMA],
        compiler_params=_cparams(("arbitrary",), VMEM_LIMIT),
        name="moe_combine_ln2_ple",
    )(pos1, pos2, x1, route, ys, p2, w_ple_gate_b, w_ple_proj_b, ln_g, ln_b)


def _routing_plan(route, counts):
    tm = TM_EXP
    n = route.shape[0]
    t_max = (2 * n) // tm + N_EXPERTS
    cnt = counts[0, :N_EXPERTS].astype(jnp.int32)
    padded = ((cnt + tm - 1) // tm) * tm
    ends = jnp.cumsum(padded)
    offs = ends - padded
    e1 = route[:, 0].astype(jnp.int32)
    e2 = route[:, 1].astype(jnp.int32)
    pos1 = offs[e1] + route[:, 2].astype(jnp.int32)
    pos2 = offs[e2] + route[:, 3].astype(jnp.int32)
    n_tiles = (ends[-1] // tm).astype(jnp.int32)
    tile_start = jnp.arange(t_max, dtype=jnp.int32) * tm
    te = jnp.sum((ends[None, :] <= tile_start[:, None]).astype(jnp.int32), axis=1)
    te_last = jnp.sum((ends <= (n_tiles - 1) * tm).astype(jnp.int32))
    te = jnp.where(jnp.arange(t_max) < n_tiles, te, te_last)
    te = jnp.minimum(te, N_EXPERTS - 1).astype(jnp.int32)
    return pos1, pos2, te, n_tiles.reshape(1), t_max


def kernel(x, p, positions, w_in, w_out, da_lambda_q1, da_lambda_k1, da_lambda_q2, da_lambda_k2, da_subln_w, ln1_g, ln1_b, w_router_group, b_router_group, w_router_expert, b_router_expert, w_exp_gate, w_exp_up, w_exp_down, ln2_g, ln2_b, w_ple_gate, w_ple_proj):
    batch, seq, d = x.shape
    n = batch * seq
    assert d == D_MODEL and w_in.shape[0] == DEPTH == 1
    l = 0
    lam_init = 0.8 - 0.6 * math.exp(-0.3 * l)
    x2 = x.reshape(n, d)

    tab = _rope_tables(positions)
    qkv = _in_proj(x2, w_in[l].astype(BF16), tab)
    oda = _diff_attn(qkv, da_lambda_q1[l], da_lambda_k1[l], da_lambda_q2[l], da_lambda_k2[l],
                     da_subln_w[l], lam_init, batch, seq)
    o_r = _retention(qkv, batch, seq)

    pad = LANES - N_GROUPS - N_EXPERTS
    w_router = jnp.concatenate([w_router_group[l], w_router_expert[l], jnp.zeros((d, pad), F32)], axis=1)
    b_router = jnp.concatenate([b_router_group[l], b_router_expert[l], jnp.zeros((pad,), F32)]).reshape(1, LANES)
    x1, x1p, route, counts = _out_proj(oda, o_r, x2, w_out[l].astype(BF16),
                                       ln1_g[l].reshape(1, d), ln1_b[l].reshape(1, d), w_router, b_router)

    pos1, pos2, tile_expert, n_tiles, t_max = _routing_plan(route, counts)
    xs = _dispatch(pos1, pos2, x1p, t_max * TM_EXP)
    ys = _expert_ffn(tile_expert, n_tiles, xs, w_exp_gate[l], w_exp_up[l], w_exp_down[l], t_max)
    out = _combine(pos1, pos2, x1, route, ys, p[l].reshape(n, PLE_DIM),
                   w_ple_gate[l].astype(BF16), w_ple_proj[l].astype(BF16),
                   ln2_g[l].reshape(1, d), ln2_b[l].reshape(1, d))
    return out.reshape(batch, seq, d)
```

```python
import functools
import math

import numpy as np
import jax
import jax.numpy as jnp
from jax import lax
from jax.experimental import pallas as pl
from jax.experimental.pallas import tpu as pltpu

D_MODEL = 2048
DA_HEADS = 8
DA_HEAD_DIM = 64
DA_V_DIM = 128
RET_HEADS = 4
RET_K_DIM = 128
RET_V_DIM = 256
RET_CHUNK = 128
ROPE_THETA = 10000.0
PLE_DIM = 256
N_GROUPS = 4
EXPERTS_PER_GROUP = 8
N_EXPERTS = 32
EXPERT_FF = 512
DEPTH = 1
DEEPNORM_ALPHA = (2 * DEPTH) ** 0.25
LN_EPS = 1e-5
IN_WIDTH = 6144
DA_WIDTH = 1024
RET_WIDTH = 1024

OFF_QDA, OFF_KDA, OFF_VDA, OFF_QR, OFF_KR, OFF_VR, OFF_GR = 0, 1024, 2048, 3072, 3584, 4096, 5120

LANES = 128
SUBLANES = 8
VMEM_LIMIT = 56 * 1024 * 1024

BF16 = jnp.bfloat16
F32 = jnp.float32

TM_PROJ = 512
TN_PROJ = 512
TQ = 256
TM_ROW = 256
TM_EXP = 256


def _cparams(sem, vmem=None):
    return pltpu.CompilerParams(dimension_semantics=sem, vmem_limit_bytes=vmem)


def _rope_table_kernel(pos_ref, inv_ref, sgn_ref, tab_ref):
    pos = pos_ref[...].astype(F32)
    ang = pos * inv_ref[...]
    c = jnp.cos(ang)
    s = jnp.sin(ang)
    lane = lax.broadcasted_iota(jnp.int32, ang.shape, 1)
    low = lane < 64
    cr = pltpu.roll(c, 64, 1)
    sr = pltpu.roll(s, 64, 1)
    sgn = sgn_ref[...]
    tab_ref[:, 0:128] = jnp.where(low, cr, c)
    tab_ref[:, 128:256] = jnp.where(low, sr, s) * sgn[0:1, :]
    tab_ref[:, 256:384] = jnp.where(low, c, cr)
    tab_ref[:, 384:512] = jnp.where(low, s, sr) * sgn[1:2, :]


def _rope_tables(positions):
    n = positions.size
    tm = 1024
    i128 = np.power(ROPE_THETA, -np.arange(0, 128, 2, dtype=np.float64) / 128)
    i64 = np.power(ROPE_THETA, -np.arange(0, 64, 2, dtype=np.float64) / 64)
    inv = np.concatenate([i128, i64, i64]).astype(np.float32)[None, :]
    lane = np.arange(128)
    sgn = np.stack([np.where(lane % 64 < 32, -1.0, 1.0), np.where(lane < 64, -1.0, 1.0)]).astype(np.float32)
    sgn = np.concatenate([sgn, np.zeros((6, 128), np.float32)])
    return pl.pallas_call(
        _rope_table_kernel,
        grid=(n // tm,),
        in_specs=[pl.BlockSpec((tm, 1), lambda i: (i, 0)),
                  pl.BlockSpec((1, 128), lambda i: (0, 0)),
                  pl.BlockSpec((8, 128), lambda i: (0, 0))],
        out_specs=pl.BlockSpec((tm, 512), lambda i: (i, 0)),
        out_shape=jax.ShapeDtypeStruct((n, 512), F32),
        compiler_params=_cparams(("arbitrary",)),
        name="rope_tables",
    )(positions.reshape(n, 1), jnp.asarray(inv), jnp.asarray(sgn))


def _rope_cols(acc, cos, sin, half, scale):
    outs = []
    lane = lax.broadcasted_iota(jnp.int32, (acc.shape[0], LANES), 1)
    for c in range(acc.shape[1] // LANES):
        t = acc[:, c * LANES:(c + 1) * LANES]
        if half == 64:
            partner = pltpu.roll(t, 64, 1)
        else:
            partner = jnp.where((lane & 32) == 0, pltpu.roll(t, 96, 1), pltpu.roll(t, 32, 1))
        o = t * cos + partner * sin
        if scale != 1.0:
            o = o * scale
        outs.append(o)
    return jnp.concatenate(outs, axis=1)


def _in_proj_kernel(x_ref, w_ref, tab_ref, o_ref, xb_ref):
    j = pl.program_id(1)

    @pl.when(j == 0)
    def _():
        xb_ref[...] = x_ref[...].astype(BF16)

    acc = jnp.dot(xb_ref[...], w_ref[...], preferred_element_type=F32)
    n_da = (OFF_VDA) // TN_PROJ
    j_qr = OFF_QR // TN_PROJ
    j_kr = OFF_KR // TN_PROJ

    @pl.when(j < n_da)
    def _():
        o_ref[...] = _rope_cols(acc, tab_ref[:, 0:128], tab_ref[:, 128:256], 32, 1.0).astype(BF16)

    @pl.when(j == j_qr)
    def _():
        o_ref[...] = _rope_cols(acc, tab_ref[:, 256:384], tab_ref[:, 384:512], 64, 1.0).astype(BF16)

    @pl.when(j == j_kr)
    def _():
        o_ref[...] = _rope_cols(acc, tab_ref[:, 256:384], tab_ref[:, 384:512], 64,
                                RET_K_DIM ** -0.5).astype(BF16)

    @pl.when((j >= n_da) & (j != j_qr) & (j != j_kr))
    def _():
        o_ref[...] = acc.astype(BF16)


def _in_proj(x2, w_in_b, tab):
    n = x2.shape[0]
    return pl.pallas_call(
        _in_proj_kernel,
        grid=(n // TM_PROJ, IN_WIDTH // TN_PROJ),
        in_specs=[pl.BlockSpec((TM_PROJ, D_MODEL), lambda i, j: (i, 0)),
                  pl.BlockSpec((D_MODEL, TN_PROJ), lambda i, j: (0, j)),
                  pl.BlockSpec((TM_PROJ, 512), lambda i, j: (i, 0))],
        out_specs=pl.BlockSpec((TM_PROJ, TN_PROJ), lambda i, j: (i, j)),
        out_shape=jax.ShapeDtypeStruct((n, IN_WIDTH), BF16),
        scratch_shapes=[pltpu.VMEM((TM_PROJ, D_MODEL), BF16)],
        compiler_params=_cparams(("arbitrary", "arbitrary"), VMEM_LIMIT),
        name="in_proj",
    )(x2, w_in_b, tab)


def _diff_attn_kernel(lam_init, q_ref, k_ref, v_ref, lq1_ref, lk1_ref, lq2_ref, lk2_ref, sw_ref,
                      o_ref, m_ref, l_ref, acc_ref):
    seq = q_ref.shape[0]
    lam = (jnp.exp(jnp.sum(lq1_ref[...] * lk1_ref[...], axis=-1, keepdims=True))
           - jnp.exp(jnp.sum(lq2_ref[...] * lk2_ref[...], axis=-1, keepdims=True)) + lam_init)
    scale = jnp.asarray(DA_HEAD_DIM ** -0.5, BF16)
    lane = lax.broadcasted_iota(jnp.int32, (TQ, LANES), 1)
    row_i = lax.broadcasted_iota(jnp.int32, (2 * TQ, TQ), 0)
    col_i = lax.broadcasted_iota(jnp.int32, (2 * TQ, TQ), 1)
    causal = col_i <= jnp.where(row_i >= TQ, row_i - TQ, row_i)

    def step(qs, start, width, masked):
        k = k_ref[pl.ds(start, width), :]
        v = v_ref[pl.ds(start, width), :]
        s = lax.dot_general(qs, k, (((1,), (1,)), ((), ())), preferred_element_type=F32)
        if masked:
            s = jnp.where(causal, s, jnp.finfo(F32).min)
        chunks = [s[:, c * LANES:(c + 1) * LANES] for c in range(width // LANES)]
        m_old = m_ref[...]
        m_new = jnp.maximum(m_old, jnp.max(functools.reduce(jnp.maximum, chunks), axis=-1, keepdims=True))
        alpha = jnp.exp(m_old - m_new)
        p = jnp.concatenate([jnp.exp(c - m_new).astype(BF16) for c in chunks], axis=1)
        v_ext = jnp.concatenate([v, jnp.ones_like(v)], axis=1)
        pv = jnp.dot(p, v_ext, preferred_element_type=F32)
        acc_ref[...] = alpha * acc_ref[...] + pv[:, 0:LANES]
        l_ref[...] = alpha * l_ref[...] + pv[:, LANES:2 * LANES]
        m_ref[...] = m_new

    for qi in range(seq // TQ):
        q = q_ref[qi * TQ:(qi + 1) * TQ, :]
        zero = jnp.zeros_like(q)
        qs = jnp.concatenate([jnp.where(lane < 64, q, zero), jnp.where(lane >= 64, q, zero)], axis=0) * scale
        m_ref[...] = jnp.full(m_ref.shape, -jnp.inf, F32)
        l_ref[...] = jnp.zeros(l_ref.shape, F32)
        acc_ref[...] = jnp.zeros(acc_ref.shape, F32)

        def pair(j, carry, qs=qs):
            step(qs, pl.multiple_of(j * (2 * TQ), 2 * TQ), 2 * TQ, False)
            return carry

        lax.fori_loop(0, qi // 2, pair, 0)
        if qi % 2:
            step(qs, (qi - 1) * TQ, TQ, False)
        step(qs, qi * TQ, TQ, True)

        a = acc_ref[...] / l_ref[...]
        o = a[0:TQ, :] - lam * a[TQ:2 * TQ, :]
        o = o * lax.rsqrt(jnp.mean(o * o, axis=-1, keepdims=True) + LN_EPS)
        o = o * sw_ref[...] * (1.0 - lam_init)
        o_ref[qi * TQ:(qi + 1) * TQ, :] = o.astype(BF16)


def _diff_attn(qkv, lq1, lk1, lq2, lk2, subln_w, lam_init, batch, seq):
    n = qkv.shape[0]
    qb0 = OFF_QDA // LANES
    kb0 = OFF_KDA // LANES
    vb0 = OFF_VDA // LANES
    vec = lambda a: a.reshape(1, -1).astype(F32)
    small = lambda w: pl.BlockSpec((1, w), lambda b, h: (0, 0))
    return pl.pallas_call(
        functools.partial(_diff_attn_kernel, lam_init),
        grid=(batch, DA_HEADS),
        in_specs=[pl.BlockSpec((seq, LANES), lambda b, h: (b, qb0 + h)),
                  pl.BlockSpec((seq, LANES), lambda b, h: (b, kb0 + h)),
                  pl.BlockSpec((seq, LANES), lambda b, h: (b, vb0 + h)),
                  small(64), small(64), small(64), small(64), small(128)],
        out_specs=pl.BlockSpec((seq, LANES), lambda b, h: (b, h)),
        out_shape=jax.ShapeDtypeStruct((n, DA_WIDTH), BF16),
        scratch_shapes=[pltpu.VMEM((2 * TQ, LANES), F32), pltpu.VMEM((2 * TQ, LANES), F32),
                        pltpu.VMEM((2 * TQ, LANES), F32)],
        compiler_params=_cparams(("arbitrary", "arbitrary")),
        name="diff_attn",
    )(qkv, qkv, qkv, vec(lq1), vec(lk1), vec(lq2), vec(lk2), vec(subln_w))


def _retention_kernel(q_ref, k_ref, v_ref, g_ref, lg_ref, o_ref, state_ref):
    C = RET_CHUNK
    lg = lg_ref[0, 0:1, :]
    lg_col = lg[:, 0:1]
    n_i = lax.broadcasted_iota(jnp.int32, (C, C), 0)
    m_i = lax.broadcasted_iota(jnp.int32, (C, C), 1)
    rel = (n_i - m_i).astype(F32)
    decay = jnp.where(rel >= 0, jnp.exp(rel * lg_col), 0.0)
    n_col = lax.broadcasted_iota(jnp.int32, (C, 1), 0).astype(F32)
    zeta = jnp.exp((C - 1.0 - n_col) * lg_col)
    xi = jnp.exp((n_col + 1.0) * lg_col)
    chunk_decay = jnp.exp(C * lg_col)
    state_ref[...] = jnp.zeros(state_ref.shape, F32)

    def body(c, carry):
        start = pl.multiple_of(c * C, C)
        q = q_ref[pl.ds(start, C), :]
        k = k_ref[pl.ds(start, C), :]
        v = v_ref[pl.ds(start, C), :]
        g = g_ref[pl.ds(start, C), :].astype(F32)
        qk = lax.dot_general(q, k, (((1,), (1,)), ((), ())), preferred_element_type=F32)
        inner = (qk * decay).astype(BF16)
        inner_o = jnp.dot(inner, v, preferred_element_type=F32)
        state = state_ref[...]
        cross = jnp.dot(q, state.astype(BF16), preferred_element_type=F32)
        y = inner_o + cross * xi
        kz_t = (k.astype(F32) * zeta).T.astype(BF16)
        kv = jnp.dot(kz_t, v, preferred_element_type=F32)
        state_ref[...] = chunk_decay * state + kv
        mu = jnp.mean(y, axis=-1, keepdims=True)
        yc = y - mu
        var = jnp.mean(yc * yc, axis=-1, keepdims=True)
        yn = yc * lax.rsqrt(var + LN_EPS)
        o_ref[pl.ds(start, C), :] = (g * jax.nn.sigmoid(g) * yn).astype(BF16)
        return carry

    lax.fori_loop(0, q_ref.shape[0] // C, body, 0)


def _retention(qkv, batch, seq):
    n = qkv.shape[0]
    lg = np.log(1.0 - np.power(2.0, -5.0 - np.arange(RET_HEADS, dtype=np.float64)))
    lg_tab = np.broadcast_to(lg[:, None, None], (RET_HEADS, SUBLANES, LANES)).astype(np.float32)
    qb0 = OFF_QR // RET_K_DIM
    kb0 = OFF_KR // RET_K_DIM
    vb0 = OFF_VR // RET_V_DIM
    gb0 = OFF_GR // RET_V_DIM
    return pl.pallas_call(
        _retention_kernel,
        grid=(batch, RET_HEADS),
        in_specs=[pl.BlockSpec((seq, RET_K_DIM), lambda b, h: (b, qb0 + h)),
                  pl.BlockSpec((seq, RET_K_DIM), lambda b, h: (b, kb0 + h)),
                  pl.BlockSpec((seq, RET_V_DIM), lambda b, h: (b, vb0 + h)),
                  pl.BlockSpec((seq, RET_V_DIM), lambda b, h: (b, gb0 + h)),
                  pl.BlockSpec((1, SUBLANES, LANES), lambda b, h: (h, 0, 0))],
        out_specs=pl.BlockSpec((seq, RET_V_DIM), lambda b, h: (b, h)),
        out_shape=jax.ShapeDtypeStruct((n, RET_WIDTH), BF16),
        scratch_shapes=[pltpu.VMEM((RET_K_DIM, RET_V_DIM), F32)],
        compiler_params=_cparams(("arbitrary", "arbitrary")),
        name="retention",
    )(qkv, qkv, qkv, qkv, jnp.asarray(lg_tab))


def _layer_norm(z, g, b):
    mu = jnp.mean(z, axis=-1, keepdims=True)
    zc = z - mu
    var = jnp.mean(zc * zc, axis=-1, keepdims=True)
    return zc * lax.rsqrt(var + LN_EPS) * g + b


def _split_bf16(a):
    hi = a.astype(BF16)
    lo = (a - hi.astype(F32)).astype(BF16)
    return hi, lo


def _out_proj_kernel(oda_ref, or_ref, x_ref, w_ref, g_ref, b_ref, wr_ref, br_ref,
                     x1_ref, route_ref, cnt_ref, run_ref):
    i = pl.program_id(0)
    tm = x_ref.shape[0]

    @pl.when(i == 0)
    def _():
        run_ref[...] = jnp.zeros(run_ref.shape, F32)

    h = jnp.dot(oda_ref[...], w_ref[0:DA_WIDTH, :], preferred_element_type=F32)
    h = h + jnp.dot(or_ref[...], w_ref[DA_WIDTH:DA_WIDTH + RET_WIDTH, :], preferred_element_type=F32)
    x1 = _layer_norm(DEEPNORM_ALPHA * x_ref[...] + h, g_ref[...], b_ref[...])
    x1_ref[...] = x1

    xh, xl = _split_bf16(x1)
    wh, wl = _split_bf16(wr_ref[...])
    logits = (jnp.dot(xh, wh, preferred_element_type=F32) + jnp.dot(xl, wh, preferred_element_type=F32)
              + jnp.dot(xh, wl, preferred_element_type=F32)) + br_ref[...]
    lane = lax.broadcasted_iota(jnp.int32, logits.shape, 1).astype(F32)
    neg = jnp.float32(-jnp.inf)
    big = jnp.float32(1 << 20)
    is_g = lane < N_GROUPS
    gl = jnp.where(is_g, logits, neg)
    gmax = jnp.max(gl, axis=-1, keepdims=True)
    g_idx = jnp.min(jnp.where(is_g & (gl == gmax), lane, big), axis=-1, keepdims=True)
    g_w = 1.0 / jnp.sum(jnp.where(is_g, jnp.exp(gl - gmax), 0.0), axis=-1, keepdims=True)
    e_lane = lane - N_GROUPS
    in_grp = (e_lane >= g_idx * EXPERTS_PER_GROUP) & (e_lane < (g_idx + 1.0) * EXPERTS_PER_GROUP)
    el = jnp.where(in_grp, logits, neg)
    v1 = jnp.max(el, axis=-1, keepdims=True)
    l1 = jnp.min(jnp.where(in_grp & (el == v1), lane, big), axis=-1, keepdims=True)
    el2 = jnp.where(lane == l1, neg, el)
    v2 = jnp.max(el2, axis=-1, keepdims=True)
    l2 = jnp.min(jnp.where(in_grp & (lane != l1) & (el2 == v2), lane, big), axis=-1, keepdims=True)
    t = jnp.exp(v2 - v1)
    fw1 = g_w / (1.0 + t)
    fw2 = g_w * t / (1.0 + t)
    e1 = l1 - N_GROUPS
    e2 = l2 - N_GROUPS

    oh1 = (lane == e1).astype(F32)
    oh2 = (lane == e2).astype(F32)
    cnt = oh1 + oh2
    r_i = lax.broadcasted_iota(jnp.int32, (tm, tm), 0)
    c_i = lax.broadcasted_iota(jnp.int32, (tm, tm), 1)
    tri = (c_i < r_i).astype(BF16)
    before = jnp.dot(tri, cnt.astype(BF16), preferred_element_type=F32) + run_ref[...]
    rank1 = jnp.sum(oh1 * before, axis=-1, keepdims=True)
    rank2 = jnp.sum(oh2 * before, axis=-1, keepdims=True)
    run_new = run_ref[...] + jnp.sum(cnt, axis=0, keepdims=True)
    run_ref[...] = run_new
    cnt_ref[...] = jnp.broadcast_to(run_new, cnt_ref.shape)

    route = jnp.where(lane == 0, e1, 0.0)
    route = jnp.where(lane == 1, e2, route)
    route = jnp.where(lane == 2, rank1, route)
    route = jnp.where(lane == 3, rank2, route)
    route = jnp.where(lane == 4, fw1, route)
    route = jnp.where(lane == 5, fw2, route)
    route_ref[...] = route


def _out_proj(oda, o_r, x2, w_out_b, ln_g, ln_b, w_router, b_router):
    n = x2.shape[0]
    tm = TM_ROW
    row = lambda w: pl.BlockSpec((1, w), lambda i: (0, 0))
    return pl.pallas_call(
        _out_proj_kernel,
        grid=(n // tm,),
        in_specs=[pl.BlockSpec((tm, DA_WIDTH), lambda i: (i, 0)),
                  pl.BlockSpec((tm, RET_WIDTH), lambda i: (i, 0)),
                  pl.BlockSpec((tm, D_MODEL), lambda i: (i, 0)),
                  pl.BlockSpec((D_MODEL, D_MODEL), lambda i: (0, 0)),
                  row(D_MODEL), row(D_MODEL),
                  pl.BlockSpec((D_MODEL, LANES), lambda i: (0, 0)),
                  row(LANES)],
        out_specs=[pl.BlockSpec((tm, D_MODEL), lambda i: (i, 0)),
                   pl.BlockSpec((tm, LANES), lambda i: (i, 0)),
                   pl.BlockSpec((SUBLANES, LANES), lambda i: (0, 0))],
        out_shape=[jax.ShapeDtypeStruct((n, D_MODEL), F32),
                   jax.ShapeDtypeStruct((n, LANES), F32),
                   jax.ShapeDtypeStruct((SUBLANES, LANES), F32)],
        scratch_shapes=[pltpu.VMEM((1, LANES), F32)],
        compiler_params=_cparams(("arbitrary",), VMEM_LIMIT),
        name="out_proj_ln1_router",
    )(oda, o_r, x2, w_out_b, ln_g, ln_b, w_router, b_router)


def _slot_kernel(route_ref, cnt_ref, pos_ref):
    route = route_ref[...]
    lane = lax.broadcasted_iota(jnp.int32, (1, LANES), 1)
    cnt = jnp.where(lane < N_EXPERTS, cnt_ref[0:1, :], 0.0)
    padded = jnp.floor((cnt + (TM_EXP - 1)) * (1.0 / TM_EXP)) * TM_EXP
    ends = padded
    for sh in (1, 2, 4, 8, 16):
        ends = ends + jnp.where(lane >= sh, pltpu.roll(ends, sh, 1), 0.0)
    offs = ends - padded
    lane_f = lax.broadcasted_iota(jnp.int32, route.shape, 1).astype(F32)
    off1 = jnp.sum(jnp.where(lane_f == route[:, 0:1], offs, 0.0), axis=-1, keepdims=True)
    off2 = jnp.sum(jnp.where(lane_f == route[:, 1:2], offs, 0.0), axis=-1, keepdims=True)
    slots = jnp.where(lane_f == 0.0, off1 + route[:, 2:3], 0.0)
    slots = jnp.where(lane_f == 1.0, off2 + route[:, 3:4], slots)
    pos_ref[...] = slots.T[0:SUBLANES, :].astype(jnp.int32)


def _slots(route, counts):
    n = route.shape[0]
    tm = 1024
    return pl.pallas_call(
        _slot_kernel,
        grid=(n // tm,),
        in_specs=[pl.BlockSpec((tm, LANES), lambda i: (i, 0)),
                  pl.BlockSpec((SUBLANES, LANES), lambda i: (0, 0))],
        out_specs=pl.BlockSpec((SUBLANES, tm), lambda i: (0, i)),
        out_shape=jax.ShapeDtypeStruct((SUBLANES, n), jnp.int32),
        compiler_params=_cparams(("arbitrary",)),
        name="moe_slots",
    )(route, counts)


def _inverse_kernel(pos1_ref, pos2_ref, inv_ref):
    i = pl.program_id(0)
    tm = pos1_ref.shape[0]

    @pl.when(i == 0)
    def _():
        def zero(j, carry):
            inv_ref[j] = 0
            return carry
        lax.fori_loop(0, inv_ref.shape[0], zero, 0, unroll=8)

    def scatter(r, carry):
        tok = i * tm + r
        inv_ref[pos1_ref[r]] = tok
        inv_ref[pos2_ref[r]] = tok
        return carry

    lax.fori_loop(0, tm, scatter, 0, unroll=8)


def _inverse_map(pos1, pos2, p_rows):
    n = pos1.shape[0]
    tm = 1024
    smem = pl.BlockSpec((tm,), lambda i: (i,), memory_space=pltpu.SMEM)
    return pl.pallas_call(
        _inverse_kernel,
        grid=(n // tm,),
        in_specs=[smem, smem],
        out_specs=pl.BlockSpec(memory_space=pltpu.SMEM),
        out_shape=jax.ShapeDtypeStruct((p_rows,), jnp.int32),
        compiler_params=_cparams(("arbitrary",)),
        name="moe_inverse_map",
    )(pos1, pos2)


def _expert_ffn_kernel(te_ref, nt_ref, inv_ref, inv_next_ref, x1_ref, wg_ref, wu_ref, wd_ref,
                       ys_ref, xbuf_ref, wgb_ref, wub_ref, wdb_ref, sem):
    i = pl.program_id(0)
    tm = TM_EXP
    n_tiles = nt_ref[0]
    active = i < n_tiles
    slot = i % 2

    def gather_rows(idx_ref, buf_slot):
        def issue(r, carry):
            pltpu.make_async_copy(x1_ref.at[pl.ds(idx_ref[r], 1), :],
                                  xbuf_ref.at[buf_slot, pl.ds(r, 1), :], sem.at[buf_slot]).start()
            return carry
        lax.fori_loop(0, tm, issue, 0, unroll=8)

    @pl.when(i == 0)
    def _():
        gather_rows(inv_ref, 0)

    @pl.when(i + 1 < n_tiles)
    def _():
        gather_rows(inv_next_ref, 1 - slot)

    prev = te_ref[jnp.maximum(i - 1, 0)]
    new_expert = (i == 0) | (te_ref[i] != prev)

    @pl.when(active & new_expert)
    def _():
        wgb_ref[...] = wg_ref[0].astype(BF16)
        wub_ref[...] = wu_ref[0].astype(BF16)
        wdb_ref[...] = wd_ref[0].astype(BF16)

    @pl.when(active)
    def _():
        pltpu.make_async_copy(x1_ref.at[pl.ds(0, tm), :], xbuf_ref.at[slot], sem.at[slot]).wait()
        x = xbuf_ref[slot].astype(BF16)
        g = jnp.dot(x, wgb_ref[...], preferred_element_type=F32)
        u = jnp.dot(x, wub_ref[...], preferred_element_type=F32)
        hmid = (g * jax.nn.sigmoid(g) * u).astype(BF16)
        ys_ref[...] = jnp.dot(hmid, wdb_ref[...], preferred_element_type=F32)

    @pl.when(jnp.logical_not(active))
    def _():
        ys_ref[...] = jnp.zeros(ys_ref.shape, F32)


def _expert_ffn(tile_expert, n_tiles, inv, x1, w_gate, w_up, w_down, t_max):
    tm = TM_EXP
    grid_spec = pltpu.PrefetchScalarGridSpec(
        num_scalar_prefetch=2,
        grid=(t_max,),
        in_specs=[pl.BlockSpec((tm,), lambda i, te, nt: (i,), memory_space=pltpu.SMEM),
                  pl.BlockSpec((tm,), lambda i, te, nt: (jnp.minimum(i + 1, t_max - 1),),
                               memory_space=pltpu.SMEM),
                  pl.BlockSpec(memory_space=pl.ANY),
                  pl.BlockSpec((1, D_MODEL, EXPERT_FF), lambda i, te, nt: (te[i], 0, 0)),
                  pl.BlockSpec((1, D_MODEL, EXPERT_FF), lambda i, te, nt: (te[i], 0, 0)),
                  pl.BlockSpec((1, EXPERT_FF, D_MODEL), lambda i, te, nt: (te[i], 0, 0))],
        out_specs=pl.BlockSpec((tm, D_MODEL), lambda i, te, nt: (i, 0)),
        scratch_shapes=[pltpu.VMEM((2, tm, D_MODEL), F32),
                        pltpu.VMEM((D_MODEL, EXPERT_FF), BF16),
                        pltpu.VMEM((D_MODEL, EXPERT_FF), BF16),
                        pltpu.VMEM((EXPERT_FF, D_MODEL), BF16),
                        pltpu.SemaphoreType.DMA((2,))],
    )
    return pl.pallas_call(
        _expert_ffn_kernel,
        grid_spec=grid_spec,
        out_shape=jax.ShapeDtypeStruct((t_max * tm, D_MODEL), F32),
        compiler_params=_cparams(("arbitrary",), VMEM_LIMIT),
        name="moe_expert_ffn",
    )(tile_expert, n_tiles, inv, inv, x1, w_gate, w_up, w_down)


def _combine_kernel(pos1_ref, pos2_ref, x1_ref, route_ref, ys_ref, p_ref, wpg_ref, wpp_ref, g_ref, b_ref,
                    o_ref, y1_ref, y2_ref, sem):
    tm = x1_ref.shape[0]

    def row_copy(src_row, dst, r):
        return pltpu.make_async_copy(ys_ref.at[pl.ds(src_row, 1), :], dst.at[pl.ds(r, 1), :], sem)

    def issue(r, carry):
        row_copy(pos1_ref[r], y1_ref, r).start()
        row_copy(pos2_ref[r], y2_ref, r).start()
        return carry

    lax.fori_loop(0, tm, issue, 0, unroll=8)
    pltpu.make_async_copy(ys_ref.at[pl.ds(0, tm), :], y1_ref, sem).wait()
    pltpu.make_async_copy(ys_ref.at[pl.ds(0, tm), :], y2_ref, sem).wait()

    route = route_ref[...]
    m = route[:, 4:5] * y1_ref[...] + route[:, 5:6] * y2_ref[...]
    x2 = _layer_norm(DEEPNORM_ALPHA * x1_ref[...] + m, g_ref[...], b_ref[...])
    gate = jax.nn.sigmoid(jnp.dot(x2.astype(BF16), wpg_ref[...], preferred_element_type=F32))
    proj = jnp.dot(p_ref[...].astype(BF16), wpp_ref[...], preferred_element_type=F32)
    o_ref[...] = x2 + gate * proj


def _combine(pos1, pos2, x1, route, ys, p2, w_ple_gate_b, w_ple_proj_b, ln_g, ln_b):
    n = x1.shape[0]
    tm = TM_ROW
    smem = pl.BlockSpec((tm,), lambda i: (i,), memory_space=pltpu.SMEM)
    row = lambda w: pl.BlockSpec((1, w), lambda i: (0, 0))
    return pl.pallas_call(
        _combine_kernel,
        grid=(n // tm,),
        in_specs=[smem, smem,
                  pl.BlockSpec((tm, D_MODEL), lambda i: (i, 0)),
                  pl.BlockSpec((tm, LANES), lambda i: (i, 0)),
                  pl.BlockSpec(memory_space=pl.ANY),
                  pl.BlockSpec((tm, PLE_DIM), lambda i: (i, 0)),
                  pl.BlockSpec((D_MODEL, D_MODEL), lambda i: (0, 0)),
                  pl.BlockSpec((PLE_DIM, D_MODEL), lambda i: (0, 0)),
                  row(D_MODEL), row(D_MODEL)],
        out_specs=pl.BlockSpec((tm, D_MODEL), lambda i: (i, 0)),
        out_shape=jax.ShapeDtypeStruct((n, D_MODEL), F32),
        scratch_shapes=[pltpu.VMEM((tm, D_MODEL), F32), pltpu.VMEM((tm, D_MODEL), F32),
                        pltpu.SemaphoreType.DMA],
        compiler_params=_cparams(("arbitrary",), VMEM_LIMIT),
        name="moe_combine_ln2_ple",
    )(pos1, pos2, x1, route, ys, p2, w_ple_gate_b, w_ple_proj_b, ln_g, ln_b)


def _tile_plan(counts, t_max):
    tm = TM_EXP
    cnt = counts[0, :N_EXPERTS].astype(jnp.int32)
    ends = jnp.cumsum(((cnt + tm - 1) // tm) * tm)
    n_tiles = (ends[-1] // tm).astype(jnp.int32)
    tile_start = jnp.minimum(jnp.arange(t_max, dtype=jnp.int32), n_tiles - 1) * tm
    te = jnp.sum((ends[None, :] <= tile_start[:, None]).astype(jnp.int32), axis=1)
    return jnp.minimum(te, N_EXPERTS - 1).astype(jnp.int32), n_tiles.reshape(1)


def kernel(x, p, positions, w_in, w_out, da_lambda_q1, da_lambda_k1, da_lambda_q2, da_lambda_k2, da_subln_w, ln1_g, ln1_b, w_router_group, b_router_group, w_router_expert, b_router_expert, w_exp_gate, w_exp_up, w_exp_down, ln2_g, ln2_b, w_ple_gate, w_ple_proj):
    batch, seq, d = x.shape
    n = batch * seq
    assert d == D_MODEL and w_in.shape[0] == DEPTH == 1
    l = 0
    lam_init = 0.8 - 0.6 * math.exp(-0.3 * l)
    x2 = x.reshape(n, d)

    tab = _rope_tables(positions)
    qkv = _in_proj(x2, w_in[l].astype(BF16), tab)
    oda = _diff_attn(qkv, da_lambda_q1[l], da_lambda_k1[l], da_lambda_q2[l], da_lambda_k2[l],
                     da_subln_w[l], lam_init, batch, seq)
    o_r = _retention(qkv, batch, seq)

    pad = LANES - N_GROUPS - N_EXPERTS
    w_router = jnp.concatenate([w_router_group[l], w_router_expert[l], jnp.zeros((d, pad), F32)], axis=1)
    b_router = jnp.concatenate([b_router_group[l], b_router_expert[l], jnp.zeros((pad,), F32)]).reshape(1, LANES)
    x1, route, counts = _out_proj(oda, o_r, x2, w_out[l].astype(BF16),
                                  ln1_g[l].reshape(1, d), ln1_b[l].reshape(1, d), w_router, b_router)

    t_max = (2 * n) // TM_EXP + N_EXPERTS
    tile_expert, n_tiles = _tile_plan(counts, t_max)
    pos = _slots(route, counts)
    pos1, pos2 = pos[0], pos[1]
    inv = _inverse_map(pos1, pos2, t_max * TM_EXP)
    ys = _expert_ffn(tile_expert, n_tiles, inv, x1, w_exp_gate[l], w_exp_up[l], w_exp_down[l], t_max)
    out = _combine(pos1, pos2, x1, route, ys, p[l].reshape(n, PLE_DIM),
                   w_ple_gate[l].astype(BF16), w_ple_proj[l].astype(BF16),
                   ln2_g[l].reshape(1, d), ln2_b[l].reshape(1, d))
    return out.reshape(batch, seq, d)
```

```python
import functools
import math

import numpy as np
import jax
import jax.numpy as jnp
from jax import lax
from jax.experimental import pallas as pl
from jax.experimental.pallas import tpu as pltpu

D_MODEL = 2048
DA_HEADS = 8
DA_HEAD_DIM = 64
DA_V_DIM = 128
RET_HEADS = 4
RET_K_DIM = 128
RET_V_DIM = 256
RET_CHUNK = 128
ROPE_THETA = 10000.0
PLE_DIM = 256
N_GROUPS = 4
EXPERTS_PER_GROUP = 8
N_EXPERTS = 32
EXPERT_FF = 512
DEPTH = 1
DEEPNORM_ALPHA = (2 * DEPTH) ** 0.25
LN_EPS = 1e-5
IN_WIDTH = 6144
DA_WIDTH = 1024
RET_WIDTH = 1024

OFF_QDA, OFF_KDA, OFF_VDA, OFF_QR, OFF_KR, OFF_VR, OFF_GR = 0, 1024, 2048, 3072, 3584, 4096, 5120

LANES = 128
SUBLANES = 8
VMEM_LIMIT = 56 * 1024 * 1024

BF16 = jnp.bfloat16
F32 = jnp.float32

TM_PROJ = 512
TN_PROJ = 512
TQ = 256
TM_OUT = 512
TM_ROW = 256
TM_EXP = 256


def _cparams(sem, vmem=None):
    return pltpu.CompilerParams(dimension_semantics=sem, vmem_limit_bytes=vmem)


def _rope_table_kernel(pos_ref, inv_ref, sgn_ref, tab_ref):
    pos = pos_ref[...].astype(F32)
    ang = pos * inv_ref[...]
    c = jnp.cos(ang)
    s = jnp.sin(ang)
    lane = lax.broadcasted_iota(jnp.int32, ang.shape, 1)
    low = lane < 64
    cr = pltpu.roll(c, 64, 1)
    sr = pltpu.roll(s, 64, 1)
    sgn = sgn_ref[...]
    tab_ref[:, 0:128] = jnp.where(low, cr, c)
    tab_ref[:, 128:256] = jnp.where(low, sr, s) * sgn[0:1, :]
    tab_ref[:, 256:384] = jnp.where(low, c, cr)
    tab_ref[:, 384:512] = jnp.where(low, s, sr) * sgn[1:2, :]


def _rope_tables(positions):
    n = positions.size
    tm = 1024
    i128 = np.power(ROPE_THETA, -np.arange(0, 128, 2, dtype=np.float64) / 128)
    i64 = np.power(ROPE_THETA, -np.arange(0, 64, 2, dtype=np.float64) / 64)
    inv = np.concatenate([i128, i64, i64]).astype(np.float32)[None, :]
    lane = np.arange(128)
    sgn = np.stack([np.where(lane % 64 < 32, -1.0, 1.0), np.where(lane < 64, -1.0, 1.0)]).astype(np.float32)
    sgn = np.concatenate([sgn, np.zeros((6, 128), np.float32)])
    return pl.pallas_call(
        _rope_table_kernel,
        grid=(n // tm,),
        in_specs=[pl.BlockSpec((tm, 1), lambda i: (i, 0)),
                  pl.BlockSpec((1, 128), lambda i: (0, 0)),
                  pl.BlockSpec((8, 128), lambda i: (0, 0))],
        out_specs=pl.BlockSpec((tm, 512), lambda i: (i, 0)),
        out_shape=jax.ShapeDtypeStruct((n, 512), F32),
        compiler_params=_cparams(("arbitrary",)),
        name="rope_tables",
    )(positions.reshape(n, 1), jnp.asarray(inv), jnp.asarray(sgn))


def _rope_cols(acc, cos, sin, half, scale):
    outs = []
    lane = lax.broadcasted_iota(jnp.int32, (acc.shape[0], LANES), 1)
    for c in range(acc.shape[1] // LANES):
        t = acc[:, c * LANES:(c + 1) * LANES]
        if half == 64:
            partner = pltpu.roll(t, 64, 1)
        else:
            partner = jnp.where((lane & 32) == 0, pltpu.roll(t, 96, 1), pltpu.roll(t, 32, 1))
        o = t * cos + partner * sin
        if scale != 1.0:
            o = o * scale
        outs.append(o)
    return jnp.concatenate(outs, axis=1)


def _in_proj_kernel(x_ref, w_ref, tab_ref, o_ref):
    xb = x_ref[...].astype(BF16)
    for j in range(IN_WIDTH // TN_PROJ):
        lo = j * TN_PROJ
        acc = jnp.dot(xb, w_ref[:, lo:lo + TN_PROJ], preferred_element_type=F32)
        if lo < OFF_VDA:
            acc = _rope_cols(acc, tab_ref[:, 0:128], tab_ref[:, 128:256], 32, 1.0)
        elif OFF_QR <= lo < OFF_KR:
            acc = _rope_cols(acc, tab_ref[:, 256:384], tab_ref[:, 384:512], 64, 1.0)
        elif OFF_KR <= lo < OFF_VR:
            acc = _rope_cols(acc, tab_ref[:, 256:384], tab_ref[:, 384:512], 64, RET_K_DIM ** -0.5)
        o_ref[:, lo:lo + TN_PROJ] = acc.astype(BF16)


def _in_proj(x2, w_in_b, tab):
    n = x2.shape[0]
    return pl.pallas_call(
        _in_proj_kernel,
        grid=(n // TM_PROJ,),
        in_specs=[pl.BlockSpec((TM_PROJ, D_MODEL), lambda i: (i, 0)),
                  pl.BlockSpec((D_MODEL, IN_WIDTH), lambda i: (0, 0), pipeline_mode=pl.Buffered(1)),
                  pl.BlockSpec((TM_PROJ, 512), lambda i: (i, 0))],
        out_specs=pl.BlockSpec((TM_PROJ, IN_WIDTH), lambda i: (i, 0)),
        out_shape=jax.ShapeDtypeStruct((n, IN_WIDTH), BF16),
        compiler_params=_cparams(("arbitrary",), VMEM_LIMIT),
        name="in_proj",
    )(x2, w_in_b, tab)


def _diff_attn_kernel(lam_init, q_ref, k_ref, v_ref, lq1_ref, lk1_ref, lq2_ref, lk2_ref, sw_ref,
                      o_ref, m_ref, l_ref, acc_ref):
    seq = q_ref.shape[0]
    lam = (jnp.exp(jnp.sum(lq1_ref[...] * lk1_ref[...], axis=-1, keepdims=True))
           - jnp.exp(jnp.sum(lq2_ref[...] * lk2_ref[...], axis=-1, keepdims=True)) + lam_init)
    scale = jnp.asarray(DA_HEAD_DIM ** -0.5, BF16)
    lane = lax.broadcasted_iota(jnp.int32, (TQ, LANES), 1)
    row_i = lax.broadcasted_iota(jnp.int32, (2 * TQ, TQ), 0)
    col_i = lax.broadcasted_iota(jnp.int32, (2 * TQ, TQ), 1)
    causal = col_i <= jnp.where(row_i >= TQ, row_i - TQ, row_i)

    def step(qs, start, width, masked):
        k = k_ref[pl.ds(start, width), :]
        v = v_ref[pl.ds(start, width), :]
        s = lax.dot_general(qs, k, (((1,), (1,)), ((), ())), preferred_element_type=F32)
        if masked:
            s = jnp.where(causal, s, jnp.finfo(F32).min)
        chunks = [s[:, c * LANES:(c + 1) * LANES] for c in range(width // LANES)]
        m_old = m_ref[...]
        m_new = jnp.maximum(m_old, jnp.max(functools.reduce(jnp.maximum, chunks), axis=-1, keepdims=True))
        alpha = jnp.exp(m_old - m_new)
        p = jnp.concatenate([jnp.exp(c - m_new).astype(BF16) for c in chunks], axis=1)
        v_ext = jnp.concatenate([v, jnp.ones_like(v)], axis=1)
        pv = jnp.dot(p, v_ext, preferred_element_type=F32)
        acc_ref[...] = alpha * acc_ref[...] + pv[:, 0:LANES]
        l_ref[...] = alpha * l_ref[...] + pv[:, LANES:2 * LANES]
        m_ref[...] = m_new

    for qi in range(seq // TQ):
        q = q_ref[qi * TQ:(qi + 1) * TQ, :]
        zero = jnp.zeros_like(q)
        qs = jnp.concatenate([jnp.where(lane < 64, q, zero), jnp.where(lane >= 64, q, zero)], axis=0) * scale
        m_ref[...] = jnp.full(m_ref.shape, -jnp.inf, F32)
        l_ref[...] = jnp.zeros(l_ref.shape, F32)
        acc_ref[...] = jnp.zeros(acc_ref.shape, F32)

        def pair(j, carry, qs=qs):
            step(qs, pl.multiple_of(j * (2 * TQ), 2 * TQ), 2 * TQ, False)
            return carry

        lax.fori_loop(0, qi // 2, pair, 0)
        if qi % 2:
            step(qs, (qi - 1) * TQ, TQ, False)
        step(qs, qi * TQ, TQ, True)

        a = acc_ref[...] / l_ref[...]
        o = a[0:TQ, :] - lam * a[TQ:2 * TQ, :]
        o = o * lax.rsqrt(jnp.mean(o * o, axis=-1, keepdims=True) + LN_EPS)
        o = o * sw_ref[...] * (1.0 - lam_init)
        o_ref[qi * TQ:(qi + 1) * TQ, :] = o.astype(BF16)


def _diff_attn(qkv, lq1, lk1, lq2, lk2, subln_w, lam_init, batch, seq):
    n = qkv.shape[0]
    qb0 = OFF_QDA // LANES
    kb0 = OFF_KDA // LANES
    vb0 = OFF_VDA // LANES
    vec = lambda a: a.reshape(1, -1).astype(F32)
    small = lambda w: pl.BlockSpec((1, w), lambda b, h: (0, 0))
    return pl.pallas_call(
        functools.partial(_diff_attn_kernel, lam_init),
        grid=(batch, DA_HEADS),
        in_specs=[pl.BlockSpec((seq, LANES), lambda b, h: (b, qb0 + h)),
                  pl.BlockSpec((seq, LANES), lambda b, h: (b, kb0 + h)),
                  pl.BlockSpec((seq, LANES), lambda b, h: (b, vb0 + h)),
                  small(64), small(64), small(64), small(64), small(128)],
        out_specs=pl.BlockSpec((seq, LANES), lambda b, h: (b, h)),
        out_shape=jax.ShapeDtypeStruct((n, DA_WIDTH), BF16),
        scratch_shapes=[pltpu.VMEM((2 * TQ, LANES), F32), pltpu.VMEM((2 * TQ, LANES), F32),
                        pltpu.VMEM((2 * TQ, LANES), F32)],
        compiler_params=_cparams(("arbitrary", "arbitrary")),
        name="diff_attn",
    )(qkv, qkv, qkv, vec(lq1), vec(lk1), vec(lq2), vec(lk2), vec(subln_w))


def _retention_kernel(q_ref, k_ref, v_ref, g_ref, lg_ref, o_ref, state_ref):
    C = RET_CHUNK
    lg = lg_ref[0, 0:1, :]
    lg_col = lg[:, 0:1]
    n_i = lax.broadcasted_iota(jnp.int32, (C, C), 0)
    m_i = lax.broadcasted_iota(jnp.int32, (C, C), 1)
    rel = (n_i - m_i).astype(F32)
    decay = jnp.where(rel >= 0, jnp.exp(rel * lg_col), 0.0)
    n_col = lax.broadcasted_iota(jnp.int32, (C, 1), 0).astype(F32)
    zeta = jnp.exp((C - 1.0 - n_col) * lg_col)
    xi = jnp.exp((n_col + 1.0) * lg_col)
    chunk_decay = jnp.exp(C * lg_col)
    state_ref[...] = jnp.zeros(state_ref.shape, F32)

    def body(c, carry):
        start = pl.multiple_of(c * C, C)
        q = q_ref[pl.ds(start, C), :]
        k = k_ref[pl.ds(start, C), :]
        v = v_ref[pl.ds(start, C), :]
        g = g_ref[pl.ds(start, C), :].astype(F32)
        qk = lax.dot_general(q, k, (((1,), (1,)), ((), ())), preferred_element_type=F32)
        inner = (qk * decay).astype(BF16)
        inner_o = jnp.dot(inner, v, preferred_element_type=F32)
        state = state_ref[...]
        cross = jnp.dot(q, state.astype(BF16), preferred_element_type=F32)
        y = inner_o + cross * xi
        kz_t = (k.astype(F32) * zeta).T.astype(BF16)
        kv = jnp.dot(kz_t, v, preferred_element_type=F32)
        state_ref[...] = chunk_decay * state + kv
        mu = jnp.mean(y, axis=-1, keepdims=True)
        yc = y - mu
        var = jnp.mean(yc * yc, axis=-1, keepdims=True)
        yn = yc * lax.rsqrt(var + LN_EPS)
        o_ref[pl.ds(start, C), :] = (g * jax.nn.sigmoid(g) * yn).astype(BF16)
        return carry

    lax.fori_loop(0, q_ref.shape[0] // C, body, 0)


def _retention(qkv, batch, seq):
    n = qkv.shape[0]
    lg = np.log(1.0 - np.power(2.0, -5.0 - np.arange(RET_HEADS, dtype=np.float64)))
    lg_tab = np.broadcast_to(lg[:, None, None], (RET_HEADS, SUBLANES, LANES)).astype(np.float32)
    qb0 = OFF_QR // RET_K_DIM
    kb0 = OFF_KR // RET_K_DIM
    vb0 = OFF_VR // RET_V_DIM
    gb0 = OFF_GR // RET_V_DIM
    return pl.pallas_call(
        _retention_kernel,
        grid=(batch, RET_HEADS),
        in_specs=[pl.BlockSpec((seq, RET_K_DIM), lambda b, h: (b, qb0 + h)),
                  pl.BlockSpec((seq, RET_K_DIM), lambda b, h: (b, kb0 + h)),
                  pl.BlockSpec((seq, RET_V_DIM), lambda b, h: (b, vb0 + h)),
                  pl.BlockSpec((seq, RET_V_DIM), lambda b, h: (b, gb0 + h)),
                  pl.BlockSpec((1, SUBLANES, LANES), lambda b, h: (h, 0, 0))],
        out_specs=pl.BlockSpec((seq, RET_V_DIM), lambda b, h: (b, h)),
        out_shape=jax.ShapeDtypeStruct((n, RET_WIDTH), BF16),
        scratch_shapes=[pltpu.VMEM((RET_K_DIM, RET_V_DIM), F32)],
        compiler_params=_cparams(("arbitrary", "arbitrary")),
        name="retention",
    )(qkv, qkv, qkv, qkv, jnp.asarray(lg_tab))


def _layer_norm(z, g, b):
    mu = jnp.mean(z, axis=-1, keepdims=True)
    zc = z - mu
    var = jnp.mean(zc * zc, axis=-1, keepdims=True)
    return zc * lax.rsqrt(var + LN_EPS) * g + b


def _split_bf16(a):
    hi = a.astype(BF16)
    lo = (a - hi.astype(F32)).astype(BF16)
    return hi, lo


def _out_proj_kernel(oda_ref, or_ref, x_ref, w_ref, g_ref, b_ref, wr_ref, br_ref,
                     x1_ref, route_ref, cnt_ref, run_ref):
    i = pl.program_id(0)
    tm = x_ref.shape[0]

    @pl.when(i == 0)
    def _():
        run_ref[...] = jnp.zeros(run_ref.shape, F32)

    h = jnp.dot(oda_ref[...], w_ref[0:DA_WIDTH, :], preferred_element_type=F32)
    h = h + jnp.dot(or_ref[...], w_ref[DA_WIDTH:DA_WIDTH + RET_WIDTH, :], preferred_element_type=F32)
    x1 = _layer_norm(DEEPNORM_ALPHA * x_ref[...] + h, g_ref[...], b_ref[...])
    x1_ref[...] = x1

    xh, xl = _split_bf16(x1)
    wh, wl = _split_bf16(wr_ref[...])
    logits = (jnp.dot(xh, wh, preferred_element_type=F32) + jnp.dot(xl, wh, preferred_element_type=F32)
              + jnp.dot(xh, wl, preferred_element_type=F32)) + br_ref[...]
    lane = lax.broadcasted_iota(jnp.int32, logits.shape, 1).astype(F32)
    neg = jnp.float32(-jnp.inf)
    big = jnp.float32(1 << 20)
    is_g = lane < N_GROUPS
    gl = jnp.where(is_g, logits, neg)
    gmax = jnp.max(gl, axis=-1, keepdims=True)
    g_idx = jnp.min(jnp.where(is_g & (gl == gmax), lane, big), axis=-1, keepdims=True)
    g_w = 1.0 / jnp.sum(jnp.where(is_g, jnp.exp(gl - gmax), 0.0), axis=-1, keepdims=True)
    e_lane = lane - N_GROUPS
    in_grp = (e_lane >= g_idx * EXPERTS_PER_GROUP) & (e_lane < (g_idx + 1.0) * EXPERTS_PER_GROUP)
    el = jnp.where(in_grp, logits, neg)
    v1 = jnp.max(el, axis=-1, keepdims=True)
    l1 = jnp.min(jnp.where(in_grp & (el == v1), lane, big), axis=-1, keepdims=True)
    el2 = jnp.where(lane == l1, neg, el)
    v2 = jnp.max(el2, axis=-1, keepdims=True)
    l2 = jnp.min(jnp.where(in_grp & (lane != l1) & (el2 == v2), lane, big), axis=-1, keepdims=True)
    t = jnp.exp(v2 - v1)
    fw1 = g_w / (1.0 + t)
    fw2 = g_w * t / (1.0 + t)
    e1 = l1 - N_GROUPS
    e2 = l2 - N_GROUPS

    oh1 = (lane == e1).astype(F32)
    oh2 = (lane == e2).astype(F32)
    cnt = oh1 + oh2
    r_i = lax.broadcasted_iota(jnp.int32, (tm, tm), 0)
    c_i = lax.broadcasted_iota(jnp.int32, (tm, tm), 1)
    tri = (c_i < r_i).astype(BF16)
    before = jnp.dot(tri, cnt.astype(BF16), preferred_element_type=F32) + run_ref[...]
    rank1 = jnp.sum(oh1 * before, axis=-1, keepdims=True)
    rank2 = jnp.sum(oh2 * before, axis=-1, keepdims=True)
    run_new = run_ref[...] + jnp.sum(cnt, axis=0, keepdims=True)
    run_ref[...] = run_new
    cnt_ref[...] = jnp.broadcast_to(run_new, cnt_ref.shape)

    route = jnp.where(lane == 0, e1, 0.0)
    route = jnp.where(lane == 1, e2, route)
    route = jnp.where(lane == 2, rank1, route)
    route = jnp.where(lane == 3, rank2, route)
    route = jnp.where(lane == 4, fw1, route)
    route = jnp.where(lane == 5, fw2, route)
    route_ref[...] = route


def _out_proj(oda, o_r, x2, w_out_b, ln_g, ln_b, w_router, b_router):
    n = x2.shape[0]
    tm = TM_OUT
    row = lambda w: pl.BlockSpec((1, w), lambda i: (0, 0))
    return pl.pallas_call(
        _out_proj_kernel,
        grid=(n // tm,),
        in_specs=[pl.BlockSpec((tm, DA_WIDTH), lambda i: (i, 0)),
                  pl.BlockSpec((tm, RET_WIDTH), lambda i: (i, 0)),
                  pl.BlockSpec((tm, D_MODEL), lambda i: (i, 0)),
                  pl.BlockSpec((D_MODEL, D_MODEL), lambda i: (0, 0), pipeline_mode=pl.Buffered(1)),
                  row(D_MODEL), row(D_MODEL),
                  pl.BlockSpec((D_MODEL, LANES), lambda i: (0, 0)),
                  row(LANES)],
        out_specs=[pl.BlockSpec((tm, D_MODEL), lambda i: (i, 0)),
                   pl.BlockSpec((tm, LANES), lambda i: (i, 0)),
                   pl.BlockSpec((SUBLANES, LANES), lambda i: (0, 0))],
        out_shape=[jax.ShapeDtypeStruct((n, D_MODEL), F32),
                   jax.ShapeDtypeStruct((n, LANES), F32),
                   jax.ShapeDtypeStruct((SUBLANES, LANES), F32)],
        scratch_shapes=[pltpu.VMEM((1, LANES), F32)],
        compiler_params=_cparams(("arbitrary",), VMEM_LIMIT),
        name="out_proj_ln1_router",
    )(oda, o_r, x2, w_out_b, ln_g, ln_b, w_router, b_router)


def _slot_kernel(route_ref, cnt_ref, pos_ref):
    route = route_ref[...]
    lane = lax.broadcasted_iota(jnp.int32, (1, LANES), 1)
    cnt = jnp.where(lane < N_EXPERTS, cnt_ref[0:1, :], 0.0)
    padded = jnp.floor((cnt + (TM_EXP - 1)) * (1.0 / TM_EXP)) * TM_EXP
    ends = padded
    for sh in (1, 2, 4, 8, 16):
        ends = ends + jnp.where(lane >= sh, pltpu.roll(ends, sh, 1), 0.0)
    offs = ends - padded
    lane_f = lax.broadcasted_iota(jnp.int32, route.shape, 1).astype(F32)
    off1 = jnp.sum(jnp.where(lane_f == route[:, 0:1], offs, 0.0), axis=-1, keepdims=True)
    off2 = jnp.sum(jnp.where(lane_f == route[:, 1:2], offs, 0.0), axis=-1, keepdims=True)
    slots = jnp.where(lane_f == 0.0, off1 + route[:, 2:3], 0.0)
    slots = jnp.where(lane_f == 1.0, off2 + route[:, 3:4], slots)
    pos_ref[...] = slots.T[0:SUBLANES, :].astype(jnp.int32)


def _slots(route, counts):
    n = route.shape[0]
    tm = 1024
    return pl.pallas_call(
        _slot_kernel,
        grid=(n // tm,),
        in_specs=[pl.BlockSpec((tm, LANES), lambda i: (i, 0)),
                  pl.BlockSpec((SUBLANES, LANES), lambda i: (0, 0))],
        out_specs=pl.BlockSpec((SUBLANES, tm), lambda i: (0, i)),
        out_shape=jax.ShapeDtypeStruct((SUBLANES, n), jnp.int32),
        compiler_params=_cparams(("arbitrary",)),
        name="moe_slots",
    )(route, counts)


def _inverse_kernel(pos1_ref, pos2_ref, inv_ref):
    i = pl.program_id(0)
    tm = pos1_ref.shape[0]

    @pl.when(i == 0)
    def _():
        def zero(j, carry):
            inv_ref[j] = 0
            return carry
        lax.fori_loop(0, inv_ref.shape[0], zero, 0, unroll=8)

    def scatter(r, carry):
        tok = i * tm + r
        inv_ref[pos1_ref[r]] = tok
        inv_ref[pos2_ref[r]] = tok
        return carry

    lax.fori_loop(0, tm, scatter, 0, unroll=8)


def _inverse_map(pos1, pos2, p_rows):
    n = pos1.shape[0]
    tm = 1024
    smem = pl.BlockSpec((tm,), lambda i: (i,), memory_space=pltpu.SMEM)
    return pl.pallas_call(
        _inverse_kernel,
        grid=(n // tm,),
        in_specs=[smem, smem],
        out_specs=pl.BlockSpec(memory_space=pltpu.SMEM),
        out_shape=jax.ShapeDtypeStruct((p_rows,), jnp.int32),
        compiler_params=_cparams(("arbitrary",)),
        name="moe_inverse_map",
    )(pos1, pos2)


def _expert_ffn_kernel(te_ref, nt_ref, inv_ref, inv_next_ref, x1_ref, wg_ref, wu_ref, wd_ref,
                       ys_ref, xbuf_ref, wgb_ref, wub_ref, wdb_ref, sem):
    i = pl.program_id(0)
    tm = TM_EXP
    n_tiles = nt_ref[0]
    active = i < n_tiles
    slot = i % 2

    def gather_rows(idx_ref, buf_slot):
        def issue(r, carry):
            pltpu.make_async_copy(x1_ref.at[pl.ds(idx_ref[r], 1), :],
                                  xbuf_ref.at[buf_slot, pl.ds(r, 1), :], sem.at[buf_slot]).start()
            return carry
        lax.fori_loop(0, tm, issue, 0, unroll=8)

    @pl.when(i == 0)
    def _():
        gather_rows(inv_ref, 0)

    @pl.when(i + 1 < n_tiles)
    def _():
        gather_rows(inv_next_ref, 1 - slot)

    prev = te_ref[jnp.maximum(i - 1, 0)]
    new_expert = (i == 0) | (te_ref[i] != prev)

    @pl.when(active & new_expert)
    def _():
        wgb_ref[...] = wg_ref[0].astype(BF16)
        wub_ref[...] = wu_ref[0].astype(BF16)
        wdb_ref[...] = wd_ref[0].astype(BF16)

    @pl.when(active)
    def _():
        pltpu.make_async_copy(x1_ref.at[pl.ds(0, tm), :], xbuf_ref.at[slot], sem.at[slot]).wait()
        x = xbuf_ref[slot].astype(BF16)
        g = jnp.dot(x, wgb_ref[...], preferred_element_type=F32)
        u = jnp.dot(x, wub_ref[...], preferred_element_type=F32)
        hmid = (g * jax.nn.sigmoid(g) * u).astype(BF16)
        ys_ref[...] = jnp.dot(hmid, wdb_ref[...], preferred_element_type=F32)

    @pl.when(jnp.logical_not(active))
    def _():
        ys_ref[...] = jnp.zeros(ys_ref.shape, F32)


def _expert_ffn(tile_expert, n_tiles, inv, x1, w_gate, w_up, w_down, t_max):
    tm = TM_EXP
    grid_spec = pltpu.PrefetchScalarGridSpec(
        num_scalar_prefetch=2,
        grid=(t_max,),
        in_specs=[pl.BlockSpec((tm,), lambda i, te, nt: (i,), memory_space=pltpu.SMEM),
                  pl.BlockSpec((tm,), lambda i, te, nt: (jnp.minimum(i + 1, t_max - 1),),
                               memory_space=pltpu.SMEM),
                  pl.BlockSpec(memory_space=pl.ANY),
                  pl.BlockSpec((1, D_MODEL, EXPERT_FF), lambda i, te, nt: (te[i], 0, 0)),
                  pl.BlockSpec((1, D_MODEL, EXPERT_FF), lambda i, te, nt: (te[i], 0, 0)),
                  pl.BlockSpec((1, EXPERT_FF, D_MODEL), lambda i, te, nt: (te[i], 0, 0))],
        out_specs=pl.BlockSpec((tm, D_MODEL), lambda i, te, nt: (i, 0)),
        scratch_shapes=[pltpu.VMEM((2, tm, D_MODEL), F32),
                        pltpu.VMEM((D_MODEL, EXPERT_FF), BF16),
                        pltpu.VMEM((D_MODEL, EXPERT_FF), BF16),
                        pltpu.VMEM((EXPERT_FF, D_MODEL), BF16),
                        pltpu.SemaphoreType.DMA((2,))],
    )
    return pl.pallas_call(
        _expert_ffn_kernel,
        grid_spec=grid_spec,
        out_shape=jax.ShapeDtypeStruct((t_max * tm, D_MODEL), F32),
        compiler_params=_cparams(("arbitrary",), VMEM_LIMIT),
        name="moe_expert_ffn",
    )(tile_expert, n_tiles, inv, inv, x1, w_gate, w_up, w_down)


def _combine_kernel(pos1_ref, pos2_ref, pos1n_ref, pos2n_ref, x1_ref, route_ref, ys_ref, p_ref, wpg_ref,
                    wpp_ref, g_ref, b_ref, o_ref, y1_ref, y2_ref, sem):
    i = pl.program_id(0)
    tm = x1_ref.shape[0]
    slot = i % 2

    def gather_rows(p1_ref, p2_ref, buf_slot):
        def issue(r, carry):
            pltpu.make_async_copy(ys_ref.at[pl.ds(p1_ref[r], 1), :],
                                  y1_ref.at[buf_slot, pl.ds(r, 1), :], sem.at[buf_slot]).start()
            pltpu.make_async_copy(ys_ref.at[pl.ds(p2_ref[r], 1), :],
                                  y2_ref.at[buf_slot, pl.ds(r, 1), :], sem.at[buf_slot]).start()
            return carry
        lax.fori_loop(0, tm, issue, 0, unroll=8)

    @pl.when(i == 0)
    def _():
        gather_rows(pos1_ref, pos2_ref, 0)

    @pl.when(i + 1 < pl.num_programs(0))
    def _():
        gather_rows(pos1n_ref, pos2n_ref, 1 - slot)

    pltpu.make_async_copy(ys_ref.at[pl.ds(0, tm), :], y1_ref.at[slot], sem.at[slot]).wait()
    pltpu.make_async_copy(ys_ref.at[pl.ds(0, tm), :], y2_ref.at[slot], sem.at[slot]).wait()

    route = route_ref[...]
    m = route[:, 4:5] * y1_ref[slot] + route[:, 5:6] * y2_ref[slot]
    x2 = _layer_norm(DEEPNORM_ALPHA * x1_ref[...] + m, g_ref[...], b_ref[...])
    gate = jax.nn.sigmoid(jnp.dot(x2.astype(BF16), wpg_ref[...], preferred_element_type=F32))
    proj = jnp.dot(p_ref[...].astype(BF16), wpp_ref[...], preferred_element_type=F32)
    o_ref[...] = x2 + gate * proj


def _combine(pos1, pos2, x1, route, ys, p2, w_ple_gate_b, w_ple_proj_b, ln_g, ln_b):
    n = x1.shape[0]
    tm = TM_ROW
    smem = pl.BlockSpec((tm,), lambda i: (i,), memory_space=pltpu.SMEM)
    smem_next = pl.BlockSpec((tm,), lambda i: (jnp.minimum(i + 1, n // tm - 1),), memory_space=pltpu.SMEM)
    row = lambda w: pl.BlockSpec((1, w), lambda i: (0, 0))
    return pl.pallas_call(
        _combine_kernel,
        grid=(n // tm,),
        in_specs=[smem, smem, smem_next, smem_next,
                  pl.BlockSpec((tm, D_MODEL), lambda i: (i, 0)),
                  pl.BlockSpec((tm, LANES), lambda i: (i, 0)),
                  pl.BlockSpec(memory_space=pl.ANY),
                  pl.BlockSpec((tm, PLE_DIM), lambda i: (i, 0)),
                  pl.BlockSpec((D_MODEL, D_MODEL), lambda i: (0, 0)),
                  pl.BlockSpec((PLE_DIM, D_MODEL), lambda i: (0, 0)),
                  row(D_MODEL), row(D_MODEL)],
        out_specs=pl.BlockSpec((tm, D_MODEL), lambda i: (i, 0)),
        out_shape=jax.ShapeDtypeStruct((n, D_MODEL), F32),
        scratch_shapes=[pltpu.VMEM((2, tm, D_MODEL), F32), pltpu.VMEM((2, tm, D_MODEL), F32),
                        pltpu.SemaphoreType.DMA((2,))],
        compiler_params=_cparams(("arbitrary",), VMEM_LIMIT),
        name="moe_combine_ln2_ple",
    )(pos1, pos2, pos1, pos2, x1, route, ys, p2, w_ple_gate_b, w_ple_proj_b, ln_g, ln_b)


def _tile_plan(counts, t_max):
    tm = TM_EXP
    cnt = counts[0, :N_EXPERTS].astype(jnp.int32)
    ends = jnp.cumsum(((cnt + tm - 1) // tm) * tm)
    n_tiles = (ends[-1] // tm).astype(jnp.int32)
    tile_start = jnp.minimum(jnp.arange(t_max, dtype=jnp.int32), n_tiles - 1) * tm
    te = jnp.sum((ends[None, :] <= tile_start[:, None]).astype(jnp.int32), axis=1)
    return jnp.minimum(te, N_EXPERTS - 1).astype(jnp.int32), n_tiles.reshape(1)


def kernel(x, p, positions, w_in, w_out, da_lambda_q1, da_lambda_k1, da_lambda_q2, da_lambda_k2, da_subln_w, ln1_g, ln1_b, w_router_group, b_router_group, w_router_expert, b_router_expert, w_exp_gate, w_exp_up, w_exp_down, ln2_g, ln2_b, w_ple_gate, w_ple_proj):
    batch, seq, d = x.shape
    n = batch * seq
    assert d == D_MODEL and w_in.shape[0] == DEPTH == 1
    l = 0
    lam_init = 0.8 - 0.6 * math.exp(-0.3 * l)
    x2 = x.reshape(n, d)

    tab = _rope_tables(positions)
    qkv = _in_proj(x2, w_in[l].astype(BF16), tab)
    oda = _diff_attn(qkv, da_lambda_q1[l], da_lambda_k1[l], da_lambda_q2[l], da_lambda_k2[l],
                     da_subln_w[l], lam_init, batch, seq)
    o_r = _retention(qkv, batch, seq)

    pad = LANES - N_GROUPS - N_EXPERTS
    w_router = jnp.concatenate([w_router_group[l], w_router_expert[l], jnp.zeros((d, pad), F32)], axis=1)
    b_router = jnp.concatenate([b_router_group[l], b_router_expert[l], jnp.zeros((pad,), F32)]).reshape(1, LANES)
    x1, route, counts = _out_proj(oda, o_r, x2, w_out[l].astype(BF16),
                                  ln1_g[l].reshape(1, d), ln1_b[l].reshape(1, d), w_router, b_router)

    t_max = (2 * n) // TM_EXP + N_EXPERTS
    tile_expert, n_tiles = _tile_plan(counts, t_max)
    pos = _slots(route, counts)
    pos1, pos2 = pos[0], pos[1]
    inv = _inverse_map(pos1, pos2, t_max * TM_EXP)
    ys = _expert_ffn(tile_expert, n_tiles, inv, x1, w_exp_gate[l], w_exp_up[l], w_exp_down[l], t_max)
    out = _combine(pos1, pos2, x1, route, ys, p[l].reshape(n, PLE_DIM),
                   w_ple_gate[l].astype(BF16), w_ple_proj[l].astype(BF16),
                   ln2_g[l].reshape(1, d), ln2_b[l].reshape(1, d))
    return out.reshape(batch, seq, d)
```

```python
import functools
import math

import numpy as np
import jax
import jax.numpy as jnp
from jax import lax
from jax.experimental import pallas as pl
from jax.experimental.pallas import tpu as pltpu

D_MODEL = 2048
DA_HEADS = 8
DA_HEAD_DIM = 64
DA_V_DIM = 128
RET_HEADS = 4
RET_K_DIM = 128
RET_V_DIM = 256
RET_CHUNK = 128
ROPE_THETA = 10000.0
PLE_DIM = 256
N_GROUPS = 4
EXPERTS_PER_GROUP = 8
N_EXPERTS = 32
EXPERT_FF = 512
DEPTH = 1
DEEPNORM_ALPHA = (2 * DEPTH) ** 0.25
LN_EPS = 1e-5
IN_WIDTH = 6144
DA_WIDTH = 1024
RET_WIDTH = 1024

OFF_QDA, OFF_KDA, OFF_VDA, OFF_QR, OFF_KR, OFF_VR, OFF_GR = 0, 1024, 2048, 3072, 3584, 4096, 5120

LANES = 128
SUBLANES = 8
VMEM_LIMIT = 56 * 1024 * 1024

BF16 = jnp.bfloat16
F32 = jnp.float32

TM_PROJ = 512
TN_PROJ = 512
TQ = 256
TM_OUT = 512
TM_ROW = 256
TM_EXP = 256


def _cparams(sem, vmem=None):
    return pltpu.CompilerParams(dimension_semantics=sem, vmem_limit_bytes=vmem)


def _rope_table_kernel(pos_ref, inv_ref, sgn_ref, tab_ref):
    pos = pos_ref[...].astype(F32)
    ang = pos * inv_ref[...]
    c = jnp.cos(ang)
    s = jnp.sin(ang)
    lane = lax.broadcasted_iota(jnp.int32, ang.shape, 1)
    low = lane < 64
    cr = pltpu.roll(c, 64, 1)
    sr = pltpu.roll(s, 64, 1)
    sgn = sgn_ref[...]
    tab_ref[:, 0:128] = jnp.where(low, cr, c)
    tab_ref[:, 128:256] = jnp.where(low, sr, s) * sgn[0:1, :]
    tab_ref[:, 256:384] = jnp.where(low, c, cr)
    tab_ref[:, 384:512] = jnp.where(low, s, sr) * sgn[1:2, :]


def _rope_tables(positions):
    n = positions.size
    tm = 1024
    i128 = np.power(ROPE_THETA, -np.arange(0, 128, 2, dtype=np.float64) / 128)
    i64 = np.power(ROPE_THETA, -np.arange(0, 64, 2, dtype=np.float64) / 64)
    inv = np.concatenate([i128, i64, i64]).astype(np.float32)[None, :]
    lane = np.arange(128)
    sgn = np.stack([np.where(lane % 64 < 32, -1.0, 1.0), np.where(lane < 64, -1.0, 1.0)]).astype(np.float32)
    sgn = np.concatenate([sgn, np.zeros((6, 128), np.float32)])
    return pl.pallas_call(
        _rope_table_kernel,
        grid=(n // tm,),
        in_specs=[pl.BlockSpec((tm, 1), lambda i: (i, 0)),
                  pl.BlockSpec((1, 128), lambda i: (0, 0)),
                  pl.BlockSpec((8, 128), lambda i: (0, 0))],
        out_specs=pl.BlockSpec((tm, 512), lambda i: (i, 0)),
        out_shape=jax.ShapeDtypeStruct((n, 512), F32),
        compiler_params=_cparams(("arbitrary",)),
        name="rope_tables",
    )(positions.reshape(n, 1), jnp.asarray(inv), jnp.asarray(sgn))


def _rope_cols(acc, cos, sin, half, scale):
    outs = []
    lane = lax.broadcasted_iota(jnp.int32, (acc.shape[0], LANES), 1)
    for c in range(acc.shape[1] // LANES):
        t = acc[:, c * LANES:(c + 1) * LANES]
        if half == 64:
            partner = pltpu.roll(t, 64, 1)
        else:
            partner = jnp.where((lane & 32) == 0, pltpu.roll(t, 96, 1), pltpu.roll(t, 32, 1))
        o = t * cos + partner * sin
        if scale != 1.0:
            o = o * scale
        outs.append(o)
    return jnp.concatenate(outs, axis=1)


def _in_proj_kernel(x_ref, w_ref, tab_ref, o_ref):
    xb = x_ref[...].astype(BF16)
    for j in range(IN_WIDTH // TN_PROJ):
        lo = j * TN_PROJ
        acc = jnp.dot(xb, w_ref[:, lo:lo + TN_PROJ], preferred_element_type=F32)
        if lo < OFF_VDA:
            acc = _rope_cols(acc, tab_ref[:, 0:128], tab_ref[:, 128:256], 32, 1.0)
        elif OFF_QR <= lo < OFF_KR:
            acc = _rope_cols(acc, tab_ref[:, 256:384], tab_ref[:, 384:512], 64, 1.0)
        elif OFF_KR <= lo < OFF_VR:
            acc = _rope_cols(acc, tab_ref[:, 256:384], tab_ref[:, 384:512], 64, RET_K_DIM ** -0.5)
        o_ref[:, lo:lo + TN_PROJ] = acc.astype(BF16)


def _in_proj(x2, w_in_b, tab):
    n = x2.shape[0]
    return pl.pallas_call(
        _in_proj_kernel,
        grid=(n // TM_PROJ,),
        in_specs=[pl.BlockSpec((TM_PROJ, D_MODEL), lambda i: (i, 0)),
                  pl.BlockSpec((D_MODEL, IN_WIDTH), lambda i: (0, 0), pipeline_mode=pl.Buffered(1)),
                  pl.BlockSpec((TM_PROJ, 512), lambda i: (i, 0))],
        out_specs=pl.BlockSpec((TM_PROJ, IN_WIDTH), lambda i: (i, 0)),
        out_shape=jax.ShapeDtypeStruct((n, IN_WIDTH), BF16),
        compiler_params=_cparams(("arbitrary",), VMEM_LIMIT),
        name="in_proj",
    )(x2, w_in_b, tab)


def _diff_attn_kernel(lam_init, q_ref, k_ref, v_ref, lq1_ref, lk1_ref, lq2_ref, lk2_ref, sw_ref,
                      o_ref, m_ref, l_ref, acc_ref):
    seq = q_ref.shape[0]
    lam = (jnp.exp(jnp.sum(lq1_ref[...] * lk1_ref[...], axis=-1, keepdims=True))
           - jnp.exp(jnp.sum(lq2_ref[...] * lk2_ref[...], axis=-1, keepdims=True)) + lam_init)
    scale = jnp.asarray(DA_HEAD_DIM ** -0.5, BF16)
    lane = lax.broadcasted_iota(jnp.int32, (TQ, LANES), 1)
    row_i = lax.broadcasted_iota(jnp.int32, (2 * TQ, TQ), 0)
    col_i = lax.broadcasted_iota(jnp.int32, (2 * TQ, TQ), 1)
    causal = col_i <= jnp.where(row_i >= TQ, row_i - TQ, row_i)

    def step(qs, start, width, masked):
        k = k_ref[pl.ds(start, width), :]
        v = v_ref[pl.ds(start, width), :]
        s = lax.dot_general(qs, k, (((1,), (1,)), ((), ())), preferred_element_type=F32)
        if masked:
            s = jnp.where(causal, s, jnp.finfo(F32).min)
        chunks = [s[:, c * LANES:(c + 1) * LANES] for c in range(width // LANES)]
        m_old = m_ref[...]
        m_new = jnp.maximum(m_old, jnp.max(functools.reduce(jnp.maximum, chunks), axis=-1, keepdims=True))
        alpha = jnp.exp(m_old - m_new)
        p = jnp.concatenate([jnp.exp(c - m_new).astype(BF16) for c in chunks], axis=1)
        v_ext = jnp.concatenate([v, jnp.ones_like(v)], axis=1)
        pv = jnp.dot(p, v_ext, preferred_element_type=F32)
        acc_ref[...] = alpha * acc_ref[...] + pv[:, 0:LANES]
        l_ref[...] = alpha * l_ref[...] + pv[:, LANES:2 * LANES]
        m_ref[...] = m_new

    for qi in range(seq // TQ):
        q = q_ref[qi * TQ:(qi + 1) * TQ, :]
        zero = jnp.zeros_like(q)
        qs = jnp.concatenate([jnp.where(lane < 64, q, zero), jnp.where(lane >= 64, q, zero)], axis=0) * scale
        m_ref[...] = jnp.full(m_ref.shape, -jnp.inf, F32)
        l_ref[...] = jnp.zeros(l_ref.shape, F32)
        acc_ref[...] = jnp.zeros(acc_ref.shape, F32)

        def pair(j, carry, qs=qs):
            step(qs, pl.multiple_of(j * (2 * TQ), 2 * TQ), 2 * TQ, False)
            return carry

        lax.fori_loop(0, qi // 2, pair, 0, unroll=True)
        if qi % 2:
            step(qs, (qi - 1) * TQ, TQ, False)
        step(qs, qi * TQ, TQ, True)

        a = acc_ref[...] / l_ref[...]
        o = a[0:TQ, :] - lam * a[TQ:2 * TQ, :]
        o = o * lax.rsqrt(jnp.mean(o * o, axis=-1, keepdims=True) + LN_EPS)
        o = o * sw_ref[...] * (1.0 - lam_init)
        o_ref[qi * TQ:(qi + 1) * TQ, :] = o.astype(BF16)


def _diff_attn(qkv, lq1, lk1, lq2, lk2, subln_w, lam_init, batch, seq):
    n = qkv.shape[0]
    qb0 = OFF_QDA // LANES
    kb0 = OFF_KDA // LANES
    vb0 = OFF_VDA // LANES
    vec = lambda a: a.reshape(1, -1).astype(F32)
    small = lambda w: pl.BlockSpec((1, w), lambda b, h: (0, 0))
    return pl.pallas_call(
        functools.partial(_diff_attn_kernel, lam_init),
        grid=(batch, DA_HEADS),
        in_specs=[pl.BlockSpec((seq, LANES), lambda b, h: (b, qb0 + h)),
                  pl.BlockSpec((seq, LANES), lambda b, h: (b, kb0 + h)),
                  pl.BlockSpec((seq, LANES), lambda b, h: (b, vb0 + h)),
                  small(64), small(64), small(64), small(64), small(128)],
        out_specs=pl.BlockSpec((seq, LANES), lambda b, h: (b, h)),
        out_shape=jax.ShapeDtypeStruct((n, DA_WIDTH), BF16),
        scratch_shapes=[pltpu.VMEM((2 * TQ, LANES), F32), pltpu.VMEM((2 * TQ, LANES), F32),
                        pltpu.VMEM((2 * TQ, LANES), F32)],
        compiler_params=_cparams(("arbitrary", "arbitrary")),
        name="diff_attn",
    )(qkv, qkv, qkv, vec(lq1), vec(lk1), vec(lq2), vec(lk2), vec(subln_w))


def _retention_kernel(q_ref, k_ref, v_ref, g_ref, lg_ref, o_ref, state_ref):
    C = RET_CHUNK
    lg = lg_ref[0, 0:1, :]
    lg_col = lg[:, 0:1]
    n_i = lax.broadcasted_iota(jnp.int32, (C, C), 0)
    m_i = lax.broadcasted_iota(jnp.int32, (C, C), 1)
    rel = (n_i - m_i).astype(F32)
    decay = jnp.where(rel >= 0, jnp.exp(rel * lg_col), 0.0)
    n_col = lax.broadcasted_iota(jnp.int32, (C, 1), 0).astype(F32)
    zeta = jnp.exp((C - 1.0 - n_col) * lg_col)
    xi = jnp.exp((n_col + 1.0) * lg_col)
    chunk_decay = jnp.exp(C * lg_col)
    state_ref[...] = jnp.zeros(state_ref.shape, F32)

    def body(c, carry):
        start = pl.multiple_of(c * C, C)
        q = q_ref[pl.ds(start, C), :]
        k = k_ref[pl.ds(start, C), :]
        v = v_ref[pl.ds(start, C), :]
        g = g_ref[pl.ds(start, C), :].astype(F32)
        qk = lax.dot_general(q, k, (((1,), (1,)), ((), ())), preferred_element_type=F32)
        inner = (qk * decay).astype(BF16)
        inner_o = jnp.dot(inner, v, preferred_element_type=F32)
        state = state_ref[...]
        cross = jnp.dot(q, state.astype(BF16), preferred_element_type=F32)
        y = inner_o + cross * xi
        kz_t = (k.astype(F32) * zeta).T.astype(BF16)
        kv = jnp.dot(kz_t, v, preferred_element_type=F32)
        state_ref[...] = chunk_decay * state + kv
        mu = jnp.mean(y, axis=-1, keepdims=True)
        yc = y - mu
        var = jnp.mean(yc * yc, axis=-1, keepdims=True)
        yn = yc * lax.rsqrt(var + LN_EPS)
        o_ref[pl.ds(start, C), :] = (g * jax.nn.sigmoid(g) * yn).astype(BF16)
        return carry

    lax.fori_loop(0, q_ref.shape[0] // C, body, 0, unroll=4)


def _retention(qkv, batch, seq):
    n = qkv.shape[0]
    lg = np.log(1.0 - np.power(2.0, -5.0 - np.arange(RET_HEADS, dtype=np.float64)))
    lg_tab = np.broadcast_to(lg[:, None, None], (RET_HEADS, SUBLANES, LANES)).astype(np.float32)
    qb0 = OFF_QR // RET_K_DIM
    kb0 = OFF_KR // RET_K_DIM
    vb0 = OFF_VR // RET_V_DIM
    gb0 = OFF_GR // RET_V_DIM
    return pl.pallas_call(
        _retention_kernel,
        grid=(batch, RET_HEADS),
        in_specs=[pl.BlockSpec((seq, RET_K_DIM), lambda b, h: (b, qb0 + h)),
                  pl.BlockSpec((seq, RET_K_DIM), lambda b, h: (b, kb0 + h)),
                  pl.BlockSpec((seq, RET_V_DIM), lambda b, h: (b, vb0 + h)),
                  pl.BlockSpec((seq, RET_V_DIM), lambda b, h: (b, gb0 + h)),
                  pl.BlockSpec((1, SUBLANES, LANES), lambda b, h: (h, 0, 0))],
        out_specs=pl.BlockSpec((seq, RET_V_DIM), lambda b, h: (b, h)),
        out_shape=jax.ShapeDtypeStruct((n, RET_WIDTH), BF16),
        scratch_shapes=[pltpu.VMEM((RET_K_DIM, RET_V_DIM), F32)],
        compiler_params=_cparams(("arbitrary", "arbitrary")),
        name="retention",
    )(qkv, qkv, qkv, qkv, jnp.asarray(lg_tab))


def _layer_norm(z, g, b):
    mu = jnp.mean(z, axis=-1, keepdims=True)
    zc = z - mu
    var = jnp.mean(zc * zc, axis=-1, keepdims=True)
    return zc * lax.rsqrt(var + LN_EPS) * g + b


def _split_bf16(a):
    hi = a.astype(BF16)
    lo = (a - hi.astype(F32)).astype(BF16)
    return hi, lo


def _out_proj_kernel(oda_ref, or_ref, x_ref, w_ref, g_ref, b_ref, wr_ref, br_ref,
                     x1_ref, route_ref, cnt_ref, run_ref):
    i = pl.program_id(0)
    tm = x_ref.shape[0]

    @pl.when(i == 0)
    def _():
        run_ref[...] = jnp.zeros(run_ref.shape, F32)

    h = jnp.dot(oda_ref[...], w_ref[0:DA_WIDTH, :], preferred_element_type=F32)
    h = h + jnp.dot(or_ref[...], w_ref[DA_WIDTH:DA_WIDTH + RET_WIDTH, :], preferred_element_type=F32)
    x1 = _layer_norm(DEEPNORM_ALPHA * x_ref[...] + h, g_ref[...], b_ref[...])
    x1_ref[...] = x1

    xh, xl = _split_bf16(x1)
    wh, wl = _split_bf16(wr_ref[...])
    logits = (jnp.dot(xh, wh, preferred_element_type=F32) + jnp.dot(xl, wh, preferred_element_type=F32)
              + jnp.dot(xh, wl, preferred_element_type=F32)) + br_ref[...]
    lane = lax.broadcasted_iota(jnp.int32, logits.shape, 1).astype(F32)
    neg = jnp.float32(-jnp.inf)
    big = jnp.float32(1 << 20)
    is_g = lane < N_GROUPS
    gl = jnp.where(is_g, logits, neg)
    gmax = jnp.max(gl, axis=-1, keepdims=True)
    g_idx = jnp.min(jnp.where(is_g & (gl == gmax), lane, big), axis=-1, keepdims=True)
    g_w = 1.0 / jnp.sum(jnp.where(is_g, jnp.exp(gl - gmax), 0.0), axis=-1, keepdims=True)
    e_lane = lane - N_GROUPS
    in_grp = (e_lane >= g_idx * EXPERTS_PER_GROUP) & (e_lane < (g_idx + 1.0) * EXPERTS_PER_GROUP)
    el = jnp.where(in_grp, logits, neg)
    v1 = jnp.max(el, axis=-1, keepdims=True)
    l1 = jnp.min(jnp.where(in_grp & (el == v1), lane, big), axis=-1, keepdims=True)
    el2 = jnp.where(lane == l1, neg, el)
    v2 = jnp.max(el2, axis=-1, keepdims=True)
    l2 = jnp.min(jnp.where(in_grp & (lane != l1) & (el2 == v2), lane, big), axis=-1, keepdims=True)
    t = jnp.exp(v2 - v1)
    fw1 = g_w / (1.0 + t)
    fw2 = g_w * t / (1.0 + t)
    e1 = l1 - N_GROUPS
    e2 = l2 - N_GROUPS

    oh1 = (lane == e1).astype(F32)
    oh2 = (lane == e2).astype(F32)
    cnt = oh1 + oh2
    r_i = lax.broadcasted_iota(jnp.int32, (tm, tm), 0)
    c_i = lax.broadcasted_iota(jnp.int32, (tm, tm), 1)
    tri = (c_i < r_i).astype(BF16)
    before = jnp.dot(tri, cnt.astype(BF16), preferred_element_type=F32) + run_ref[...]
    rank1 = jnp.sum(oh1 * before, axis=-1, keepdims=True)
    rank2 = jnp.sum(oh2 * before, axis=-1, keepdims=True)
    run_new = run_ref[...] + jnp.sum(cnt, axis=0, keepdims=True)
    run_ref[...] = run_new
    cnt_ref[...] = jnp.broadcast_to(run_new, cnt_ref.shape)

    route = jnp.where(lane == 0, e1, 0.0)
    route = jnp.where(lane == 1, e2, route)
    route = jnp.where(lane == 2, rank1, route)
    route = jnp.where(lane == 3, rank2, route)
    route = jnp.where(lane == 4, fw1, route)
    route = jnp.where(lane == 5, fw2, route)
    route_ref[...] = route


def _out_proj(oda, o_r, x2, w_out_b, ln_g, ln_b, w_router, b_router):
    n = x2.shape[0]
    tm = TM_OUT
    row = lambda w: pl.BlockSpec((1, w), lambda i: (0, 0))
    return pl.pallas_call(
        _out_proj_kernel,
        grid=(n // tm,),
        in_specs=[pl.BlockSpec((tm, DA_WIDTH), lambda i: (i, 0)),
                  pl.BlockSpec((tm, RET_WIDTH), lambda i: (i, 0)),
                  pl.BlockSpec((tm, D_MODEL), lambda i: (i, 0)),
                  pl.BlockSpec((D_MODEL, D_MODEL), lambda i: (0, 0), pipeline_mode=pl.Buffered(1)),
                  row(D_MODEL), row(D_MODEL),
                  pl.BlockSpec((D_MODEL, LANES), lambda i: (0, 0)),
                  row(LANES)],
        out_specs=[pl.BlockSpec((tm, D_MODEL), lambda i: (i, 0)),
                   pl.BlockSpec((tm, LANES), lambda i: (i, 0)),
                   pl.BlockSpec((SUBLANES, LANES), lambda i: (0, 0))],
        out_shape=[jax.ShapeDtypeStruct((n, D_MODEL), F32),
                   jax.ShapeDtypeStruct((n, LANES), F32),
                   jax.ShapeDtypeStruct((SUBLANES, LANES), F32)],
        scratch_shapes=[pltpu.VMEM((1, LANES), F32)],
        compiler_params=_cparams(("arbitrary",), VMEM_LIMIT),
        name="out_proj_ln1_router",
    )(oda, o_r, x2, w_out_b, ln_g, ln_b, w_router, b_router)


def _slot_kernel(route_ref, cnt_ref, pos_ref):
    route = route_ref[...]
    lane = lax.broadcasted_iota(jnp.int32, (1, LANES), 1)
    cnt = jnp.where(lane < N_EXPERTS, cnt_ref[0:1, :], 0.0)
    padded = jnp.floor((cnt + (TM_EXP - 1)) * (1.0 / TM_EXP)) * TM_EXP
    ends = padded
    for sh in (1, 2, 4, 8, 16):
        ends = ends + jnp.where(lane >= sh, pltpu.roll(ends, sh, 1), 0.0)
    offs = ends - padded
    lane_f = lax.broadcasted_iota(jnp.int32, route.shape, 1).astype(F32)
    off1 = jnp.sum(jnp.where(lane_f == route[:, 0:1], offs, 0.0), axis=-1, keepdims=True)
    off2 = jnp.sum(jnp.where(lane_f == route[:, 1:2], offs, 0.0), axis=-1, keepdims=True)
    slots = jnp.where(lane_f == 0.0, off1 + route[:, 2:3], 0.0)
    slots = jnp.where(lane_f == 1.0, off2 + route[:, 3:4], slots)
    pos_ref[...] = slots.T[0:SUBLANES, :].astype(jnp.int32)


def _slots(route, counts):
    n = route.shape[0]
    tm = 1024
    return pl.pallas_call(
        _slot_kernel,
        grid=(n // tm,),
        in_specs=[pl.BlockSpec((tm, LANES), lambda i: (i, 0)),
                  pl.BlockSpec((SUBLANES, LANES), lambda i: (0, 0))],
        out_specs=pl.BlockSpec((SUBLANES, tm), lambda i: (0, i)),
        out_shape=jax.ShapeDtypeStruct((SUBLANES, n), jnp.int32),
        compiler_params=_cparams(("arbitrary",)),
        name="moe_slots",
    )(route, counts)


def _inverse_kernel(pos1_ref, pos2_ref, inv_ref):
    i = pl.program_id(0)
    tm = pos1_ref.shape[0]

    @pl.when(i == 0)
    def _():
        def zero(j, carry):
            inv_ref[j] = 0
            return carry
        lax.fori_loop(0, inv_ref.shape[0], zero, 0, unroll=8)

    def scatter(r, carry):
        tok = i * tm + r
        inv_ref[pos1_ref[r]] = tok
        inv_ref[pos2_ref[r]] = tok
        return carry

    lax.fori_loop(0, tm, scatter, 0, unroll=8)


def _inverse_map(pos1, pos2, p_rows):
    n = pos1.shape[0]
    tm = 1024
    smem = pl.BlockSpec((tm,), lambda i: (i,), memory_space=pltpu.SMEM)
    return pl.pallas_call(
        _inverse_kernel,
        grid=(n // tm,),
        in_specs=[smem, smem],
        out_specs=pl.BlockSpec(memory_space=pltpu.SMEM),
        out_shape=jax.ShapeDtypeStruct((p_rows,), jnp.int32),
        compiler_params=_cparams(("arbitrary",)),
        name="moe_inverse_map",
    )(pos1, pos2)


def _expert_ffn_kernel(t_max, tb_ref, inv_ref, x1_ref, wg_ref, wu_ref, wd_ref, ys_ref,
                       xbuf_ref, ybuf_ref, zbuf_ref, wgb_ref, wub_ref, wdb_ref, gsem, osem, zsem):
    e = pl.program_id(0)
    tm = TM_EXP
    g_first = tb_ref[e]
    g_end = tb_ref[e + 1]
    n_total = tb_ref[N_EXPERTS]

    def gather_rows(g, buf_slot):
        def issue(r, carry):
            pltpu.make_async_copy(x1_ref.at[pl.ds(inv_ref[g * tm + r], 1), :],
                                  xbuf_ref.at[buf_slot, pl.ds(r, 1), :], gsem.at[buf_slot]).start()
            return carry
        lax.fori_loop(0, tm, issue, 0, unroll=8)

    def tile_out(g):
        return ys_ref.at[pl.ds(pl.multiple_of(g * tm, tm), tm), :]

    @pl.when(e == 0)
    def _():
        gather_rows(0, 0)
        zbuf_ref[...] = jnp.zeros(zbuf_ref.shape, F32)

        def fill(g, carry):
            pltpu.make_async_copy(zbuf_ref, tile_out(g), zsem).start()
            return carry
        lax.fori_loop(n_total, t_max, fill, 0)

    @pl.when(g_end > g_first)
    def _():
        wgb_ref[...] = wg_ref[0].astype(BF16)
        wub_ref[...] = wu_ref[0].astype(BF16)
        wdb_ref[...] = wd_ref[0].astype(BF16)

    def tile(g, carry):
        slot = g % 2

        @pl.when(g + 1 < n_total)
        def _():
            gather_rows(g + 1, 1 - slot)

        pltpu.make_async_copy(x1_ref.at[pl.ds(0, tm), :], xbuf_ref.at[slot], gsem.at[slot]).wait()
        x = xbuf_ref[slot].astype(BF16)
        gate = jnp.dot(x, wgb_ref[...], preferred_element_type=F32)
        up = jnp.dot(x, wub_ref[...], preferred_element_type=F32)
        hmid = (gate * jax.nn.sigmoid(gate) * up).astype(BF16)
        y = jnp.dot(hmid, wdb_ref[...], preferred_element_type=F32)

        @pl.when(g >= 2)
        def _():
            pltpu.make_async_copy(ybuf_ref.at[slot], tile_out(g), osem.at[slot]).wait()

        ybuf_ref[slot] = y
        pltpu.make_async_copy(ybuf_ref.at[slot], tile_out(g), osem.at[slot]).start()
        return carry

    lax.fori_loop(g_first, g_end, tile, 0)

    @pl.when(e == pl.num_programs(0) - 1)
    def _():
        pltpu.make_async_copy(ybuf_ref.at[0], tile_out(0), osem.at[0]).wait()
        pltpu.make_async_copy(ybuf_ref.at[1], tile_out(0), osem.at[1]).wait()

        def drain(g, carry):
            pltpu.make_async_copy(zbuf_ref, tile_out(g), zsem).wait()
            return carry
        lax.fori_loop(n_total, t_max, drain, 0)


def _expert_ffn(tile_base, inv, x1, w_gate, w_up, w_down, t_max):
    tm = TM_EXP
    assert 2 * x1.shape[0] >= 2 * tm
    grid_spec = pltpu.PrefetchScalarGridSpec(
        num_scalar_prefetch=2,
        grid=(N_EXPERTS,),
        in_specs=[pl.BlockSpec(memory_space=pl.ANY),
                  pl.BlockSpec((1, D_MODEL, EXPERT_FF), lambda e, tb, inv: (e, 0, 0)),
                  pl.BlockSpec((1, D_MODEL, EXPERT_FF), lambda e, tb, inv: (e, 0, 0)),
                  pl.BlockSpec((1, EXPERT_FF, D_MODEL), lambda e, tb, inv: (e, 0, 0))],
        out_specs=pl.BlockSpec(memory_space=pl.ANY),
        scratch_shapes=[pltpu.VMEM((2, tm, D_MODEL), F32),
                        pltpu.VMEM((2, tm, D_MODEL), F32),
                        pltpu.VMEM((tm, D_MODEL), F32),
                        pltpu.VMEM((D_MODEL, EXPERT_FF), BF16),
                        pltpu.VMEM((D_MODEL, EXPERT_FF), BF16),
                        pltpu.VMEM((EXPERT_FF, D_MODEL), BF16),
                        pltpu.SemaphoreType.DMA((2,)),
                        pltpu.SemaphoreType.DMA((2,)),
                        pltpu.SemaphoreType.DMA],
    )
    return pl.pallas_call(
        functools.partial(_expert_ffn_kernel, t_max),
        grid_spec=grid_spec,
        out_shape=jax.ShapeDtypeStruct((t_max * tm, D_MODEL), F32),
        compiler_params=_cparams(("arbitrary",), VMEM_LIMIT),
        name="moe_expert_ffn",
    )(tile_base, inv, x1, w_gate, w_up, w_down)


def _combine_kernel(pos1_ref, pos2_ref, pos1n_ref, pos2n_ref, x1_ref, route_ref, ys_ref, p_ref, wpg_ref,
                    wpp_ref, g_ref, b_ref, o_ref, y1_ref, y2_ref, sem):
    i = pl.program_id(0)
    tm = x1_ref.shape[0]
    slot = i % 2

    def gather_rows(p1_ref, p2_ref, buf_slot):
        def issue(r, carry):
            pltpu.make_async_copy(ys_ref.at[pl.ds(p1_ref[r], 1), :],
                                  y1_ref.at[buf_slot, pl.ds(r, 1), :], sem.at[buf_slot]).start()
            pltpu.make_async_copy(ys_ref.at[pl.ds(p2_ref[r], 1), :],
                                  y2_ref.at[buf_slot, pl.ds(r, 1), :], sem.at[buf_slot]).start()
            return carry
        lax.fori_loop(0, tm, issue, 0, unroll=8)

    @pl.when(i == 0)
    def _():
        gather_rows(pos1_ref, pos2_ref, 0)

    @pl.when(i + 1 < pl.num_programs(0))
    def _():
        gather_rows(pos1n_ref, pos2n_ref, 1 - slot)

    pltpu.make_async_copy(ys_ref.at[pl.ds(0, tm), :], y1_ref.at[slot], sem.at[slot]).wait()
    pltpu.make_async_copy(ys_ref.at[pl.ds(0, tm), :], y2_ref.at[slot], sem.at[slot]).wait()

    route = route_ref[...]
    m = route[:, 4:5] * y1_ref[slot] + route[:, 5:6] * y2_ref[slot]
    x2 = _layer_norm(DEEPNORM_ALPHA * x1_ref[...] + m, g_ref[...], b_ref[...])
    gate = jax.nn.sigmoid(jnp.dot(x2.astype(BF16), wpg_ref[...], preferred_element_type=F32))
    proj = jnp.dot(p_ref[...].astype(BF16), wpp_ref[...], preferred_element_type=F32)
    o_ref[...] = x2 + gate * proj


def _combine(pos1, pos2, x1, route, ys, p2, w_ple_gate_b, w_ple_proj_b, ln_g, ln_b):
    n = x1.shape[0]
    tm = TM_ROW
    smem = pl.BlockSpec((tm,), lambda i: (i,), memory_space=pltpu.SMEM)
    smem_next = pl.BlockSpec((tm,), lambda i: (jnp.minimum(i + 1, n // tm - 1),), memory_space=pltpu.SMEM)
    row = lambda w: pl.BlockSpec((1, w), lambda i: (0, 0))
    return pl.pallas_call(
        _combine_kernel,
        grid=(n // tm,),
        in_specs=[smem, smem, smem_next, smem_next,
                  pl.BlockSpec((tm, D_MODEL), lambda i: (i, 0)),
                  pl.BlockSpec((tm, LANES), lambda i: (i, 0)),
                  pl.BlockSpec(memory_space=pl.ANY),
                  pl.BlockSpec((tm, PLE_DIM), lambda i: (i, 0)),
                  pl.BlockSpec((D_MODEL, D_MODEL), lambda i: (0, 0)),
                  pl.BlockSpec((PLE_DIM, D_MODEL), lambda i: (0, 0)),
                  row(D_MODEL), row(D_MODEL)],
        out_specs=pl.BlockSpec((tm, D_MODEL), lambda i: (i, 0)),
        out_shape=jax.ShapeDtypeStruct((n, D_MODEL), F32),
        scratch_shapes=[pltpu.VMEM((2, tm, D_MODEL), F32), pltpu.VMEM((2, tm, D_MODEL), F32),
                        pltpu.SemaphoreType.DMA((2,))],
        compiler_params=_cparams(("arbitrary",), VMEM_LIMIT),
        name="moe_combine_ln2_ple",
    )(pos1, pos2, pos1, pos2, x1, route, ys, p2, w_ple_gate_b, w_ple_proj_b, ln_g, ln_b)


def _tile_plan(counts):
    tm = TM_EXP
    cnt = counts[0, :N_EXPERTS].astype(jnp.int32)
    tiles = (cnt + tm - 1) // tm
    return jnp.concatenate([jnp.zeros((1,), jnp.int32), jnp.cumsum(tiles).astype(jnp.int32)])


def kernel(x, p, positions, w_in, w_out, da_lambda_q1, da_lambda_k1, da_lambda_q2, da_lambda_k2, da_subln_w, ln1_g, ln1_b, w_router_group, b_router_group, w_router_expert, b_router_expert, w_exp_gate, w_exp_up, w_exp_down, ln2_g, ln2_b, w_ple_gate, w_ple_proj):
    batch, seq, d = x.shape
    n = batch * seq
    assert d == D_MODEL and w_in.shape[0] == DEPTH == 1
    l = 0
    lam_init = 0.8 - 0.6 * math.exp(-0.3 * l)
    x2 = x.reshape(n, d)

    tab = _rope_tables(positions)
    qkv = _in_proj(x2, w_in[l].astype(BF16), tab)
    oda = _diff_attn(qkv, da_lambda_q1[l], da_lambda_k1[l], da_lambda_q2[l], da_lambda_k2[l],
                     da_subln_w[l], lam_init, batch, seq)
    o_r = _retention(qkv, batch, seq)

    pad = LANES - N_GROUPS - N_EXPERTS
    w_router = jnp.concatenate([w_router_group[l], w_router_expert[l], jnp.zeros((d, pad), F32)], axis=1)
    b_router = jnp.concatenate([b_router_group[l], b_router_expert[l], jnp.zeros((pad,), F32)]).reshape(1, LANES)
    x1, route, counts = _out_proj(oda, o_r, x2, w_out[l].astype(BF16),
                                  ln1_g[l].reshape(1, d), ln1_b[l].reshape(1, d), w_router, b_router)

    t_max = (2 * n) // TM_EXP + N_EXPERTS
    tile_base = _tile_plan(counts)
    pos = _slots(route, counts)
    pos1, pos2 = pos[0], pos[1]
    inv = _inverse_map(pos1, pos2, t_max * TM_EXP)
    ys = _expert_ffn(tile_base, inv, x1, w_exp_gate[l], w_exp_up[l], w_exp_down[l], t_max)
    out = _combine(pos1, pos2, x1, route, ys, p[l].reshape(n, PLE_DIM),
                   w_ple_gate[l].astype(BF16), w_ple_proj[l].astype(BF16),
                   ln2_g[l].reshape(1, d), ln2_b[l].reshape(1, d))
    return out.reshape(batch, seq, d)
```

```python
import functools
import math

import numpy as np
import jax
import jax.numpy as jnp
from jax import lax
from jax.experimental import pallas as pl
from jax.experimental.pallas import tpu as pltpu

D_MODEL = 2048
DA_HEADS = 8
DA_HEAD_DIM = 64
DA_V_DIM = 128
RET_HEADS = 4
RET_K_DIM = 128
RET_V_DIM = 256
RET_CHUNK = 128
ROPE_THETA = 10000.0
PLE_DIM = 256
N_GROUPS = 4
EXPERTS_PER_GROUP = 8
N_EXPERTS = 32
EXPERT_FF = 512
DEPTH = 1
DEEPNORM_ALPHA = (2 * DEPTH) ** 0.25
LN_EPS = 1e-5
IN_WIDTH = 6144
DA_WIDTH = 1024
RET_WIDTH = 1024

OFF_QDA, OFF_KDA, OFF_VDA, OFF_QR, OFF_KR, OFF_VR, OFF_GR = 0, 1024, 2048, 3072, 3584, 4096, 5120

LANES = 128
SUBLANES = 8
VMEM_LIMIT = 56 * 1024 * 1024

BF16 = jnp.bfloat16
F32 = jnp.float32

TM_PROJ = 512
TN_PROJ = 512
TQ = 256
TM_OUT = 512
TM_ROW = 256
TM_EXP = 256


def _cparams(sem, vmem=None):
    return pltpu.CompilerParams(dimension_semantics=sem, vmem_limit_bytes=vmem)


def _rope_table_kernel(pos_ref, inv_ref, sgn_ref, tab_ref):
    pos = pos_ref[...].astype(F32)
    ang = pos * inv_ref[...]
    c = jnp.cos(ang)
    s = jnp.sin(ang)
    lane = lax.broadcasted_iota(jnp.int32, ang.shape, 1)
    low = lane < 64
    cr = pltpu.roll(c, 64, 1)
    sr = pltpu.roll(s, 64, 1)
    sgn = sgn_ref[...]
    tab_ref[:, 0:128] = jnp.where(low, cr, c)
    tab_ref[:, 128:256] = jnp.where(low, sr, s) * sgn[0:1, :]
    tab_ref[:, 256:384] = jnp.where(low, c, cr)
    tab_ref[:, 384:512] = jnp.where(low, s, sr) * sgn[1:2, :]


def _rope_tables(positions):
    n = positions.size
    tm = 1024
    i128 = np.power(ROPE_THETA, -np.arange(0, 128, 2, dtype=np.float64) / 128)
    i64 = np.power(ROPE_THETA, -np.arange(0, 64, 2, dtype=np.float64) / 64)
    inv = np.concatenate([i128, i64, i64]).astype(np.float32)[None, :]
    lane = np.arange(128)
    sgn = np.stack([np.where(lane % 64 < 32, -1.0, 1.0), np.where(lane < 64, -1.0, 1.0)]).astype(np.float32)
    sgn = np.concatenate([sgn, np.zeros((6, 128), np.float32)])
    return pl.pallas_call(
        _rope_table_kernel,
        grid=(n // tm,),
        in_specs=[pl.BlockSpec((tm, 1), lambda i: (i, 0)),
                  pl.BlockSpec((1, 128), lambda i: (0, 0)),
                  pl.BlockSpec((8, 128), lambda i: (0, 0))],
        out_specs=pl.BlockSpec((tm, 512), lambda i: (i, 0)),
        out_shape=jax.ShapeDtypeStruct((n, 512), F32),
        compiler_params=_cparams(("arbitrary",)),
        name="rope_tables",
    )(positions.reshape(n, 1), jnp.asarray(inv), jnp.asarray(sgn))


def _rope_cols(acc, cos, sin, half, scale):
    outs = []
    lane = lax.broadcasted_iota(jnp.int32, (acc.shape[0], LANES), 1)
    for c in range(acc.shape[1] // LANES):
        t = acc[:, c * LANES:(c + 1) * LANES]
        if half == 64:
            partner = pltpu.roll(t, 64, 1)
        else:
            partner = jnp.where((lane & 32) == 0, pltpu.roll(t, 96, 1), pltpu.roll(t, 32, 1))
        o = t * cos + partner * sin
        if scale != 1.0:
            o = o * scale
        outs.append(o)
    return jnp.concatenate(outs, axis=1)


def _in_proj_kernel(x_ref, w_ref, tab_ref, o_ref):
    xb = x_ref[...].astype(BF16)
    for j in range(IN_WIDTH // TN_PROJ):
        lo = j * TN_PROJ
        acc = jnp.dot(xb, w_ref[:, lo:lo + TN_PROJ], preferred_element_type=F32)
        if lo < OFF_VDA:
            acc = _rope_cols(acc, tab_ref[:, 0:128], tab_ref[:, 128:256], 32, 1.0)
        elif OFF_QR <= lo < OFF_KR:
            acc = _rope_cols(acc, tab_ref[:, 256:384], tab_ref[:, 384:512], 64, 1.0)
        elif OFF_KR <= lo < OFF_VR:
            acc = _rope_cols(acc, tab_ref[:, 256:384], tab_ref[:, 384:512], 64, RET_K_DIM ** -0.5)
        o_ref[:, lo:lo + TN_PROJ] = acc.astype(BF16)


def _in_proj(x2, w_in_b, tab):
    n = x2.shape[0]
    return pl.pallas_call(
        _in_proj_kernel,
        grid=(n // TM_PROJ,),
        in_specs=[pl.BlockSpec((TM_PROJ, D_MODEL), lambda i: (i, 0)),
                  pl.BlockSpec((D_MODEL, IN_WIDTH), lambda i: (0, 0), pipeline_mode=pl.Buffered(1)),
                  pl.BlockSpec((TM_PROJ, 512), lambda i: (i, 0))],
        out_specs=pl.BlockSpec((TM_PROJ, IN_WIDTH), lambda i: (i, 0)),
        out_shape=jax.ShapeDtypeStruct((n, IN_WIDTH), BF16),
        compiler_params=_cparams(("arbitrary",), VMEM_LIMIT),
        name="in_proj",
    )(x2, w_in_b, tab)


def _diff_attn_kernel(lam_init, q_ref, k_ref, v_ref, lq1_ref, lk1_ref, lq2_ref, lk2_ref, sw_ref,
                      o_ref, m_ref, l_ref, acc_ref):
    seq = q_ref.shape[0]
    lam = (jnp.exp(jnp.sum(lq1_ref[...] * lk1_ref[...], axis=-1, keepdims=True))
           - jnp.exp(jnp.sum(lq2_ref[...] * lk2_ref[...], axis=-1, keepdims=True)) + lam_init)
    scale = jnp.asarray(DA_HEAD_DIM ** -0.5, BF16)
    lane = lax.broadcasted_iota(jnp.int32, (TQ, LANES), 1)
    row_i = lax.broadcasted_iota(jnp.int32, (2 * TQ, TQ), 0)
    col_i = lax.broadcasted_iota(jnp.int32, (2 * TQ, TQ), 1)
    causal = col_i <= jnp.where(row_i >= TQ, row_i - TQ, row_i)

    def step(qs, start, width, masked):
        k = k_ref[pl.ds(start, width), :]
        v = v_ref[pl.ds(start, width), :]
        s = lax.dot_general(qs, k, (((1,), (1,)), ((), ())), preferred_element_type=F32)
        if masked:
            s = jnp.where(causal, s, jnp.finfo(F32).min)
        chunks = [s[:, c * LANES:(c + 1) * LANES] for c in range(width // LANES)]
        m_old = m_ref[...]
        m_new = jnp.maximum(m_old, jnp.max(functools.reduce(jnp.maximum, chunks), axis=-1, keepdims=True))
        alpha = jnp.exp(m_old - m_new)
        p = jnp.concatenate([jnp.exp(c - m_new).astype(BF16) for c in chunks], axis=1)
        v_ext = jnp.concatenate([v, jnp.ones_like(v)], axis=1)
        pv = jnp.dot(p, v_ext, preferred_element_type=F32)
        acc_ref[...] = alpha * acc_ref[...] + pv[:, 0:LANES]
        l_ref[...] = alpha * l_ref[...] + pv[:, LANES:2 * LANES]
        m_ref[...] = m_new

    for qi in range(seq // TQ):
        q = q_ref[qi * TQ:(qi + 1) * TQ, :]
        zero = jnp.zeros_like(q)
        qs = jnp.concatenate([jnp.where(lane < 64, q, zero), jnp.where(lane >= 64, q, zero)], axis=0) * scale
        m_ref[...] = jnp.full(m_ref.shape, -jnp.inf, F32)
        l_ref[...] = jnp.zeros(l_ref.shape, F32)
        acc_ref[...] = jnp.zeros(acc_ref.shape, F32)

        def pair(j, carry, qs=qs):
            step(qs, pl.multiple_of(j * (2 * TQ), 2 * TQ), 2 * TQ, False)
            return carry

        lax.fori_loop(0, qi // 2, pair, 0, unroll=True)
        if qi % 2:
            step(qs, (qi - 1) * TQ, TQ, False)
        step(qs, qi * TQ, TQ, True)

        a = acc_ref[...] / l_ref[...]
        o = a[0:TQ, :] - lam * a[TQ:2 * TQ, :]
        o = o * lax.rsqrt(jnp.mean(o * o, axis=-1, keepdims=True) + LN_EPS)
        o = o * sw_ref[...] * (1.0 - lam_init)
        o_ref[qi * TQ:(qi + 1) * TQ, :] = o.astype(BF16)


def _diff_attn(qkv, lq1, lk1, lq2, lk2, subln_w, lam_init, batch, seq):
    n = qkv.shape[0]
    qb0 = OFF_QDA // LANES
    kb0 = OFF_KDA // LANES
    vb0 = OFF_VDA // LANES
    vec = lambda a: a.reshape(1, -1).astype(F32)
    small = lambda w: pl.BlockSpec((1, w), lambda b, h: (0, 0))
    return pl.pallas_call(
        functools.partial(_diff_attn_kernel, lam_init),
        grid=(batch, DA_HEADS),
        in_specs=[pl.BlockSpec((seq, LANES), lambda b, h: (b, qb0 + h)),
                  pl.BlockSpec((seq, LANES), lambda b, h: (b, kb0 + h)),
                  pl.BlockSpec((seq, LANES), lambda b, h: (b, vb0 + h)),
                  small(64), small(64), small(64), small(64), small(128)],
        out_specs=pl.BlockSpec((seq, LANES), lambda b, h: (b, h)),
        out_shape=jax.ShapeDtypeStruct((n, DA_WIDTH), BF16),
        scratch_shapes=[pltpu.VMEM((2 * TQ, LANES), F32), pltpu.VMEM((2 * TQ, LANES), F32),
                        pltpu.VMEM((2 * TQ, LANES), F32)],
        compiler_params=_cparams(("arbitrary", "arbitrary")),
        name="diff_attn",
    )(qkv, qkv, qkv, vec(lq1), vec(lk1), vec(lq2), vec(lk2), vec(subln_w))


def _retention_kernel(q_ref, k_ref, v_ref, g_ref, lg_ref, o_ref, state_ref):
    C = RET_CHUNK
    lg = lg_ref[0, 0:1, :]
    lg_col = lg[:, 0:1]
    n_i = lax.broadcasted_iota(jnp.int32, (C, C), 0)
    m_i = lax.broadcasted_iota(jnp.int32, (C, C), 1)
    rel = (n_i - m_i).astype(F32)
    decay = jnp.where(rel >= 0, jnp.exp(rel * lg_col), 0.0)
    n_col = lax.broadcasted_iota(jnp.int32, (C, 1), 0).astype(F32)
    zeta = jnp.exp((C - 1.0 - n_col) * lg_col)
    xi = jnp.exp((n_col + 1.0) * lg_col)
    chunk_decay = jnp.exp(C * lg_col)
    state_ref[...] = jnp.zeros(state_ref.shape, F32)

    def body(c, carry):
        start = pl.multiple_of(c * C, C)
        q = q_ref[pl.ds(start, C), :]
        k = k_ref[pl.ds(start, C), :]
        v = v_ref[pl.ds(start, C), :]
        g = g_ref[pl.ds(start, C), :].astype(F32)
        qk = lax.dot_general(q, k, (((1,), (1,)), ((), ())), preferred_element_type=F32)
        inner = (qk * decay).astype(BF16)
        inner_o = jnp.dot(inner, v, preferred_element_type=F32)
        state = state_ref[...]
        cross = jnp.dot(q, state.astype(BF16), preferred_element_type=F32)
        y = inner_o + cross * xi
        kz_t = (k.astype(F32) * zeta).T.astype(BF16)
        kv = jnp.dot(kz_t, v, preferred_element_type=F32)
        state_ref[...] = chunk_decay * state + kv
        mu = jnp.mean(y, axis=-1, keepdims=True)
        yc = y - mu
        var = jnp.mean(yc * yc, axis=-1, keepdims=True)
        yn = yc * lax.rsqrt(var + LN_EPS)
        o_ref[pl.ds(start, C), :] = (g * jax.nn.sigmoid(g) * yn).astype(BF16)
        return carry

    lax.fori_loop(0, q_ref.shape[0] // C, body, 0, unroll=4)


def _retention(qkv, batch, seq):
    n = qkv.shape[0]
    lg = np.log(1.0 - np.power(2.0, -5.0 - np.arange(RET_HEADS, dtype=np.float64)))
    lg_tab = np.broadcast_to(lg[:, None, None], (RET_HEADS, SUBLANES, LANES)).astype(np.float32)
    qb0 = OFF_QR // RET_K_DIM
    kb0 = OFF_KR // RET_K_DIM
    vb0 = OFF_VR // RET_V_DIM
    gb0 = OFF_GR // RET_V_DIM
    return pl.pallas_call(
        _retention_kernel,
        grid=(batch, RET_HEADS),
        in_specs=[pl.BlockSpec((seq, RET_K_DIM), lambda b, h: (b, qb0 + h)),
                  pl.BlockSpec((seq, RET_K_DIM), lambda b, h: (b, kb0 + h)),
                  pl.BlockSpec((seq, RET_V_DIM), lambda b, h: (b, vb0 + h)),
                  pl.BlockSpec((seq, RET_V_DIM), lambda b, h: (b, gb0 + h)),
                  pl.BlockSpec((1, SUBLANES, LANES), lambda b, h: (h, 0, 0))],
        out_specs=pl.BlockSpec((seq, RET_V_DIM), lambda b, h: (b, h)),
        out_shape=jax.ShapeDtypeStruct((n, RET_WIDTH), BF16),
        scratch_shapes=[pltpu.VMEM((RET_K_DIM, RET_V_DIM), F32)],
        compiler_params=_cparams(("arbitrary", "arbitrary")),
        name="retention",
    )(qkv, qkv, qkv, qkv, jnp.asarray(lg_tab))


def _layer_norm(z, g, b):
    mu = jnp.mean(z, axis=-1, keepdims=True)
    zc = z - mu
    var = jnp.mean(zc * zc, axis=-1, keepdims=True)
    return zc * lax.rsqrt(var + LN_EPS) * g + b


def _split_bf16(a):
    hi = a.astype(BF16)
    lo = (a - hi.astype(F32)).astype(BF16)
    return hi, lo


def _out_proj_kernel(oda_ref, or_ref, x_ref, w_ref, g_ref, b_ref, wr_ref, br_ref,
                     x1_ref, route_ref, cnt_ref, run_ref):
    i = pl.program_id(0)
    tm = x_ref.shape[0]

    @pl.when(i == 0)
    def _():
        run_ref[...] = jnp.zeros(run_ref.shape, F32)

    h = jnp.dot(oda_ref[...], w_ref[0:DA_WIDTH, :], preferred_element_type=F32)
    h = h + jnp.dot(or_ref[...], w_ref[DA_WIDTH:DA_WIDTH + RET_WIDTH, :], preferred_element_type=F32)
    x1 = _layer_norm(DEEPNORM_ALPHA * x_ref[...] + h, g_ref[...], b_ref[...])
    x1_ref[...] = x1

    xh, xl = _split_bf16(x1)
    wh, wl = _split_bf16(wr_ref[...])
    logits = (jnp.dot(xh, wh, preferred_element_type=F32) + jnp.dot(xl, wh, preferred_element_type=F32)
              + jnp.dot(xh, wl, preferred_element_type=F32)) + br_ref[...]
    lane = lax.broadcasted_iota(jnp.int32, logits.shape, 1).astype(F32)
    neg = jnp.float32(-jnp.inf)
    big = jnp.float32(1 << 20)
    is_g = lane < N_GROUPS
    gl = jnp.where(is_g, logits, neg)
    gmax = jnp.max(gl, axis=-1, keepdims=True)
    g_idx = jnp.min(jnp.where(is_g & (gl == gmax), lane, big), axis=-1, keepdims=True)
    g_w = 1.0 / jnp.sum(jnp.where(is_g, jnp.exp(gl - gmax), 0.0), axis=-1, keepdims=True)
    e_lane = lane - N_GROUPS
    in_grp = (e_lane >= g_idx * EXPERTS_PER_GROUP) & (e_lane < (g_idx + 1.0) * EXPERTS_PER_GROUP)
    el = jnp.where(in_grp, logits, neg)
    v1 = jnp.max(el, axis=-1, keepdims=True)
    l1 = jnp.min(jnp.where(in_grp & (el == v1), lane, big), axis=-1, keepdims=True)
    el2 = jnp.where(lane == l1, neg, el)
    v2 = jnp.max(el2, axis=-1, keepdims=True)
    l2 = jnp.min(jnp.where(in_grp & (lane != l1) & (el2 == v2), lane, big), axis=-1, keepdims=True)
    t = jnp.exp(v2 - v1)
    fw1 = g_w / (1.0 + t)
    fw2 = g_w * t / (1.0 + t)
    e1 = l1 - N_GROUPS
    e2 = l2 - N_GROUPS

    oh1 = (lane == e1).astype(F32)
    oh2 = (lane == e2).astype(F32)
    cnt = oh1 + oh2
    r_i = lax.broadcasted_iota(jnp.int32, (tm, tm), 0)
    c_i = lax.broadcasted_iota(jnp.int32, (tm, tm), 1)
    tri = (c_i < r_i).astype(BF16)
    before = jnp.dot(tri, cnt.astype(BF16), preferred_element_type=F32) + run_ref[...]
    rank1 = jnp.sum(oh1 * before, axis=-1, keepdims=True)
    rank2 = jnp.sum(oh2 * before, axis=-1, keepdims=True)
    run_new = run_ref[...] + jnp.sum(cnt, axis=0, keepdims=True)
    run_ref[...] = run_new
    cnt_ref[...] = jnp.broadcast_to(run_new, cnt_ref.shape)

    route = jnp.where(lane == 0, e1, 0.0)
    route = jnp.where(lane == 1, e2, route)
    route = jnp.where(lane == 2, rank1, route)
    route = jnp.where(lane == 3, rank2, route)
    route = jnp.where(lane == 4, fw1, route)
    route = jnp.where(lane == 5, fw2, route)
    route_ref[...] = route


def _out_proj(oda, o_r, x2, w_out_b, ln_g, ln_b, w_router, b_router):
    n = x2.shape[0]
    tm = TM_OUT
    row = lambda w: pl.BlockSpec((1, w), lambda i: (0, 0))
    return pl.pallas_call(
        _out_proj_kernel,
        grid=(n // tm,),
        in_specs=[pl.BlockSpec((tm, DA_WIDTH), lambda i: (i, 0)),
                  pl.BlockSpec((tm, RET_WIDTH), lambda i: (i, 0)),
                  pl.BlockSpec((tm, D_MODEL), lambda i: (i, 0)),
                  pl.BlockSpec((D_MODEL, D_MODEL), lambda i: (0, 0), pipeline_mode=pl.Buffered(1)),
                  row(D_MODEL), row(D_MODEL),
                  pl.BlockSpec((D_MODEL, LANES), lambda i: (0, 0)),
                  row(LANES)],
        out_specs=[pl.BlockSpec((tm, D_MODEL), lambda i: (i, 0)),
                   pl.BlockSpec((tm, LANES), lambda i: (i, 0)),
                   pl.BlockSpec((SUBLANES, LANES), lambda i: (0, 0))],
        out_shape=[jax.ShapeDtypeStruct((n, D_MODEL), F32),
                   jax.ShapeDtypeStruct((n, LANES), F32),
                   jax.ShapeDtypeStruct((SUBLANES, LANES), F32)],
        scratch_shapes=[pltpu.VMEM((1, LANES), F32)],
        compiler_params=_cparams(("arbitrary",), VMEM_LIMIT),
        name="out_proj_ln1_router",
    )(oda, o_r, x2, w_out_b, ln_g, ln_b, w_router, b_router)


def _slot_kernel(route_ref, cnt_ref, pos_ref):
    route = route_ref[...]
    lane = lax.broadcasted_iota(jnp.int32, (1, LANES), 1)
    cnt = jnp.where(lane < N_EXPERTS, cnt_ref[0:1, :], 0.0)
    padded = jnp.floor((cnt + (TM_EXP - 1)) * (1.0 / TM_EXP)) * TM_EXP
    ends = padded
    for sh in (1, 2, 4, 8, 16):
        ends = ends + jnp.where(lane >= sh, pltpu.roll(ends, sh, 1), 0.0)
    offs = ends - padded
    lane_f = lax.broadcasted_iota(jnp.int32, route.shape, 1).astype(F32)
    off1 = jnp.sum(jnp.where(lane_f == route[:, 0:1], offs, 0.0), axis=-1, keepdims=True)
    off2 = jnp.sum(jnp.where(lane_f == route[:, 1:2], offs, 0.0), axis=-1, keepdims=True)
    slots = jnp.where(lane_f == 0.0, off1 + route[:, 2:3], 0.0)
    slots = jnp.where(lane_f == 1.0, off2 + route[:, 3:4], slots)
    pos_ref[...] = slots.T[0:SUBLANES, :].astype(jnp.int32)


def _slots(route, counts):
    n = route.shape[0]
    tm = 1024
    return pl.pallas_call(
        _slot_kernel,
        grid=(n // tm,),
        in_specs=[pl.BlockSpec((tm, LANES), lambda i: (i, 0)),
                  pl.BlockSpec((SUBLANES, LANES), lambda i: (0, 0))],
        out_specs=pl.BlockSpec((SUBLANES, tm), lambda i: (0, i)),
        out_shape=jax.ShapeDtypeStruct((SUBLANES, n), jnp.int32),
        compiler_params=_cparams(("arbitrary",)),
        name="moe_slots",
    )(route, counts)


def _inverse_kernel(pos1_ref, pos2_ref, inv_ref):
    i = pl.program_id(0)
    tm = pos1_ref.shape[0]

    @pl.when(i == 0)
    def _():
        def zero(j, carry):
            inv_ref[j] = 0
            return carry
        lax.fori_loop(0, inv_ref.shape[0], zero, 0, unroll=8)

    def scatter(r, carry):
        tok = i * tm + r
        inv_ref[pos1_ref[r]] = tok
        inv_ref[pos2_ref[r]] = tok
        return carry

    lax.fori_loop(0, tm, scatter, 0, unroll=8)


def _inverse_map(pos1, pos2, p_rows):
    n = pos1.shape[0]
    tm = 1024
    smem = pl.BlockSpec((tm,), lambda i: (i,), memory_space=pltpu.SMEM)
    return pl.pallas_call(
        _inverse_kernel,
        grid=(n // tm,),
        in_specs=[smem, smem],
        out_specs=pl.BlockSpec(memory_space=pltpu.SMEM),
        out_shape=jax.ShapeDtypeStruct((p_rows,), jnp.int32),
        compiler_params=_cparams(("arbitrary",)),
        name="moe_inverse_map",
    )(pos1, pos2)


def _expert_ffn_kernel(t_max, tb_ref, inv_ref, x1_ref, wg_ref, wu_ref, wd_ref, ys_ref,
                       xbuf_ref, ybuf_ref, zbuf_ref, wgb_ref, wub_ref, wdb_ref, gsem, osem, zsem):
    e = pl.program_id(0)
    tm = TM_EXP
    g_first = tb_ref[e]
    g_end = tb_ref[e + 1]
    n_total = tb_ref[N_EXPERTS]

    def gather_rows(g, buf_slot):
        def issue(r, carry):
            pltpu.make_async_copy(x1_ref.at[pl.ds(inv_ref[g * tm + r], 1), :],
                                  xbuf_ref.at[buf_slot, pl.ds(r, 1), :], gsem.at[buf_slot]).start(priority=1)
            return carry
        lax.fori_loop(0, tm, issue, 0, unroll=8)

    def tile_out(g):
        return ys_ref.at[pl.ds(pl.multiple_of(g * tm, tm), tm), :]

    @pl.when(e == 0)
    def _():
        gather_rows(0, 0)
        zbuf_ref[...] = jnp.zeros(zbuf_ref.shape, F32)

        def fill(g, carry):
            pltpu.make_async_copy(zbuf_ref, tile_out(g), zsem).start()
            return carry
        lax.fori_loop(n_total, t_max, fill, 0)

    @pl.when(g_end > g_first)
    def _():
        wgb_ref[...] = wg_ref[0].astype(BF16)
        wub_ref[...] = wu_ref[0].astype(BF16)
        wdb_ref[...] = wd_ref[0].astype(BF16)

    def tile(g, carry):
        slot = g % 2

        @pl.when(g + 1 < n_total)
        def _():
            gather_rows(g + 1, 1 - slot)

        pltpu.make_async_copy(x1_ref.at[pl.ds(0, tm), :], xbuf_ref.at[slot], gsem.at[slot]).wait()
        x = xbuf_ref[slot].astype(BF16)
        gate = jnp.dot(x, wgb_ref[...], preferred_element_type=F32)
        up = jnp.dot(x, wub_ref[...], preferred_element_type=F32)
        hmid = (gate * jax.nn.sigmoid(gate) * up).astype(BF16)
        y = jnp.dot(hmid, wdb_ref[...], preferred_element_type=F32)

        @pl.when(g >= 2)
        def _():
            pltpu.make_async_copy(ybuf_ref.at[slot], tile_out(g), osem.at[slot]).wait()

        ybuf_ref[slot] = y
        pltpu.make_async_copy(ybuf_ref.at[slot], tile_out(g), osem.at[slot]).start()
        return carry

    lax.fori_loop(g_first, g_end, tile, 0)

    @pl.when(e == pl.num_programs(0) - 1)
    def _():
        pltpu.make_async_copy(ybuf_ref.at[0], tile_out(0), osem.at[0]).wait()
        pltpu.make_async_copy(ybuf_ref.at[1], tile_out(0), osem.at[1]).wait()

        def drain(g, carry):
            pltpu.make_async_copy(zbuf_ref, tile_out(g), zsem).wait()
            return carry
        lax.fori_loop(n_total, t_max, drain, 0)


def _expert_ffn(tile_base, inv, x1, w_gate, w_up, w_down, t_max):
    tm = TM_EXP
    assert 2 * x1.shape[0] >= 2 * tm
    grid_spec = pltpu.PrefetchScalarGridSpec(
        num_scalar_prefetch=2,
        grid=(N_EXPERTS,),
        in_specs=[pl.BlockSpec(memory_space=pl.ANY),
                  pl.BlockSpec((1, D_MODEL, EXPERT_FF), lambda e, tb, inv: (e, 0, 0)),
                  pl.BlockSpec((1, D_MODEL, EXPERT_FF), lambda e, tb, inv: (e, 0, 0)),
                  pl.BlockSpec((1, EXPERT_FF, D_MODEL), lambda e, tb, inv: (e, 0, 0))],
        out_specs=pl.BlockSpec(memory_space=pl.ANY),
        scratch_shapes=[pltpu.VMEM((2, tm, D_MODEL), F32),
                        pltpu.VMEM((2, tm, D_MODEL), F32),
                        pltpu.VMEM((tm, D_MODEL), F32),
                        pltpu.VMEM((D_MODEL, EXPERT_FF), BF16),
                        pltpu.VMEM((D_MODEL, EXPERT_FF), BF16),
                        pltpu.VMEM((EXPERT_FF, D_MODEL), BF16),
                        pltpu.SemaphoreType.DMA((2,)),
                        pltpu.SemaphoreType.DMA((2,)),
                        pltpu.SemaphoreType.DMA],
    )
    return pl.pallas_call(
        functools.partial(_expert_ffn_kernel, t_max),
        grid_spec=grid_spec,
        out_shape=jax.ShapeDtypeStruct((t_max * tm, D_MODEL), F32),
        compiler_params=_cparams(("arbitrary",), VMEM_LIMIT),
        name="moe_expert_ffn",
    )(tile_base, inv, x1, w_gate, w_up, w_down)


def _combine_kernel(pos1_ref, pos2_ref, pos1n_ref, pos2n_ref, x1_ref, route_ref, ys_ref, p_ref, wpg_ref,
                    wpp_ref, g_ref, b_ref, o_ref, y1_ref, y2_ref, sem):
    i = pl.program_id(0)
    tm = x1_ref.shape[0]
    slot = i % 2

    def gather_rows(p1_ref, p2_ref, buf_slot):
        def issue(r, carry):
            pltpu.make_async_copy(ys_ref.at[pl.ds(p1_ref[r], 1), :],
                                  y1_ref.at[buf_slot, pl.ds(r, 1), :], sem.at[buf_slot]).start()
            pltpu.make_async_copy(ys_ref.at[pl.ds(p2_ref[r], 1), :],
                                  y2_ref.at[buf_slot, pl.ds(r, 1), :], sem.at[buf_slot]).start(priority=1)
            return carry
        lax.fori_loop(0, tm, issue, 0, unroll=8)

    @pl.when(i == 0)
    def _():
        gather_rows(pos1_ref, pos2_ref, 0)

    @pl.when(i + 1 < pl.num_programs(0))
    def _():
        gather_rows(pos1n_ref, pos2n_ref, 1 - slot)

    pltpu.make_async_copy(ys_ref.at[pl.ds(0, tm), :], y1_ref.at[slot], sem.at[slot]).wait()
    pltpu.make_async_copy(ys_ref.at[pl.ds(0, tm), :], y2_ref.at[slot], sem.at[slot]).wait()

    route = route_ref[...]
    m = route[:, 4:5] * y1_ref[slot] + route[:, 5:6] * y2_ref[slot]
    x2 = _layer_norm(DEEPNORM_ALPHA * x1_ref[...] + m, g_ref[...], b_ref[...])
    gate = jax.nn.sigmoid(jnp.dot(x2.astype(BF16), wpg_ref[...], preferred_element_type=F32))
    proj = jnp.dot(p_ref[...].astype(BF16), wpp_ref[...], preferred_element_type=F32)
    o_ref[...] = x2 + gate * proj


def _combine(pos1, pos2, x1, route, ys, p2, w_ple_gate_b, w_ple_proj_b, ln_g, ln_b):
    n = x1.shape[0]
    tm = TM_ROW
    smem = pl.BlockSpec((tm,), lambda i: (i,), memory_space=pltpu.SMEM)
    smem_next = pl.BlockSpec((tm,), lambda i: (jnp.minimum(i + 1, n // tm - 1),), memory_space=pltpu.SMEM)
    row = lambda w: pl.BlockSpec((1, w), lambda i: (0, 0))
    return pl.pallas_call(
        _combine_kernel,
        grid=(n // tm,),
        in_specs=[smem, smem, smem_next, smem_next,
                  pl.BlockSpec((tm, D_MODEL), lambda i: (i, 0)),
                  pl.BlockSpec((tm, LANES), lambda i: (i, 0)),
                  pl.BlockSpec(memory_space=pl.ANY),
                  pl.BlockSpec((tm, PLE_DIM), lambda i: (i, 0)),
                  pl.BlockSpec((D_MODEL, D_MODEL), lambda i: (0, 0)),
                  pl.BlockSpec((PLE_DIM, D_MODEL), lambda i: (0, 0)),
                  row(D_MODEL), row(D_MODEL)],
        out_specs=pl.BlockSpec((tm, D_MODEL), lambda i: (i, 0)),
        out_shape=jax.ShapeDtypeStruct((n, D_MODEL), F32),
        scratch_shapes=[pltpu.VMEM((2, tm, D_MODEL), F32), pltpu.VMEM((2, tm, D_MODEL), F32),
                        pltpu.SemaphoreType.DMA((2,))],
        compiler_params=_cparams(("arbitrary",), VMEM_LIMIT),
        name="moe_combine_ln2_ple",
    )(pos1, pos2, pos1, pos2, x1, route, ys, p2, w_ple_gate_b, w_ple_proj_b, ln_g, ln_b)


def _tile_plan(counts):
    tm = TM_EXP
    cnt = counts[0, :N_EXPERTS].astype(jnp.int32)
    tiles = (cnt + tm - 1) // tm
    return jnp.concatenate([jnp.zeros((1,), jnp.int32), jnp.cumsum(tiles).astype(jnp.int32)])


def kernel(x, p, positions, w_in, w_out, da_lambda_q1, da_lambda_k1, da_lambda_q2, da_lambda_k2, da_subln_w, ln1_g, ln1_b, w_router_group, b_router_group, w_router_expert, b_router_expert, w_exp_gate, w_exp_up, w_exp_down, ln2_g, ln2_b, w_ple_gate, w_ple_proj):
    batch, seq, d = x.shape
    n = batch * seq
    assert d == D_MODEL and w_in.shape[0] == DEPTH == 1
    l = 0
    lam_init = 0.8 - 0.6 * math.exp(-0.3 * l)
    x2 = x.reshape(n, d)

    tab = _rope_tables(positions)
    qkv = _in_proj(x2, w_in[l].astype(BF16), tab)
    oda = _diff_attn(qkv, da_lambda_q1[l], da_lambda_k1[l], da_lambda_q2[l], da_lambda_k2[l],
                     da_subln_w[l], lam_init, batch, seq)
    o_r = _retention(qkv, batch, seq)

    pad = LANES - N_GROUPS - N_EXPERTS
    w_router = jnp.concatenate([w_router_group[l], w_router_expert[l], jnp.zeros((d, pad), F32)], axis=1)
    b_router = jnp.concatenate([b_router_group[l], b_router_expert[l], jnp.zeros((pad,), F32)]).reshape(1, LANES)
    x1, route, counts = _out_proj(oda, o_r, x2, w_out[l].astype(BF16),
                                  ln1_g[l].reshape(1, d), ln1_b[l].reshape(1, d), w_router, b_router)

    t_max = (2 * n) // TM_EXP + N_EXPERTS
    tile_base = _tile_plan(counts)
    pos = _slots(route, counts)
    pos1, pos2 = pos[0], pos[1]
    inv = _inverse_map(pos1, pos2, t_max * TM_EXP)
    ys = _expert_ffn(tile_base, inv, x1, w_exp_gate[l], w_exp_up[l], w_exp_down[l], t_max)
    out = _combine(pos1, pos2, x1, route, ys, p[l].reshape(n, PLE_DIM),
                   w_ple_gate[l].astype(BF16), w_ple_proj[l].astype(BF16),
                   ln2_g[l].reshape(1, d), ln2_b[l].reshape(1, d))
    return out.reshape(batch, seq, d)
```

```python
import functools
import math

import numpy as np
import jax
import jax.numpy as jnp
from jax import lax
from jax.experimental import pallas as pl
from jax.experimental.pallas import tpu as pltpu

D_MODEL = 2048
DA_HEADS = 8
DA_HEAD_DIM = 64
DA_V_DIM = 128
RET_HEADS = 4
RET_K_DIM = 128
RET_V_DIM = 256
RET_CHUNK = 128
ROPE_THETA = 10000.0
PLE_DIM = 256
N_GROUPS = 4
EXPERTS_PER_GROUP = 8
N_EXPERTS = 32
EXPERT_FF = 512
DEPTH = 1
DEEPNORM_ALPHA = (2 * DEPTH) ** 0.25
LN_EPS = 1e-5
IN_WIDTH = 6144
DA_WIDTH = 1024
RET_WIDTH = 1024

OFF_QDA, OFF_KDA, OFF_VDA, OFF_QR, OFF_KR, OFF_VR, OFF_GR = 0, 1024, 2048, 3072, 3584, 4096, 5120

LANES = 128
SUBLANES = 8
VMEM_LIMIT = 56 * 1024 * 1024

BF16 = jnp.bfloat16
F32 = jnp.float32

TM_PROJ = 512
TN_PROJ = 512
TQ = 256
TM_OUT = 512
TM_ROW = 256
TM_EXP = 256


def _cparams(sem, vmem=None):
    return pltpu.CompilerParams(dimension_semantics=sem, vmem_limit_bytes=vmem)


def _rope_table_kernel(pos_ref, inv_ref, sgn_ref, tab_ref):
    pos = pos_ref[...].astype(F32)
    ang = pos * inv_ref[...]
    c = jnp.cos(ang)
    s = jnp.sin(ang)
    lane = lax.broadcasted_iota(jnp.int32, ang.shape, 1)
    low = lane < 64
    cr = pltpu.roll(c, 64, 1)
    sr = pltpu.roll(s, 64, 1)
    sgn = sgn_ref[...]
    tab_ref[:, 0:128] = jnp.where(low, cr, c)
    tab_ref[:, 128:256] = jnp.where(low, sr, s) * sgn[0:1, :]
    tab_ref[:, 256:384] = jnp.where(low, c, cr)
    tab_ref[:, 384:512] = jnp.where(low, s, sr) * sgn[1:2, :]


def _rope_tables(positions):
    n = positions.size
    tm = 1024
    i128 = np.power(ROPE_THETA, -np.arange(0, 128, 2, dtype=np.float64) / 128)
    i64 = np.power(ROPE_THETA, -np.arange(0, 64, 2, dtype=np.float64) / 64)
    inv = np.concatenate([i128, i64, i64]).astype(np.float32)[None, :]
    lane = np.arange(128)
    sgn = np.stack([np.where(lane % 64 < 32, -1.0, 1.0), np.where(lane < 64, -1.0, 1.0)]).astype(np.float32)
    sgn = np.concatenate([sgn, np.zeros((6, 128), np.float32)])
    return pl.pallas_call(
        _rope_table_kernel,
        grid=(n // tm,),
        in_specs=[pl.BlockSpec((tm, 1), lambda i: (i, 0)),
                  pl.BlockSpec((1, 128), lambda i: (0, 0)),
                  pl.BlockSpec((8, 128), lambda i: (0, 0))],
        out_specs=pl.BlockSpec((tm, 512), lambda i: (i, 0)),
        out_shape=jax.ShapeDtypeStruct((n, 512), F32),
        compiler_params=_cparams(("arbitrary",)),
        name="rope_tables",
    )(positions.reshape(n, 1), jnp.asarray(inv), jnp.asarray(sgn))


def _rope_cols(acc, cos, sin, half, scale):
    outs = []
    lane = lax.broadcasted_iota(jnp.int32, (acc.shape[0], LANES), 1)
    for c in range(acc.shape[1] // LANES):
        t = acc[:, c * LANES:(c + 1) * LANES]
        if half == 64:
            partner = pltpu.roll(t, 64, 1)
        else:
            partner = jnp.where((lane & 32) == 0, pltpu.roll(t, 96, 1), pltpu.roll(t, 32, 1))
        o = t * cos + partner * sin
        if scale != 1.0:
            o = o * scale
        outs.append(o)
    return jnp.concatenate(outs, axis=1)


def _in_proj_kernel(x_ref, w_ref, tab_ref, o_ref):
    xb = x_ref[...].astype(BF16)
    for j in range(IN_WIDTH // TN_PROJ):
        lo = j * TN_PROJ
        acc = jnp.dot(xb, w_ref[:, lo:lo + TN_PROJ], preferred_element_type=F32)
        if lo < OFF_VDA:
            acc = _rope_cols(acc, tab_ref[:, 0:128], tab_ref[:, 128:256], 32, 1.0)
        elif OFF_QR <= lo < OFF_KR:
            acc = _rope_cols(acc, tab_ref[:, 256:384], tab_ref[:, 384:512], 64, 1.0)
        elif OFF_KR <= lo < OFF_VR:
            acc = _rope_cols(acc, tab_ref[:, 256:384], tab_ref[:, 384:512], 64, RET_K_DIM ** -0.5)
        o_ref[:, lo:lo + TN_PROJ] = acc.astype(BF16)


def _in_proj(x2, w_in_b, tab):
    n = x2.shape[0]
    return pl.pallas_call(
        _in_proj_kernel,
        grid=(n // TM_PROJ,),
        in_specs=[pl.BlockSpec((TM_PROJ, D_MODEL), lambda i: (i, 0)),
                  pl.BlockSpec((D_MODEL, IN_WIDTH), lambda i: (0, 0), pipeline_mode=pl.Buffered(1)),
                  pl.BlockSpec((TM_PROJ, 512), lambda i: (i, 0))],
        out_specs=pl.BlockSpec((TM_PROJ, IN_WIDTH), lambda i: (i, 0)),
        out_shape=jax.ShapeDtypeStruct((n, IN_WIDTH), BF16),
        compiler_params=_cparams(("arbitrary",), VMEM_LIMIT),
        name="in_proj",
    )(x2, w_in_b, tab)


def _diff_attn_kernel(lam_init, q_ref, k_ref, v_ref, lq1_ref, lk1_ref, lq2_ref, lk2_ref, sw_ref,
                      o_ref, m_ref, l_ref, acc_ref):
    seq = q_ref.shape[0]
    lam = (jnp.exp(jnp.sum(lq1_ref[...] * lk1_ref[...], axis=-1, keepdims=True))
           - jnp.exp(jnp.sum(lq2_ref[...] * lk2_ref[...], axis=-1, keepdims=True)) + lam_init)
    scale = jnp.asarray(DA_HEAD_DIM ** -0.5, BF16)
    lane = lax.broadcasted_iota(jnp.int32, (TQ, LANES), 1)
    row_i = lax.broadcasted_iota(jnp.int32, (2 * TQ, TQ), 0)
    col_i = lax.broadcasted_iota(jnp.int32, (2 * TQ, TQ), 1)
    causal = col_i <= jnp.where(row_i >= TQ, row_i - TQ, row_i)

    def step(qs, start, width, masked):
        k = k_ref[pl.ds(start, width), :]
        v = v_ref[pl.ds(start, width), :]
        s = lax.dot_general(qs, k, (((1,), (1,)), ((), ())), preferred_element_type=F32)
        if masked:
            s = jnp.where(causal, s, jnp.finfo(F32).min)
        chunks = [s[:, c * LANES:(c + 1) * LANES] for c in range(width // LANES)]
        m_old = m_ref[...]
        m_new = jnp.maximum(m_old, jnp.max(functools.reduce(jnp.maximum, chunks), axis=-1, keepdims=True))
        alpha = jnp.exp(m_old - m_new)
        p = jnp.concatenate([jnp.exp(c - m_new).astype(BF16) for c in chunks], axis=1)
        v_ext = jnp.concatenate([v, jnp.ones_like(v)], axis=1)
        pv = jnp.dot(p, v_ext, preferred_element_type=F32)
        acc_ref[...] = alpha * acc_ref[...] + pv[:, 0:LANES]
        l_ref[...] = alpha * l_ref[...] + pv[:, LANES:2 * LANES]
        m_ref[...] = m_new

    for qi in range(seq // TQ):
        q = q_ref[qi * TQ:(qi + 1) * TQ, :]
        zero = jnp.zeros_like(q)
        qs = jnp.concatenate([jnp.where(lane < 64, q, zero), jnp.where(lane >= 64, q, zero)], axis=0) * scale
        m_ref[...] = jnp.full(m_ref.shape, -jnp.inf, F32)
        l_ref[...] = jnp.zeros(l_ref.shape, F32)
        acc_ref[...] = jnp.zeros(acc_ref.shape, F32)

        def pair(j, carry, qs=qs):
            step(qs, pl.multiple_of(j * (2 * TQ), 2 * TQ), 2 * TQ, False)
            return carry

        lax.fori_loop(0, qi // 2, pair, 0, unroll=True)
        if qi % 2:
            step(qs, (qi - 1) * TQ, TQ, False)
        step(qs, qi * TQ, TQ, True)

        a = acc_ref[...] / l_ref[...]
        o = a[0:TQ, :] - lam * a[TQ:2 * TQ, :]
        o = o * lax.rsqrt(jnp.mean(o * o, axis=-1, keepdims=True) + LN_EPS)
        o = o * sw_ref[...] * (1.0 - lam_init)
        o_ref[qi * TQ:(qi + 1) * TQ, :] = o.astype(BF16)


def _diff_attn(qkv, lq1, lk1, lq2, lk2, subln_w, lam_init, batch, seq):
    n = qkv.shape[0]
    qb0 = OFF_QDA // LANES
    kb0 = OFF_KDA // LANES
    vb0 = OFF_VDA // LANES
    vec = lambda a: a.reshape(1, -1).astype(F32)
    small = lambda w: pl.BlockSpec((1, w), lambda b, h: (0, 0))
    return pl.pallas_call(
        functools.partial(_diff_attn_kernel, lam_init),
        grid=(batch, DA_HEADS),
        in_specs=[pl.BlockSpec((seq, LANES), lambda b, h: (b, qb0 + h)),
                  pl.BlockSpec((seq, LANES), lambda b, h: (b, kb0 + h)),
                  pl.BlockSpec((seq, LANES), lambda b, h: (b, vb0 + h)),
                  small(64), small(64), small(64), small(64), small(128)],
        out_specs=pl.BlockSpec((seq, LANES), lambda b, h: (b, h)),
        out_shape=jax.ShapeDtypeStruct((n, DA_WIDTH), BF16),
        scratch_shapes=[pltpu.VMEM((2 * TQ, LANES), F32), pltpu.VMEM((2 * TQ, LANES), F32),
                        pltpu.VMEM((2 * TQ, LANES), F32)],
        compiler_params=_cparams(("arbitrary", "arbitrary")),
        name="diff_attn",
    )(qkv, qkv, qkv, vec(lq1), vec(lk1), vec(lq2), vec(lk2), vec(subln_w))


def _retention_kernel(q_ref, k_ref, v_ref, g_ref, lg_ref, o_ref, state_ref):
    C = RET_CHUNK
    lg = lg_ref[0, 0:1, :]
    lg_col = lg[:, 0:1]
    n_i = lax.broadcasted_iota(jnp.int32, (C, C), 0)
    m_i = lax.broadcasted_iota(jnp.int32, (C, C), 1)
    rel = (n_i - m_i).astype(F32)
    decay = jnp.where(rel >= 0, jnp.exp(rel * lg_col), 0.0)
    n_col = lax.broadcasted_iota(jnp.int32, (C, 1), 0).astype(F32)
    zeta = jnp.exp((C - 1.0 - n_col) * lg_col)
    xi = jnp.exp((n_col + 1.0) * lg_col)
    chunk_decay = jnp.exp(C * lg_col)
    state_ref[...] = jnp.zeros(state_ref.shape, F32)

    def body(c, carry):
        start = pl.multiple_of(c * C, C)
        q = q_ref[pl.ds(start, C), :]
        k = k_ref[pl.ds(start, C), :]
        v = v_ref[pl.ds(start, C), :]
        g = g_ref[pl.ds(start, C), :].astype(F32)
        qk = lax.dot_general(q, k, (((1,), (1,)), ((), ())), preferred_element_type=F32)
        inner = (qk * decay).astype(BF16)
        inner_o = jnp.dot(inner, v, preferred_element_type=F32)
        state = state_ref[...]
        cross = jnp.dot(q, state.astype(BF16), preferred_element_type=F32)
        y = inner_o + cross * xi
        kz_t = (k.astype(F32) * zeta).T.astype(BF16)
        kv = jnp.dot(kz_t, v, preferred_element_type=F32)
        state_ref[...] = chunk_decay * state + kv
        mu = jnp.mean(y, axis=-1, keepdims=True)
        yc = y - mu
        var = jnp.mean(yc * yc, axis=-1, keepdims=True)
        yn = yc * lax.rsqrt(var + LN_EPS)
        o_ref[pl.ds(start, C), :] = (g * jax.nn.sigmoid(g) * yn).astype(BF16)
        return carry

    lax.fori_loop(0, q_ref.shape[0] // C, body, 0, unroll=4)


def _retention(qkv, batch, seq):
    n = qkv.shape[0]
    lg = np.log(1.0 - np.power(2.0, -5.0 - np.arange(RET_HEADS, dtype=np.float64)))
    lg_tab = np.broadcast_to(lg[:, None, None], (RET_HEADS, SUBLANES, LANES)).astype(np.float32)
    qb0 = OFF_QR // RET_K_DIM
    kb0 = OFF_KR // RET_K_DIM
    vb0 = OFF_VR // RET_V_DIM
    gb0 = OFF_GR // RET_V_DIM
    return pl.pallas_call(
        _retention_kernel,
        grid=(batch, RET_HEADS),
        in_specs=[pl.BlockSpec((seq, RET_K_DIM), lambda b, h: (b, qb0 + h)),
                  pl.BlockSpec((seq, RET_K_DIM), lambda b, h: (b, kb0 + h)),
                  pl.BlockSpec((seq, RET_V_DIM), lambda b, h: (b, vb0 + h)),
                  pl.BlockSpec((seq, RET_V_DIM), lambda b, h: (b, gb0 + h)),
                  pl.BlockSpec((1, SUBLANES, LANES), lambda b, h: (h, 0, 0))],
        out_specs=pl.BlockSpec((seq, RET_V_DIM), lambda b, h: (b, h)),
        out_shape=jax.ShapeDtypeStruct((n, RET_WIDTH), BF16),
        scratch_shapes=[pltpu.VMEM((RET_K_DIM, RET_V_DIM), F32)],
        compiler_params=_cparams(("arbitrary", "arbitrary")),
        name="retention",
    )(qkv, qkv, qkv, qkv, jnp.asarray(lg_tab))


def _layer_norm(z, g, b):
    mu = jnp.mean(z, axis=-1, keepdims=True)
    zc = z - mu
    var = jnp.mean(zc * zc, axis=-1, keepdims=True)
    return zc * lax.rsqrt(var + LN_EPS) * g + b


def _split_bf16(a):
    hi = a.astype(BF16)
    lo = (a - hi.astype(F32)).astype(BF16)
    return hi, lo


def _out_proj_kernel(oda_ref, or_ref, x_ref, w_ref, g_ref, b_ref, wr_ref, br_ref,
                     x1_ref, route_ref, cnt_ref, run_ref):
    i = pl.program_id(0)
    tm = x_ref.shape[0]

    @pl.when(i == 0)
    def _():
        run_ref[...] = jnp.zeros(run_ref.shape, F32)

    h = jnp.dot(oda_ref[...], w_ref[0:DA_WIDTH, :], preferred_element_type=F32)
    h = h + jnp.dot(or_ref[...], w_ref[DA_WIDTH:DA_WIDTH + RET_WIDTH, :], preferred_element_type=F32)
    x1 = _layer_norm(DEEPNORM_ALPHA * x_ref[...] + h, g_ref[...], b_ref[...])
    x1_ref[...] = x1

    xh, xl = _split_bf16(x1)
    wh, wl = _split_bf16(wr_ref[...])
    logits = (jnp.dot(xh, wh, preferred_element_type=F32) + jnp.dot(xl, wh, preferred_element_type=F32)
              + jnp.dot(xh, wl, preferred_element_type=F32)) + br_ref[...]
    lane = lax.broadcasted_iota(jnp.int32, logits.shape, 1).astype(F32)
    neg = jnp.float32(-jnp.inf)
    big = jnp.float32(1 << 20)
    is_g = lane < N_GROUPS
    gl = jnp.where(is_g, logits, neg)
    gmax = jnp.max(gl, axis=-1, keepdims=True)
    g_idx = jnp.min(jnp.where(is_g & (gl == gmax), lane, big), axis=-1, keepdims=True)
    g_w = 1.0 / jnp.sum(jnp.where(is_g, jnp.exp(gl - gmax), 0.0), axis=-1, keepdims=True)
    e_lane = lane - N_GROUPS
    in_grp = (e_lane >= g_idx * EXPERTS_PER_GROUP) & (e_lane < (g_idx + 1.0) * EXPERTS_PER_GROUP)
    el = jnp.where(in_grp, logits, neg)
    v1 = jnp.max(el, axis=-1, keepdims=True)
    l1 = jnp.min(jnp.where(in_grp & (el == v1), lane, big), axis=-1, keepdims=True)
    el2 = jnp.where(lane == l1, neg, el)
    v2 = jnp.max(el2, axis=-1, keepdims=True)
    l2 = jnp.min(jnp.where(in_grp & (lane != l1) & (el2 == v2), lane, big), axis=-1, keepdims=True)
    t = jnp.exp(v2 - v1)
    fw1 = g_w / (1.0 + t)
    fw2 = g_w * t / (1.0 + t)
    e1 = l1 - N_GROUPS
    e2 = l2 - N_GROUPS

    oh1 = (lane == e1).astype(F32)
    oh2 = (lane == e2).astype(F32)
    cnt = oh1 + oh2
    r_i = lax.broadcasted_iota(jnp.int32, (tm, tm), 0)
    c_i = lax.broadcasted_iota(jnp.int32, (tm, tm), 1)
    tri = (c_i < r_i).astype(BF16)
    before = jnp.dot(tri, cnt.astype(BF16), preferred_element_type=F32) + run_ref[...]
    rank1 = jnp.sum(oh1 * before, axis=-1, keepdims=True)
    rank2 = jnp.sum(oh2 * before, axis=-1, keepdims=True)
    run_new = run_ref[...] + jnp.sum(cnt, axis=0, keepdims=True)
    run_ref[...] = run_new
    cnt_ref[...] = jnp.broadcast_to(run_new, cnt_ref.shape)

    route = jnp.where(lane == 0, e1, 0.0)
    route = jnp.where(lane == 1, e2, route)
    route = jnp.where(lane == 2, rank1, route)
    route = jnp.where(lane == 3, rank2, route)
    route = jnp.where(lane == 4, fw1, route)
    route = jnp.where(lane == 5, fw2, route)
    route_ref[...] = route


def _out_proj(oda, o_r, x2, w_out_b, ln_g, ln_b, w_router, b_router):
    n = x2.shape[0]
    tm = TM_OUT
    row = lambda w: pl.BlockSpec((1, w), lambda i: (0, 0))
    return pl.pallas_call(
        _out_proj_kernel,
        grid=(n // tm,),
        in_specs=[pl.BlockSpec((tm, DA_WIDTH), lambda i: (i, 0)),
                  pl.BlockSpec((tm, RET_WIDTH), lambda i: (i, 0)),
                  pl.BlockSpec((tm, D_MODEL), lambda i: (i, 0)),
                  pl.BlockSpec((D_MODEL, D_MODEL), lambda i: (0, 0), pipeline_mode=pl.Buffered(1)),
                  row(D_MODEL), row(D_MODEL),
                  pl.BlockSpec((D_MODEL, LANES), lambda i: (0, 0)),
                  row(LANES)],
        out_specs=[pl.BlockSpec((tm, D_MODEL), lambda i: (i, 0)),
                   pl.BlockSpec((tm, LANES), lambda i: (i, 0)),
                   pl.BlockSpec((SUBLANES, LANES), lambda i: (0, 0))],
        out_shape=[jax.ShapeDtypeStruct((n, D_MODEL), F32),
                   jax.ShapeDtypeStruct((n, LANES), F32),
                   jax.ShapeDtypeStruct((SUBLANES, LANES), F32)],
        scratch_shapes=[pltpu.VMEM((1, LANES), F32)],
        compiler_params=_cparams(("arbitrary",), VMEM_LIMIT),
        name="out_proj_ln1_router",
    )(oda, o_r, x2, w_out_b, ln_g, ln_b, w_router, b_router)


def _slot_kernel(route_ref, cnt_ref, pos_ref):
    route = route_ref[...]
    lane = lax.broadcasted_iota(jnp.int32, (1, LANES), 1)
    cnt = jnp.where(lane < N_EXPERTS, cnt_ref[0:1, :], 0.0)
    padded = jnp.floor((cnt + (TM_EXP - 1)) * (1.0 / TM_EXP)) * TM_EXP
    ends = padded
    for sh in (1, 2, 4, 8, 16):
        ends = ends + jnp.where(lane >= sh, pltpu.roll(ends, sh, 1), 0.0)
    offs = ends - padded
    lane_f = lax.broadcasted_iota(jnp.int32, route.shape, 1).astype(F32)
    off1 = jnp.sum(jnp.where(lane_f == route[:, 0:1], offs, 0.0), axis=-1, keepdims=True)
    off2 = jnp.sum(jnp.where(lane_f == route[:, 1:2], offs, 0.0), axis=-1, keepdims=True)
    slots = jnp.where(lane_f == 0.0, off1 + route[:, 2:3], 0.0)
    slots = jnp.where(lane_f == 1.0, off2 + route[:, 3:4], slots)
    pos_ref[...] = slots.T[0:SUBLANES, :].astype(jnp.int32)


def _slots(route, counts):
    n = route.shape[0]
    tm = 1024
    return pl.pallas_call(
        _slot_kernel,
        grid=(n // tm,),
        in_specs=[pl.BlockSpec((tm, LANES), lambda i: (i, 0)),
                  pl.BlockSpec((SUBLANES, LANES), lambda i: (0, 0))],
        out_specs=pl.BlockSpec((SUBLANES, tm), lambda i: (0, i)),
        out_shape=jax.ShapeDtypeStruct((SUBLANES, n), jnp.int32),
        compiler_params=_cparams(("arbitrary",)),
        name="moe_slots",
    )(route, counts)


def _inverse_kernel(pos1_ref, pos2_ref, inv_ref):
    i = pl.program_id(0)
    tm = pos1_ref.shape[0]

    @pl.when(i == 0)
    def _():
        n_tok = tm * pl.num_programs(0)

        def fill(j, carry):
            inv_ref[j] = lax.rem(j, n_tok)
            return carry
        lax.fori_loop(0, inv_ref.shape[0], fill, 0, unroll=8)

    def scatter(r, carry):
        tok = i * tm + r
        inv_ref[pos1_ref[r]] = tok
        inv_ref[pos2_ref[r]] = tok
        return carry

    lax.fori_loop(0, tm, scatter, 0, unroll=8)


def _inverse_map(pos1, pos2, p_rows):
    n = pos1.shape[0]
    tm = 1024
    smem = pl.BlockSpec((tm,), lambda i: (i,), memory_space=pltpu.SMEM)
    return pl.pallas_call(
        _inverse_kernel,
        grid=(n // tm,),
        in_specs=[smem, smem],
        out_specs=pl.BlockSpec(memory_space=pltpu.SMEM),
        out_shape=jax.ShapeDtypeStruct((p_rows,), jnp.int32),
        compiler_params=_cparams(("arbitrary",)),
        name="moe_inverse_map",
    )(pos1, pos2)


def _expert_ffn_kernel(t_max, tb_ref, inv_ref, x1_ref, wg_ref, wu_ref, wd_ref, ys_ref,
                       xbuf_ref, ybuf_ref, zbuf_ref, wgb_ref, wub_ref, wdb_ref, gsem, osem, zsem):
    e = pl.program_id(0)
    tm = TM_EXP
    g_first = tb_ref[e]
    g_end = tb_ref[e + 1]
    n_total = tb_ref[N_EXPERTS]

    def gather_rows(g, buf_slot):
        def issue(r, carry):
            pltpu.make_async_copy(x1_ref.at[pl.ds(inv_ref[g * tm + r], 1), :],
                                  xbuf_ref.at[buf_slot, pl.ds(r, 1), :], gsem.at[buf_slot]).start(priority=1)
            return carry
        lax.fori_loop(0, tm, issue, 0, unroll=8)

    def tile_out(g):
        return ys_ref.at[pl.ds(pl.multiple_of(g * tm, tm), tm), :]

    @pl.when(e == 0)
    def _():
        gather_rows(0, 0)
        zbuf_ref[...] = jnp.zeros(zbuf_ref.shape, F32)

        def fill(g, carry):
            pltpu.make_async_copy(zbuf_ref, tile_out(g), zsem).start()
            return carry
        lax.fori_loop(n_total, t_max, fill, 0)

    @pl.when(g_end > g_first)
    def _():
        wgb_ref[...] = wg_ref[0].astype(BF16)
        wub_ref[...] = wu_ref[0].astype(BF16)
        wdb_ref[...] = wd_ref[0].astype(BF16)

    def tile(g, carry):
        slot = g % 2

        @pl.when(g + 1 < n_total)
        def _():
            gather_rows(g + 1, 1 - slot)

        pltpu.make_async_copy(x1_ref.at[pl.ds(0, tm), :], xbuf_ref.at[slot], gsem.at[slot]).wait()
        x = xbuf_ref[slot].astype(BF16)
        gate = jnp.dot(x, wgb_ref[...], preferred_element_type=F32)
        up = jnp.dot(x, wub_ref[...], preferred_element_type=F32)
        hmid = (gate * jax.nn.sigmoid(gate) * up).astype(BF16)
        y = jnp.dot(hmid, wdb_ref[...], preferred_element_type=F32)

        @pl.when(g >= 2)
        def _():
            pltpu.make_async_copy(ybuf_ref.at[slot], tile_out(g), osem.at[slot]).wait()

        ybuf_ref[slot] = y
        pltpu.make_async_copy(ybuf_ref.at[slot], tile_out(g), osem.at[slot]).start()
        return carry

    lax.fori_loop(g_first, g_end, tile, 0)

    @pl.when(e == pl.num_programs(0) - 1)
    def _():
        pltpu.make_async_copy(ybuf_ref.at[0], tile_out(0), osem.at[0]).wait()
        pltpu.make_async_copy(ybuf_ref.at[1], tile_out(0), osem.at[1]).wait()

        def drain(g, carry):
            pltpu.make_async_copy(zbuf_ref, tile_out(g), zsem).wait()
            return carry
        lax.fori_loop(n_total, t_max, drain, 0)


def _expert_ffn(tile_base, inv, x1, w_gate, w_up, w_down, t_max):
    tm = TM_EXP
    assert 2 * x1.shape[0] >= 2 * tm
    grid_spec = pltpu.PrefetchScalarGridSpec(
        num_scalar_prefetch=2,
        grid=(N_EXPERTS,),
        in_specs=[pl.BlockSpec(memory_space=pl.ANY),
                  pl.BlockSpec((1, D_MODEL, EXPERT_FF), lambda e, tb, inv: (e, 0, 0)),
                  pl.BlockSpec((1, D_MODEL, EXPERT_FF), lambda e, tb, inv: (e, 0, 0)),
                  pl.BlockSpec((1, EXPERT_FF, D_MODEL), lambda e, tb, inv: (e, 0, 0))],
        out_specs=pl.BlockSpec(memory_space=pl.ANY),
        scratch_shapes=[pltpu.VMEM((2, tm, D_MODEL), F32),
                        pltpu.VMEM((2, tm, D_MODEL), F32),
                        pltpu.VMEM((tm, D_MODEL), F32),
                        pltpu.VMEM((D_MODEL, EXPERT_FF), BF16),
                        pltpu.VMEM((D_MODEL, EXPERT_FF), BF16),
                        pltpu.VMEM((EXPERT_FF, D_MODEL), BF16),
                        pltpu.SemaphoreType.DMA((2,)),
                        pltpu.SemaphoreType.DMA((2,)),
                        pltpu.SemaphoreType.DMA],
    )
    return pl.pallas_call(
        functools.partial(_expert_ffn_kernel, t_max),
        grid_spec=grid_spec,
        out_shape=jax.ShapeDtypeStruct((t_max * tm, D_MODEL), F32),
        compiler_params=_cparams(("arbitrary",), VMEM_LIMIT),
        name="moe_expert_ffn",
    )(tile_base, inv, x1, w_gate, w_up, w_down)


def _combine_kernel(pos1_ref, pos2_ref, pos1n_ref, pos2n_ref, x1_ref, route_ref, ys_ref, p_ref, wpg_ref,
                    wpp_ref, g_ref, b_ref, o_ref, y1_ref, y2_ref, sem):
    i = pl.program_id(0)
    tm = x1_ref.shape[0]
    slot = i % 2

    def gather_rows(p1_ref, p2_ref, buf_slot):
        def issue(r, carry):
            pltpu.make_async_copy(ys_ref.at[pl.ds(p1_ref[r], 1), :],
                                  y1_ref.at[buf_slot, pl.ds(r, 1), :], sem.at[buf_slot]).start()
            pltpu.make_async_copy(ys_ref.at[pl.ds(p2_ref[r], 1), :],
                                  y2_ref.at[buf_slot, pl.ds(r, 1), :], sem.at[buf_slot]).start(priority=1)
            return carry
        lax.fori_loop(0, tm, issue, 0, unroll=8)

    @pl.when(i == 0)
    def _():
        gather_rows(pos1_ref, pos2_ref, 0)

    @pl.when(i + 1 < pl.num_programs(0))
    def _():
        gather_rows(pos1n_ref, pos2n_ref, 1 - slot)

    pltpu.make_async_copy(ys_ref.at[pl.ds(0, tm), :], y1_ref.at[slot], sem.at[slot]).wait()
    pltpu.make_async_copy(ys_ref.at[pl.ds(0, tm), :], y2_ref.at[slot], sem.at[slot]).wait()

    route = route_ref[...]
    m = route[:, 4:5] * y1_ref[slot] + route[:, 5:6] * y2_ref[slot]
    x2 = _layer_norm(DEEPNORM_ALPHA * x1_ref[...] + m, g_ref[...], b_ref[...])
    gate = jax.nn.sigmoid(jnp.dot(x2.astype(BF16), wpg_ref[...], preferred_element_type=F32))
    proj = jnp.dot(p_ref[...].astype(BF16), wpp_ref[...], preferred_element_type=F32)
    o_ref[...] = x2 + gate * proj


def _combine(pos1, pos2, x1, route, ys, p2, w_ple_gate_b, w_ple_proj_b, ln_g, ln_b):
    n = x1.shape[0]
    tm = TM_ROW
    smem = pl.BlockSpec((tm,), lambda i: (i,), memory_space=pltpu.SMEM)
    smem_next = pl.BlockSpec((tm,), lambda i: (jnp.minimum(i + 1, n // tm - 1),), memory_space=pltpu.SMEM)
    row = lambda w: pl.BlockSpec((1, w), lambda i: (0, 0))
    return pl.pallas_call(
        _combine_kernel,
        grid=(n // tm,),
        in_specs=[smem, smem, smem_next, smem_next,
                  pl.BlockSpec((tm, D_MODEL), lambda i: (i, 0)),
                  pl.BlockSpec((tm, LANES), lambda i: (i, 0)),
                  pl.BlockSpec(memory_space=pl.ANY),
                  pl.BlockSpec((tm, PLE_DIM), lambda i: (i, 0)),
                  pl.BlockSpec((D_MODEL, D_MODEL), lambda i: (0, 0)),
                  pl.BlockSpec((PLE_DIM, D_MODEL), lambda i: (0, 0)),
                  row(D_MODEL), row(D_MODEL)],
        out_specs=pl.BlockSpec((tm, D_MODEL), lambda i: (i, 0)),
        out_shape=jax.ShapeDtypeStruct((n, D_MODEL), F32),
        scratch_shapes=[pltpu.VMEM((2, tm, D_MODEL), F32), pltpu.VMEM((2, tm, D_MODEL), F32),
                        pltpu.SemaphoreType.DMA((2,))],
        compiler_params=_cparams(("arbitrary",), VMEM_LIMIT),
        name="moe_combine_ln2_ple",
    )(pos1, pos2, pos1, pos2, x1, route, ys, p2, w_ple_gate_b, w_ple_proj_b, ln_g, ln_b)


def _tile_plan(counts):
    tm = TM_EXP
    cnt = counts[0, :N_EXPERTS].astype(jnp.int32)
    tiles = (cnt + tm - 1) // tm
    return jnp.concatenate([jnp.zeros((1,), jnp.int32), jnp.cumsum(tiles).astype(jnp.int32)])


def kernel(x, p, positions, w_in, w_out, da_lambda_q1, da_lambda_k1, da_lambda_q2, da_lambda_k2, da_subln_w, ln1_g, ln1_b, w_router_group, b_router_group, w_router_expert, b_router_expert, w_exp_gate, w_exp_up, w_exp_down, ln2_g, ln2_b, w_ple_gate, w_ple_proj):
    batch, seq, d = x.shape
    n = batch * seq
    assert d == D_MODEL and w_in.shape[0] == DEPTH == 1
    l = 0
    lam_init = 0.8 - 0.6 * math.exp(-0.3 * l)
    x2 = x.reshape(n, d)

    tab = _rope_tables(positions)
    qkv = _in_proj(x2, w_in[l].astype(BF16), tab)
    oda = _diff_attn(qkv, da_lambda_q1[l], da_lambda_k1[l], da_lambda_q2[l], da_lambda_k2[l],
                     da_subln_w[l], lam_init, batch, seq)
    o_r = _retention(qkv, batch, seq)

    pad = LANES - N_GROUPS - N_EXPERTS
    w_router = jnp.concatenate([w_router_group[l], w_router_expert[l], jnp.zeros((d, pad), F32)], axis=1)
    b_router = jnp.concatenate([b_router_group[l], b_router_expert[l], jnp.zeros((pad,), F32)]).reshape(1, LANES)
    x1, route, counts = _out_proj(oda, o_r, x2, w_out[l].astype(BF16),
                                  ln1_g[l].reshape(1, d), ln1_b[l].reshape(1, d), w_router, b_router)

    t_max = (2 * n) // TM_EXP + N_EXPERTS
    tile_base = _tile_plan(counts)
    pos = _slots(route, counts)
    pos1, pos2 = pos[0], pos[1]
    inv = _inverse_map(pos1, pos2, t_max * TM_EXP)
    ys = _expert_ffn(tile_base, inv, x1, w_exp_gate[l], w_exp_up[l], w_exp_down[l], t_max)
    out = _combine(pos1, pos2, x1, route, ys, p[l].reshape(n, PLE_DIM),
                   w_ple_gate[l].astype(BF16), w_ple_proj[l].astype(BF16),
                   ln2_g[l].reshape(1, d), ln2_b[l].reshape(1, d))
    return out.reshape(batch, seq, d)
```

```python
import functools
import math

import numpy as np
import jax
import jax.numpy as jnp
from jax import lax
from jax.experimental import pallas as pl
from jax.experimental.pallas import tpu as pltpu

D_MODEL = 2048
DA_HEADS = 8
DA_HEAD_DIM = 64
DA_V_DIM = 128
RET_HEADS = 4
RET_K_DIM = 128
RET_V_DIM = 256
RET_CHUNK = 128
ROPE_THETA = 10000.0
PLE_DIM = 256
N_GROUPS = 4
EXPERTS_PER_GROUP = 8
N_EXPERTS = 32
EXPERT_FF = 512
DEPTH = 1
DEEPNORM_ALPHA = (2 * DEPTH) ** 0.25
LN_EPS = 1e-5
IN_WIDTH = 6144
DA_WIDTH = 1024
RET_WIDTH = 1024

OFF_QDA, OFF_KDA, OFF_VDA, OFF_QR, OFF_KR, OFF_VR, OFF_GR = 0, 1024, 2048, 3072, 3584, 4096, 5120

LANES = 128
SUBLANES = 8
VMEM_LIMIT = 56 * 1024 * 1024

BF16 = jnp.bfloat16
F32 = jnp.float32

TM_PROJ = 512
TN_PROJ = 512
TQ = 256
TM_OUT = 512
SUB_OUT = 512
SUB_COMB = 256
TM_ROW = 512
TM_EXP = 256
GATHER_GROUP = 8


def _cparams(sem, vmem=None):
    return pltpu.CompilerParams(dimension_semantics=sem, vmem_limit_bytes=vmem)


def _rope_table_kernel(pos_ref, inv_ref, sgn_ref, tab_ref):
    pos = pos_ref[...].astype(F32)
    ang = pos * inv_ref[...]
    c = jnp.cos(ang)
    s = jnp.sin(ang)
    lane = lax.broadcasted_iota(jnp.int32, ang.shape, 1)
    low = lane < 64
    cr = pltpu.roll(c, 64, 1)
    sr = pltpu.roll(s, 64, 1)
    sgn = sgn_ref[...]
    tab_ref[:, 0:128] = jnp.where(low, cr, c)
    tab_ref[:, 128:256] = jnp.where(low, sr, s) * sgn[0:1, :]
    tab_ref[:, 256:384] = jnp.where(low, c, cr)
    tab_ref[:, 384:512] = jnp.where(low, s, sr) * sgn[1:2, :]


def _rope_tables(positions):
    n = positions.size
    tm = 1024
    i128 = np.power(ROPE_THETA, -np.arange(0, 128, 2, dtype=np.float64) / 128)
    i64 = np.power(ROPE_THETA, -np.arange(0, 64, 2, dtype=np.float64) / 64)
    inv = np.concatenate([i128, i64, i64]).astype(np.float32)[None, :]
    lane = np.arange(128)
    sgn = np.stack([np.where(lane % 64 < 32, -1.0, 1.0), np.where(lane < 64, -1.0, 1.0)]).astype(np.float32)
    sgn = np.concatenate([sgn, np.zeros((6, 128), np.float32)])
    return pl.pallas_call(
        _rope_table_kernel,
        grid=(n // tm,),
        in_specs=[pl.BlockSpec((tm, 1), lambda i: (i, 0)),
                  pl.BlockSpec((1, 128), lambda i: (0, 0)),
                  pl.BlockSpec((8, 128), lambda i: (0, 0))],
        out_specs=pl.BlockSpec((tm, 512), lambda i: (i, 0)),
        out_shape=jax.ShapeDtypeStruct((n, 512), F32),
        compiler_params=_cparams(("arbitrary",)),
        name="rope_tables",
    )(positions.reshape(n, 1), jnp.asarray(inv), jnp.asarray(sgn))


def _rope_cols(acc, cos, sin, half, scale):
    outs = []
    lane = lax.broadcasted_iota(jnp.int32, (acc.shape[0], LANES), 1)
    for c in range(acc.shape[1] // LANES):
        t = acc[:, c * LANES:(c + 1) * LANES]
        if half == 64:
            partner = pltpu.roll(t, 64, 1)
        else:
            partner = jnp.where((lane & 32) == 0, pltpu.roll(t, 96, 1), pltpu.roll(t, 32, 1))
        o = t * cos + partner * sin
        if scale != 1.0:
            o = o * scale
        outs.append(o)
    return jnp.concatenate(outs, axis=1)


def _in_proj_kernel(x_ref, w_ref, tab_ref, o_ref):
    xb = x_ref[...].astype(BF16)
    for j in range(IN_WIDTH // TN_PROJ):
        lo = j * TN_PROJ
        acc = jnp.dot(xb, w_ref[:, lo:lo + TN_PROJ], preferred_element_type=F32)
        if lo < OFF_VDA:
            acc = _rope_cols(acc, tab_ref[:, 0:128], tab_ref[:, 128:256], 32, 1.0)
        elif OFF_QR <= lo < OFF_KR:
            acc = _rope_cols(acc, tab_ref[:, 256:384], tab_ref[:, 384:512], 64, 1.0)
        elif OFF_KR <= lo < OFF_VR:
            acc = _rope_cols(acc, tab_ref[:, 256:384], tab_ref[:, 384:512], 64, RET_K_DIM ** -0.5)
        o_ref[:, lo:lo + TN_PROJ] = acc.astype(BF16)


def _in_proj(x2, w_in_b, tab):
    n = x2.shape[0]
    return pl.pallas_call(
        _in_proj_kernel,
        grid=(n // TM_PROJ,),
        in_specs=[pl.BlockSpec((TM_PROJ, D_MODEL), lambda i: (i, 0)),
                  pl.BlockSpec((D_MODEL, IN_WIDTH), lambda i: (0, 0), pipeline_mode=pl.Buffered(1)),
                  pl.BlockSpec((TM_PROJ, 512), lambda i: (i, 0))],
        out_specs=pl.BlockSpec((TM_PROJ, IN_WIDTH), lambda i: (i, 0)),
        out_shape=jax.ShapeDtypeStruct((n, IN_WIDTH), BF16),
        compiler_params=_cparams(("arbitrary",), VMEM_LIMIT),
        name="in_proj",
    )(x2, w_in_b, tab)


def _diff_attn_kernel(lam_init, q_ref, k_ref, v_ref, lq1_ref, lk1_ref, lq2_ref, lk2_ref, sw_ref,
                      o_ref, m_ref, l_ref, acc_ref):
    seq = q_ref.shape[0]
    lam = (jnp.exp(jnp.sum(lq1_ref[...] * lk1_ref[...], axis=-1, keepdims=True))
           - jnp.exp(jnp.sum(lq2_ref[...] * lk2_ref[...], axis=-1, keepdims=True)) + lam_init)
    scale = jnp.asarray(DA_HEAD_DIM ** -0.5, BF16)
    lane = lax.broadcasted_iota(jnp.int32, (TQ, LANES), 1)
    row_i = lax.broadcasted_iota(jnp.int32, (2 * TQ, TQ), 0)
    col_i = lax.broadcasted_iota(jnp.int32, (2 * TQ, TQ), 1)
    causal = col_i <= jnp.where(row_i >= TQ, row_i - TQ, row_i)

    def step(qs, start, width, masked):
        k = k_ref[pl.ds(start, width), :]
        v = v_ref[pl.ds(start, width), :]
        s = lax.dot_general(qs, k, (((1,), (1,)), ((), ())), preferred_element_type=F32)
        if masked:
            s = jnp.where(causal, s, jnp.finfo(F32).min)
        chunks = [s[:, c * LANES:(c + 1) * LANES] for c in range(width // LANES)]
        m_old = m_ref[...]
        m_new = jnp.maximum(m_old, jnp.max(functools.reduce(jnp.maximum, chunks), axis=-1, keepdims=True))
        alpha = jnp.exp(m_old - m_new)
        p = jnp.concatenate([jnp.exp(c - m_new).astype(BF16) for c in chunks], axis=1)
        v_ext = jnp.concatenate([v, jnp.ones_like(v)], axis=1)
        pv = jnp.dot(p, v_ext, preferred_element_type=F32)
        acc_ref[...] = alpha * acc_ref[...] + pv[:, 0:LANES]
        l_ref[...] = alpha * l_ref[...] + pv[:, LANES:2 * LANES]
        m_ref[...] = m_new

    for qi in range(seq // TQ):
        q = q_ref[qi * TQ:(qi + 1) * TQ, :]
        zero = jnp.zeros_like(q)
        qs = jnp.concatenate([jnp.where(lane < 64, q, zero), jnp.where(lane >= 64, q, zero)], axis=0) * scale
        m_ref[...] = jnp.full(m_ref.shape, -jnp.inf, F32)
        l_ref[...] = jnp.zeros(l_ref.shape, F32)
        acc_ref[...] = jnp.zeros(acc_ref.shape, F32)

        def pair(j, carry, qs=qs):
            step(qs, pl.multiple_of(j * (2 * TQ), 2 * TQ), 2 * TQ, False)
            return carry

        lax.fori_loop(0, qi // 2, pair, 0, unroll=True)
        if qi % 2:
            step(qs, (qi - 1) * TQ, TQ, False)
        step(qs, qi * TQ, TQ, True)

        a = acc_ref[...] / l_ref[...]
        o = a[0:TQ, :] - lam * a[TQ:2 * TQ, :]
        o = o * lax.rsqrt(jnp.mean(o * o, axis=-1, keepdims=True) + LN_EPS)
        o = o * sw_ref[...] * (1.0 - lam_init)
        o_ref[qi * TQ:(qi + 1) * TQ, :] = o.astype(BF16)


def _diff_attn(qkv, lq1, lk1, lq2, lk2, subln_w, lam_init, batch, seq):
    n = qkv.shape[0]
    qb0 = OFF_QDA // LANES
    kb0 = OFF_KDA // LANES
    vb0 = OFF_VDA // LANES
    vec = lambda a: a.reshape(1, -1).astype(F32)
    small = lambda w: pl.BlockSpec((1, w), lambda b, h: (0, 0))
    return pl.pallas_call(
        functools.partial(_diff_attn_kernel, lam_init),
        grid=(batch, DA_HEADS),
        in_specs=[pl.BlockSpec((seq, LANES), lambda b, h: (b, qb0 + h)),
                  pl.BlockSpec((seq, LANES), lambda b, h: (b, kb0 + h)),
                  pl.BlockSpec((seq, LANES), lambda b, h: (b, vb0 + h)),
                  small(64), small(64), small(64), small(64), small(128)],
        out_specs=pl.BlockSpec((seq, LANES), lambda b, h: (b, h)),
        out_shape=jax.ShapeDtypeStruct((n, DA_WIDTH), BF16),
        scratch_shapes=[pltpu.VMEM((2 * TQ, LANES), F32), pltpu.VMEM((2 * TQ, LANES), F32),
                        pltpu.VMEM((2 * TQ, LANES), F32)],
        compiler_params=_cparams(("arbitrary", "arbitrary")),
        name="diff_attn",
    )(qkv, qkv, qkv, vec(lq1), vec(lk1), vec(lq2), vec(lk2), vec(subln_w))


def _retention_kernel(q_ref, k_ref, v_ref, g_ref, lg_ref, o_ref, state_ref):
    C = RET_CHUNK
    lg = lg_ref[0, 0:1, :]
    lg_col = lg[:, 0:1]
    n_i = lax.broadcasted_iota(jnp.int32, (C, C), 0)
    m_i = lax.broadcasted_iota(jnp.int32, (C, C), 1)
    rel = (n_i - m_i).astype(F32)
    decay = jnp.where(rel >= 0, jnp.exp(rel * lg_col), 0.0)
    n_col = lax.broadcasted_iota(jnp.int32, (C, 1), 0).astype(F32)
    zeta = jnp.exp((C - 1.0 - n_col) * lg_col)
    xi = jnp.exp((n_col + 1.0) * lg_col)
    chunk_decay = jnp.exp(C * lg_col)
    state_ref[...] = jnp.zeros(state_ref.shape, F32)

    def body(c, carry):
        start = pl.multiple_of(c * C, C)
        q = q_ref[pl.ds(start, C), :]
        k = k_ref[pl.ds(start, C), :]
        v = v_ref[pl.ds(start, C), :]
        g = g_ref[pl.ds(start, C), :].astype(F32)
        qk = lax.dot_general(q, k, (((1,), (1,)), ((), ())), preferred_element_type=F32)
        inner = (qk * decay).astype(BF16)
        inner_o = jnp.dot(inner, v, preferred_element_type=F32)
        state = state_ref[...]
        cross = jnp.dot(q, state.astype(BF16), preferred_element_type=F32)
        y = inner_o + cross * xi
        kz_t = (k.astype(F32) * zeta).T.astype(BF16)
        kv = jnp.dot(kz_t, v, preferred_element_type=F32)
        state_ref[...] = chunk_decay * state + kv
        mu = jnp.mean(y, axis=-1, keepdims=True)
        yc = y - mu
        var = jnp.mean(yc * yc, axis=-1, keepdims=True)
        yn = yc * lax.rsqrt(var + LN_EPS)
        o_ref[pl.ds(start, C), :] = (g * jax.nn.sigmoid(g) * yn).astype(BF16)
        return carry

    lax.fori_loop(0, q_ref.shape[0] // C, body, 0, unroll=4)


def _retention(qkv, batch, seq):
    n = qkv.shape[0]
    lg = np.log(1.0 - np.power(2.0, -5.0 - np.arange(RET_HEADS, dtype=np.float64)))
    lg_tab = np.broadcast_to(lg[:, None, None], (RET_HEADS, SUBLANES, LANES)).astype(np.float32)
    qb0 = OFF_QR // RET_K_DIM
    kb0 = OFF_KR // RET_K_DIM
    vb0 = OFF_VR // RET_V_DIM
    gb0 = OFF_GR // RET_V_DIM
    return pl.pallas_call(
        _retention_kernel,
        grid=(batch, RET_HEADS),
        in_specs=[pl.BlockSpec((seq, RET_K_DIM), lambda b, h: (b, qb0 + h)),
                  pl.BlockSpec((seq, RET_K_DIM), lambda b, h: (b, kb0 + h)),
                  pl.BlockSpec((seq, RET_V_DIM), lambda b, h: (b, vb0 + h)),
                  pl.BlockSpec((seq, RET_V_DIM), lambda b, h: (b, gb0 + h)),
                  pl.BlockSpec((1, SUBLANES, LANES), lambda b, h: (h, 0, 0))],
        out_specs=pl.BlockSpec((seq, RET_V_DIM), lambda b, h: (b, h)),
        out_shape=jax.ShapeDtypeStruct((n, RET_WIDTH), BF16),
        scratch_shapes=[pltpu.VMEM((RET_K_DIM, RET_V_DIM), F32)],
        compiler_params=_cparams(("arbitrary", "arbitrary")),
        name="retention",
    )(qkv, qkv, qkv, qkv, jnp.asarray(lg_tab))


def _layer_norm(z, g, b):
    mu = jnp.mean(z, axis=-1, keepdims=True)
    zc = z - mu
    var = jnp.mean(zc * zc, axis=-1, keepdims=True)
    return zc * lax.rsqrt(var + LN_EPS) * g + b


def _split_bf16(a):
    hi = a.astype(BF16)
    lo = (a - hi.astype(F32)).astype(BF16)
    return hi, lo


def _out_proj_kernel(oda_ref, or_ref, x_ref, w_ref, g_ref, b_ref, wr_ref, br_ref,
                     x1_ref, route_ref, cnt_ref, run_ref):
    i = pl.program_id(0)

    @pl.when(i == 0)
    def _():
        run_ref[...] = jnp.zeros(run_ref.shape, F32)

    wh, wl = _split_bf16(wr_ref[...])
    run = run_ref[...]
    for s in range(x_ref.shape[0] // SUB_OUT):
        rows = slice(s * SUB_OUT, (s + 1) * SUB_OUT)
        run = _out_proj_rows(oda_ref[rows, :], or_ref[rows, :], x_ref[rows, :], w_ref, g_ref[...], b_ref[...],
                             wh, wl, br_ref[...], run, x1_ref.at[rows, :], route_ref.at[rows, :])
    run_ref[...] = run
    cnt_ref[...] = jnp.broadcast_to(run, cnt_ref.shape)


def _out_proj_rows(oda, o_r, x, w_ref, ln_g, ln_b, wh, wl, b_router, run, x1_ref, route_ref):
    tm = x.shape[0]
    h = jnp.dot(oda, w_ref[0:DA_WIDTH, :], preferred_element_type=F32)
    h = h + jnp.dot(o_r, w_ref[DA_WIDTH:DA_WIDTH + RET_WIDTH, :], preferred_element_type=F32)
    x1 = _layer_norm(DEEPNORM_ALPHA * x + h, ln_g, ln_b)
    x1_ref[...] = x1

    xh, xl = _split_bf16(x1)
    logits = (jnp.dot(xh, wh, preferred_element_type=F32) + jnp.dot(xl, wh, preferred_element_type=F32)
              + jnp.dot(xh, wl, preferred_element_type=F32)) + b_router
    lane = lax.broadcasted_iota(jnp.int32, logits.shape, 1).astype(F32)
    neg = jnp.float32(-jnp.inf)
    big = jnp.float32(1 << 20)
    is_g = lane < N_GROUPS
    gl = jnp.where(is_g, logits, neg)
    gmax = jnp.max(gl, axis=-1, keepdims=True)
    g_idx = jnp.min(jnp.where(is_g & (gl == gmax), lane, big), axis=-1, keepdims=True)
    g_w = 1.0 / jnp.sum(jnp.where(is_g, jnp.exp(gl - gmax), 0.0), axis=-1, keepdims=True)
    e_lane = lane - N_GROUPS
    in_grp = (e_lane >= g_idx * EXPERTS_PER_GROUP) & (e_lane < (g_idx + 1.0) * EXPERTS_PER_GROUP)
    el = jnp.where(in_grp, logits, neg)
    v1 = jnp.max(el, axis=-1, keepdims=True)
    l1 = jnp.min(jnp.where(in_grp & (el == v1), lane, big), axis=-1, keepdims=True)
    el2 = jnp.where(lane == l1, neg, el)
    v2 = jnp.max(el2, axis=-1, keepdims=True)
    l2 = jnp.min(jnp.where(in_grp & (lane != l1) & (el2 == v2), lane, big), axis=-1, keepdims=True)
    t = jnp.exp(v2 - v1)
    fw1 = g_w / (1.0 + t)
    fw2 = g_w * t / (1.0 + t)
    e1 = l1 - N_GROUPS
    e2 = l2 - N_GROUPS

    oh1 = (lane == e1).astype(F32)
    oh2 = (lane == e2).astype(F32)
    cnt = oh1 + oh2
    r_i = lax.broadcasted_iota(jnp.int32, (tm, tm), 0)
    c_i = lax.broadcasted_iota(jnp.int32, (tm, tm), 1)
    tri = (c_i < r_i).astype(BF16)
    before = jnp.dot(tri, cnt.astype(BF16), preferred_element_type=F32) + run
    rank1 = jnp.sum(oh1 * before, axis=-1, keepdims=True)
    rank2 = jnp.sum(oh2 * before, axis=-1, keepdims=True)

    route = jnp.where(lane == 0, e1, 0.0)
    route = jnp.where(lane == 1, e2, route)
    route = jnp.where(lane == 2, rank1, route)
    route = jnp.where(lane == 3, rank2, route)
    route = jnp.where(lane == 4, fw1, route)
    route = jnp.where(lane == 5, fw2, route)
    route_ref[...] = route
    return run + jnp.sum(cnt, axis=0, keepdims=True)


def _out_proj(oda, o_r, x2, w_out_b, ln_g, ln_b, w_router, b_router):
    n = x2.shape[0]
    tm = TM_OUT
    row = lambda w: pl.BlockSpec((1, w), lambda i: (0, 0))
    return pl.pallas_call(
        _out_proj_kernel,
        grid=(n // tm,),
        in_specs=[pl.BlockSpec((tm, DA_WIDTH), lambda i: (i, 0)),
                  pl.BlockSpec((tm, RET_WIDTH), lambda i: (i, 0)),
                  pl.BlockSpec((tm, D_MODEL), lambda i: (i, 0)),
                  pl.BlockSpec((D_MODEL, D_MODEL), lambda i: (0, 0), pipeline_mode=pl.Buffered(1)),
                  row(D_MODEL), row(D_MODEL),
                  pl.BlockSpec((D_MODEL, LANES), lambda i: (0, 0)),
                  row(LANES)],
        out_specs=[pl.BlockSpec((tm, D_MODEL), lambda i: (i, 0)),
                   pl.BlockSpec((tm, LANES), lambda i: (i, 0)),
                   pl.BlockSpec((SUBLANES, LANES), lambda i: (0, 0))],
        out_shape=[jax.ShapeDtypeStruct((n, D_MODEL), F32),
                   jax.ShapeDtypeStruct((n, LANES), F32),
                   jax.ShapeDtypeStruct((SUBLANES, LANES), F32)],
        scratch_shapes=[pltpu.VMEM((1, LANES), F32)],
        compiler_params=_cparams(("arbitrary",), VMEM_LIMIT),
        name="out_proj_ln1_router",
    )(oda, o_r, x2, w_out_b, ln_g, ln_b, w_router, b_router)


def _slot_kernel(route_ref, cnt_ref, pos_ref):
    route = route_ref[...]
    lane = lax.broadcasted_iota(jnp.int32, (1, LANES), 1)
    cnt = jnp.where(lane < N_EXPERTS, cnt_ref[0:1, :], 0.0)
    padded = jnp.floor((cnt + (TM_EXP - 1)) * (1.0 / TM_EXP)) * TM_EXP
    ends = padded
    for sh in (1, 2, 4, 8, 16):
        ends = ends + jnp.where(lane >= sh, pltpu.roll(ends, sh, 1), 0.0)
    offs = ends - padded
    lane_f = lax.broadcasted_iota(jnp.int32, route.shape, 1).astype(F32)
    off1 = jnp.sum(jnp.where(lane_f == route[:, 0:1], offs, 0.0), axis=-1, keepdims=True)
    off2 = jnp.sum(jnp.where(lane_f == route[:, 1:2], offs, 0.0), axis=-1, keepdims=True)
    slots = jnp.where(lane_f == 0.0, off1 + route[:, 2:3], 0.0)
    slots = jnp.where(lane_f == 1.0, off2 + route[:, 3:4], slots)
    pos_ref[...] = slots.T[0:SUBLANES, :].astype(jnp.int32)


def _slots(route, counts):
    n = route.shape[0]
    tm = 1024
    return pl.pallas_call(
        _slot_kernel,
        grid=(n // tm,),
        in_specs=[pl.BlockSpec((tm, LANES), lambda i: (i, 0)),
                  pl.BlockSpec((SUBLANES, LANES), lambda i: (0, 0))],
        out_specs=pl.BlockSpec((SUBLANES, tm), lambda i: (0, i)),
        out_shape=jax.ShapeDtypeStruct((SUBLANES, n), jnp.int32),
        compiler_params=_cparams(("arbitrary",)),
        name="moe_slots",
    )(route, counts)


def _inverse_kernel(n_tok, pos1_ref, pos2_ref, inv_ref):
    i = pl.program_id(0)
    tm = pos1_ref.shape[0]
    n_slots = inv_ref.shape[0]

    @pl.when(i == 0)
    def _():
        for base in range(0, n_slots, n_tok):
            def fill(j, carry, base=base):
                inv_ref[base + j] = j
                return carry
            lax.fori_loop(0, min(n_tok, n_slots - base), fill, 0, unroll=8)

    def scatter(r, carry):
        tok = i * tm + r
        inv_ref[pos1_ref[r]] = tok
        inv_ref[pos2_ref[r]] = tok
        return carry

    lax.fori_loop(0, tm, scatter, 0, unroll=8)


def _inverse_map(pos1, pos2, p_rows):
    n = pos1.shape[0]
    tm = 1024
    smem = pl.BlockSpec((tm,), lambda i: (i,), memory_space=pltpu.SMEM)
    return pl.pallas_call(
        functools.partial(_inverse_kernel, n),
        grid=(n // tm,),
        in_specs=[smem, smem],
        out_specs=pl.BlockSpec(memory_space=pltpu.SMEM),
        out_shape=jax.ShapeDtypeStruct((p_rows,), jnp.int32),
        compiler_params=_cparams(("arbitrary",)),
        name="moe_inverse_map",
    )(pos1, pos2)


def _expert_ffn_kernel(t_max, tb_ref, tv_ref, inv_ref, x1_ref, wg_ref, wu_ref, wd_ref, ys_ref,
                       xbuf_ref, ybuf_ref, zbuf_ref, wgb_ref, wub_ref, wdb_ref, gsem, osem, zsem):
    e = pl.program_id(0)
    tm = TM_EXP
    g_first = tb_ref[e]
    g_end = tb_ref[e + 1]
    n_total = tb_ref[N_EXPERTS]

    def gathered_rows(g):
        return ((tv_ref[g] + (GATHER_GROUP - 1)) // GATHER_GROUP) * GATHER_GROUP

    def gather_rows(g, buf_slot):
        def issue(q, carry):
            for k in range(GATHER_GROUP):
                r = q * GATHER_GROUP + k
                pltpu.make_async_copy(x1_ref.at[pl.ds(inv_ref[g * tm + r], 1), :],
                                      xbuf_ref.at[buf_slot, pl.ds(r, 1), :], gsem.at[buf_slot]).start()
            return carry
        lax.fori_loop(0, gathered_rows(g) // GATHER_GROUP, issue, 0)

    def wait_rows(g, buf_slot):
        rows = gathered_rows(g)
        k = GATHER_GROUP
        while k <= tm:
            @pl.when((rows & k) != 0)
            def _(k=k):
                pltpu.make_async_copy(x1_ref.at[pl.ds(0, k), :], xbuf_ref.at[buf_slot, pl.ds(0, k), :],
                                      gsem.at[buf_slot]).wait()
            k *= 2

    def tile_out(g):
        return ys_ref.at[pl.ds(pl.multiple_of(g * tm, tm), tm), :]

    @pl.when(e == 0)
    def _():
        xbuf_ref[...] = jnp.zeros(xbuf_ref.shape, F32)
        gather_rows(0, 0)
        zbuf_ref[...] = jnp.zeros(zbuf_ref.shape, F32)

        def fill(g, carry):
            pltpu.make_async_copy(zbuf_ref, tile_out(g), zsem).start()
            return carry
        lax.fori_loop(n_total, t_max, fill, 0)

    @pl.when(g_end > g_first)
    def _():
        wgb_ref[...] = wg_ref[0].astype(BF16)
        wub_ref[...] = wu_ref[0].astype(BF16)
        wdb_ref[...] = wd_ref[0].astype(BF16)

    def tile(g, carry):
        slot = g % 2

        @pl.when(g + 1 < n_total)
        def _():
            gather_rows(g + 1, 1 - slot)

        wait_rows(g, slot)
        x = xbuf_ref[slot].astype(BF16)
        gate = jnp.dot(x, wgb_ref[...], preferred_element_type=F32)
        up = jnp.dot(x, wub_ref[...], preferred_element_type=F32)
        hmid = (gate * jax.nn.sigmoid(gate) * up).astype(BF16)
        y = jnp.dot(hmid, wdb_ref[...], preferred_element_type=F32)

        @pl.when(g >= 2)
        def _():
            pltpu.make_async_copy(ybuf_ref.at[slot], tile_out(g), osem.at[slot]).wait()

        ybuf_ref[slot] = y
        pltpu.make_async_copy(ybuf_ref.at[slot], tile_out(g), osem.at[slot]).start()
        return carry

    lax.fori_loop(g_first, g_end, tile, 0)

    @pl.when(e == pl.num_programs(0) - 1)
    def _():
        pltpu.make_async_copy(ybuf_ref.at[0], tile_out(0), osem.at[0]).wait()
        pltpu.make_async_copy(ybuf_ref.at[1], tile_out(0), osem.at[1]).wait()

        def drain(g, carry):
            pltpu.make_async_copy(zbuf_ref, tile_out(g), zsem).wait()
            return carry
        lax.fori_loop(n_total, t_max, drain, 0)


def _expert_ffn(tile_base, tile_valid, inv, x1, w_gate, w_up, w_down, t_max):
    tm = TM_EXP
    assert 2 * x1.shape[0] >= 2 * tm
    grid_spec = pltpu.PrefetchScalarGridSpec(
        num_scalar_prefetch=3,
        grid=(N_EXPERTS,),
        in_specs=[pl.BlockSpec(memory_space=pl.ANY),
                  pl.BlockSpec((1, D_MODEL, EXPERT_FF), lambda e, tb, tv, inv: (e, 0, 0)),
                  pl.BlockSpec((1, D_MODEL, EXPERT_FF), lambda e, tb, tv, inv: (e, 0, 0)),
                  pl.BlockSpec((1, EXPERT_FF, D_MODEL), lambda e, tb, tv, inv: (e, 0, 0))],
        out_specs=pl.BlockSpec(memory_space=pl.ANY),
        scratch_shapes=[pltpu.VMEM((2, tm, D_MODEL), F32),
                        pltpu.VMEM((2, tm, D_MODEL), F32),
                        pltpu.VMEM((tm, D_MODEL), F32),
                        pltpu.VMEM((D_MODEL, EXPERT_FF), BF16),
                        pltpu.VMEM((D_MODEL, EXPERT_FF), BF16),
                        pltpu.VMEM((EXPERT_FF, D_MODEL), BF16),
                        pltpu.SemaphoreType.DMA((2,)),
                        pltpu.SemaphoreType.DMA((2,)),
                        pltpu.SemaphoreType.DMA],
    )
    return pl.pallas_call(
        functools.partial(_expert_ffn_kernel, t_max),
        grid_spec=grid_spec,
        out_shape=jax.ShapeDtypeStruct((t_max * tm, D_MODEL), F32),
        compiler_params=_cparams(("arbitrary",), VMEM_LIMIT),
        name="moe_expert_ffn",
    )(tile_base, tile_valid, inv, x1, w_gate, w_up, w_down)


def _combine_kernel(pos1_ref, pos2_ref, pos1n_ref, pos2n_ref, x1_ref, route_ref, ys_ref, p_ref, wpg_ref,
                    wpp_ref, g_ref, b_ref, o_ref, y1_ref, y2_ref, sem):
    i = pl.program_id(0)
    tm = x1_ref.shape[0]
    slot = i % 2

    def gather_rows(p1_ref, p2_ref, buf_slot):
        def issue(r, carry):
            pltpu.make_async_copy(ys_ref.at[pl.ds(p1_ref[r], 1), :],
                                  y1_ref.at[buf_slot, pl.ds(r, 1), :], sem.at[buf_slot]).start()
            pltpu.make_async_copy(ys_ref.at[pl.ds(p2_ref[r], 1), :],
                                  y2_ref.at[buf_slot, pl.ds(r, 1), :], sem.at[buf_slot]).start(priority=1)
            return carry
        lax.fori_loop(0, tm, issue, 0, unroll=8)

    @pl.when(i == 0)
    def _():
        gather_rows(pos1_ref, pos2_ref, 0)

    @pl.when(i + 1 < pl.num_programs(0))
    def _():
        gather_rows(pos1n_ref, pos2n_ref, 1 - slot)

    pltpu.make_async_copy(ys_ref.at[pl.ds(0, tm), :], y1_ref.at[slot], sem.at[slot]).wait()
    pltpu.make_async_copy(ys_ref.at[pl.ds(0, tm), :], y2_ref.at[slot], sem.at[slot]).wait()

    for s in range(tm // SUB_COMB):
        rows = pl.ds(s * SUB_COMB, SUB_COMB)
        route = route_ref[rows, :]
        m = route[:, 4:5] * y1_ref[slot, rows, :] + route[:, 5:6] * y2_ref[slot, rows, :]
        x2 = _layer_norm(DEEPNORM_ALPHA * x1_ref[rows, :] + m, g_ref[...], b_ref[...])
        gate = jax.nn.sigmoid(jnp.dot(x2.astype(BF16), wpg_ref[...], preferred_element_type=F32))
        proj = jnp.dot(p_ref[rows, :].astype(BF16), wpp_ref[...], preferred_element_type=F32)
        o_ref[rows, :] = x2 + gate * proj


def _combine(pos1, pos2, x1, route, ys, p2, w_ple_gate_b, w_ple_proj_b, ln_g, ln_b):
    n = x1.shape[0]
    tm = TM_ROW
    smem = pl.BlockSpec((tm,), lambda i: (i,), memory_space=pltpu.SMEM)
    smem_next = pl.BlockSpec((tm,), lambda i: (jnp.minimum(i + 1, n // tm - 1),), memory_space=pltpu.SMEM)
    row = lambda w: pl.BlockSpec((1, w), lambda i: (0, 0))
    return pl.pallas_call(
        _combine_kernel,
        grid=(n // tm,),
        in_specs=[smem, smem, smem_next, smem_next,
                  pl.BlockSpec((tm, D_MODEL), lambda i: (i, 0)),
                  pl.BlockSpec((tm, LANES), lambda i: (i, 0)),
                  pl.BlockSpec(memory_space=pl.ANY),
                  pl.BlockSpec((tm, PLE_DIM), lambda i: (i, 0)),
                  pl.BlockSpec((D_MODEL, D_MODEL), lambda i: (0, 0), pipeline_mode=pl.Buffered(1)),
                  pl.BlockSpec((PLE_DIM, D_MODEL), lambda i: (0, 0)),
                  row(D_MODEL), row(D_MODEL)],
        out_specs=pl.BlockSpec((tm, D_MODEL), lambda i: (i, 0)),
        out_shape=jax.ShapeDtypeStruct((n, D_MODEL), F32),
        scratch_shapes=[pltpu.VMEM((2, tm, D_MODEL), F32), pltpu.VMEM((2, tm, D_MODEL), F32),
                        pltpu.SemaphoreType.DMA((2,))],
        compiler_params=_cparams(("arbitrary",), VMEM_LIMIT),
        name="moe_combine_ln2_ple",
    )(pos1, pos2, pos1, pos2, x1, route, ys, p2, w_ple_gate_b, w_ple_proj_b, ln_g, ln_b)


def _tile_plan(counts, t_max):
    tm = TM_EXP
    cnt = counts[0, :N_EXPERTS].astype(jnp.int32)
    tiles = (cnt + tm - 1) // tm
    tile_base = jnp.concatenate([jnp.zeros((1,), jnp.int32), jnp.cumsum(tiles).astype(jnp.int32)])
    g = jnp.arange(t_max, dtype=jnp.int32)
    in_e = (g[:, None] >= tile_base[None, :-1]) & (g[:, None] < tile_base[None, 1:])
    rows = cnt[None, :] - (g[:, None] - tile_base[None, :-1]) * tm
    tile_valid = jnp.sum(jnp.where(in_e, jnp.clip(rows, 0, tm), 0), axis=1).astype(jnp.int32)
    return tile_base, tile_valid


def kernel(x, p, positions, w_in, w_out, da_lambda_q1, da_lambda_k1, da_lambda_q2, da_lambda_k2, da_subln_w, ln1_g, ln1_b, w_router_group, b_router_group, w_router_expert, b_router_expert, w_exp_gate, w_exp_up, w_exp_down, ln2_g, ln2_b, w_ple_gate, w_ple_proj):
    batch, seq, d = x.shape
    n = batch * seq
    assert d == D_MODEL and w_in.shape[0] == DEPTH == 1
    l = 0
    lam_init = 0.8 - 0.6 * math.exp(-0.3 * l)
    x2 = x.reshape(n, d)

    tab = _rope_tables(positions)
    qkv = _in_proj(x2, w_in[l].astype(BF16), tab)
    oda = _diff_attn(qkv, da_lambda_q1[l], da_lambda_k1[l], da_lambda_q2[l], da_lambda_k2[l],
                     da_subln_w[l], lam_init, batch, seq)
    o_r = _retention(qkv, batch, seq)

    pad = LANES - N_GROUPS - N_EXPERTS
    w_router = jnp.concatenate([w_router_group[l], w_router_expert[l], jnp.zeros((d, pad), F32)], axis=1)
    b_router = jnp.concatenate([b_router_group[l], b_router_expert[l], jnp.zeros((pad,), F32)]).reshape(1, LANES)
    x1, route, counts = _out_proj(oda, o_r, x2, w_out[l].astype(BF16),
                                  ln1_g[l].reshape(1, d), ln1_b[l].reshape(1, d), w_router, b_router)

    t_max = (2 * n) // TM_EXP + N_EXPERTS
    tile_base, tile_valid = _tile_plan(counts, t_max)
    pos = _slots(route, counts)
    pos1, pos2 = pos[0], pos[1]
    inv = _inverse_map(pos1, pos2, t_max * TM_EXP)
    ys = _expert_ffn(tile_base, tile_valid, inv, x1, w_exp_gate[l], w_exp_up[l], w_exp_down[l], t_max)
    out = _combine(pos1, pos2, x1, route, ys, p[l].reshape(n, PLE_DIM),
                   w_ple_gate[l].astype(BF16), w_ple_proj[l].astype(BF16),
                   ln2_g[l].reshape(1, d), ln2_b[l].reshape(1, d))
    return out.reshape(batch, seq, d)
```

```python
import functools
import math

import numpy as np
import jax
import jax.numpy as jnp
from jax import lax
from jax.experimental import pallas as pl
from jax.experimental.pallas import tpu as pltpu

D_MODEL = 2048
DA_HEADS = 8
DA_HEAD_DIM = 64
DA_V_DIM = 128
RET_HEADS = 4
RET_K_DIM = 128
RET_V_DIM = 256
RET_CHUNK = 128
ROPE_THETA = 10000.0
PLE_DIM = 256
N_GROUPS = 4
EXPERTS_PER_GROUP = 8
N_EXPERTS = 32
EXPERT_FF = 512
DEPTH = 1
DEEPNORM_ALPHA = (2 * DEPTH) ** 0.25
LN_EPS = 1e-5
IN_WIDTH = 6144
DA_WIDTH = 1024
RET_WIDTH = 1024

OFF_QDA, OFF_KDA, OFF_VDA, OFF_QR, OFF_KR, OFF_VR, OFF_GR = 0, 1024, 2048, 3072, 3584, 4096, 5120

LANES = 128
SUBLANES = 8
VMEM_LIMIT = 56 * 1024 * 1024

BF16 = jnp.bfloat16
F32 = jnp.float32

TM_PROJ = 512
TN_PROJ = 512
TQ = 256
TM_OUT = 512
SUB_COMB = 256
TM_ROW = 512
TM_EXP = 256
GATHER_GROUP = 8
GATHER_SLOTS = 3


def _cparams(sem, vmem=None):
    return pltpu.CompilerParams(dimension_semantics=sem, vmem_limit_bytes=vmem)


def _rope_table_kernel(pos_ref, inv_ref, sgn_ref, tab_ref):
    pos = pos_ref[...].astype(F32)
    ang = pos * inv_ref[...]
    c = jnp.cos(ang)
    s = jnp.sin(ang)
    lane = lax.broadcasted_iota(jnp.int32, ang.shape, 1)
    low = lane < 64
    cr = pltpu.roll(c, 64, 1)
    sr = pltpu.roll(s, 64, 1)
    sgn = sgn_ref[...]
    tab_ref[:, 0:128] = jnp.where(low, cr, c)
    tab_ref[:, 128:256] = jnp.where(low, sr, s) * sgn[0:1, :]
    tab_ref[:, 256:384] = jnp.where(low, c, cr)
    tab_ref[:, 384:512] = jnp.where(low, s, sr) * sgn[1:2, :]


def _rope_tables(positions):
    n = positions.size
    tm = 1024
    i128 = np.power(ROPE_THETA, -np.arange(0, 128, 2, dtype=np.float64) / 128)
    i64 = np.power(ROPE_THETA, -np.arange(0, 64, 2, dtype=np.float64) / 64)
    inv = np.concatenate([i128, i64, i64]).astype(np.float32)[None, :]
    lane = np.arange(128)
    sgn = np.stack([np.where(lane % 64 < 32, -1.0, 1.0), np.where(lane < 64, -1.0, 1.0)]).astype(np.float32)
    sgn = np.concatenate([sgn, np.zeros((6, 128), np.float32)])
    return pl.pallas_call(
        _rope_table_kernel,
        grid=(n // tm,),
        in_specs=[pl.BlockSpec((tm, 1), lambda i: (i, 0)),
                  pl.BlockSpec((1, 128), lambda i: (0, 0)),
                  pl.BlockSpec((8, 128), lambda i: (0, 0))],
        out_specs=pl.BlockSpec((tm, 512), lambda i: (i, 0)),
        out_shape=jax.ShapeDtypeStruct((n, 512), F32),
        compiler_params=_cparams(("arbitrary",)),
        name="rope_tables",
    )(positions.reshape(n, 1), jnp.asarray(inv), jnp.asarray(sgn))


def _rope_cols(acc, cos, sin, half, scale):
    outs = []
    lane = lax.broadcasted_iota(jnp.int32, (acc.shape[0], LANES), 1)
    for c in range(acc.shape[1] // LANES):
        t = acc[:, c * LANES:(c + 1) * LANES]
        if half == 64:
            partner = pltpu.roll(t, 64, 1)
        else:
            partner = jnp.where((lane & 32) == 0, pltpu.roll(t, 96, 1), pltpu.roll(t, 32, 1))
        o = t * cos + partner * sin
        if scale != 1.0:
            o = o * scale
        outs.append(o)
    return jnp.concatenate(outs, axis=1)


def _in_proj_kernel(x_ref, w_ref, tab_ref, o_ref):
    xb = x_ref[...].astype(BF16)
    for j in range(IN_WIDTH // TN_PROJ):
        lo = j * TN_PROJ
        acc = jnp.dot(xb, w_ref[:, lo:lo + TN_PROJ], preferred_element_type=F32)
        if lo < OFF_VDA:
            acc = _rope_cols(acc, tab_ref[:, 0:128], tab_ref[:, 128:256], 32, 1.0)
        elif OFF_QR <= lo < OFF_KR:
            acc = _rope_cols(acc, tab_ref[:, 256:384], tab_ref[:, 384:512], 64, 1.0)
        elif OFF_KR <= lo < OFF_VR:
            acc = _rope_cols(acc, tab_ref[:, 256:384], tab_ref[:, 384:512], 64, RET_K_DIM ** -0.5)
        o_ref[:, lo:lo + TN_PROJ] = acc.astype(BF16)


def _in_proj(x2, w_in_b, tab):
    n = x2.shape[0]
    return pl.pallas_call(
        _in_proj_kernel,
        grid=(n // TM_PROJ,),
        in_specs=[pl.BlockSpec((TM_PROJ, D_MODEL), lambda i: (i, 0)),
                  pl.BlockSpec((D_MODEL, IN_WIDTH), lambda i: (0, 0), pipeline_mode=pl.Buffered(1)),
                  pl.BlockSpec((TM_PROJ, 512), lambda i: (i, 0))],
        out_specs=pl.BlockSpec((TM_PROJ, IN_WIDTH), lambda i: (i, 0)),
        out_shape=jax.ShapeDtypeStruct((n, IN_WIDTH), BF16),
        compiler_params=_cparams(("arbitrary",), VMEM_LIMIT),
        name="in_proj",
    )(x2, w_in_b, tab)


def _diff_attn_kernel(lam_init, q_ref, k_ref, v_ref, lq1_ref, lk1_ref, lq2_ref, lk2_ref, sw_ref,
                      o_ref, m_ref, l_ref, acc_ref):
    seq = q_ref.shape[0]
    lam = (jnp.exp(jnp.sum(lq1_ref[...] * lk1_ref[...], axis=-1, keepdims=True))
           - jnp.exp(jnp.sum(lq2_ref[...] * lk2_ref[...], axis=-1, keepdims=True)) + lam_init)
    scale = jnp.asarray(DA_HEAD_DIM ** -0.5, BF16)
    lane = lax.broadcasted_iota(jnp.int32, (TQ, LANES), 1)
    row_i = lax.broadcasted_iota(jnp.int32, (2 * TQ, TQ), 0)
    col_i = lax.broadcasted_iota(jnp.int32, (2 * TQ, TQ), 1)
    causal = col_i <= jnp.where(row_i >= TQ, row_i - TQ, row_i)

    def step(qs, start, width, masked):
        k = k_ref[pl.ds(start, width), :]
        v = v_ref[pl.ds(start, width), :]
        s = lax.dot_general(qs, k, (((1,), (1,)), ((), ())), preferred_element_type=F32)
        if masked:
            s = jnp.where(causal, s, jnp.finfo(F32).min)
        chunks = [s[:, c * LANES:(c + 1) * LANES] for c in range(width // LANES)]
        m_old = m_ref[...]
        m_new = jnp.maximum(m_old, jnp.max(functools.reduce(jnp.maximum, chunks), axis=-1, keepdims=True))
        alpha = jnp.exp(m_old - m_new)
        p = jnp.concatenate([jnp.exp(c - m_new).astype(BF16) for c in chunks], axis=1)
        v_ext = jnp.concatenate([v, jnp.ones_like(v)], axis=1)
        pv = jnp.dot(p, v_ext, preferred_element_type=F32)
        acc_ref[...] = alpha * acc_ref[...] + pv[:, 0:LANES]
        l_ref[...] = alpha * l_ref[...] + pv[:, LANES:2 * LANES]
        m_ref[...] = m_new

    for qi in range(seq // TQ):
        q = q_ref[qi * TQ:(qi + 1) * TQ, :]
        zero = jnp.zeros_like(q)
        qs = jnp.concatenate([jnp.where(lane < 64, q, zero), jnp.where(lane >= 64, q, zero)], axis=0) * scale
        m_ref[...] = jnp.full(m_ref.shape, -jnp.inf, F32)
        l_ref[...] = jnp.zeros(l_ref.shape, F32)
        acc_ref[...] = jnp.zeros(acc_ref.shape, F32)

        def pair(j, carry, qs=qs):
            step(qs, pl.multiple_of(j * (2 * TQ), 2 * TQ), 2 * TQ, False)
            return carry

        lax.fori_loop(0, qi // 2, pair, 0, unroll=True)
        if qi % 2:
            step(qs, (qi - 1) * TQ, TQ, False)
        step(qs, qi * TQ, TQ, True)

        a = acc_ref[...] / l_ref[...]
        o = a[0:TQ, :] - lam * a[TQ:2 * TQ, :]
        o = o * lax.rsqrt(jnp.mean(o * o, axis=-1, keepdims=True) + LN_EPS)
        o = o * sw_ref[...] * (1.0 - lam_init)
        o_ref[qi * TQ:(qi + 1) * TQ, :] = o.astype(BF16)


def _diff_attn(qkv, lq1, lk1, lq2, lk2, subln_w, lam_init, batch, seq):
    n = qkv.shape[0]
    qb0 = OFF_QDA // LANES
    kb0 = OFF_KDA // LANES
    vb0 = OFF_VDA // LANES
    vec = lambda a: a.reshape(1, -1).astype(F32)
    small = lambda w: pl.BlockSpec((1, w), lambda b, h: (0, 0))
    return pl.pallas_call(
        functools.partial(_diff_attn_kernel, lam_init),
        grid=(batch, DA_HEADS),
        in_specs=[pl.BlockSpec((seq, LANES), lambda b, h: (b, qb0 + h)),
                  pl.BlockSpec((seq, LANES), lambda b, h: (b, kb0 + h)),
                  pl.BlockSpec((seq, LANES), lambda b, h: (b, vb0 + h)),
                  small(64), small(64), small(64), small(64), small(128)],
        out_specs=pl.BlockSpec((seq, LANES), lambda b, h: (b, h)),
        out_shape=jax.ShapeDtypeStruct((n, DA_WIDTH), BF16),
        scratch_shapes=[pltpu.VMEM((2 * TQ, LANES), F32), pltpu.VMEM((2 * TQ, LANES), F32),
                        pltpu.VMEM((2 * TQ, LANES), F32)],
        compiler_params=_cparams(("arbitrary", "arbitrary")),
        name="diff_attn",
    )(qkv, qkv, qkv, vec(lq1), vec(lk1), vec(lq2), vec(lk2), vec(subln_w))


def _retention_kernel(q_ref, k_ref, v_ref, g_ref, lg_ref, o_ref, state_ref):
    C = RET_CHUNK
    lg = lg_ref[0, 0:1, :]
    lg_col = lg[:, 0:1]
    n_i = lax.broadcasted_iota(jnp.int32, (C, C), 0)
    m_i = lax.broadcasted_iota(jnp.int32, (C, C), 1)
    rel = (n_i - m_i).astype(F32)
    decay = jnp.where(rel >= 0, jnp.exp(rel * lg_col), 0.0)
    n_col = lax.broadcasted_iota(jnp.int32, (C, 1), 0).astype(F32)
    zeta = jnp.exp((C - 1.0 - n_col) * lg_col)
    xi = jnp.exp((n_col + 1.0) * lg_col)
    chunk_decay = jnp.exp(C * lg_col)
    state_ref[...] = jnp.zeros(state_ref.shape, F32)

    def body(c, carry):
        start = pl.multiple_of(c * C, C)
        q = q_ref[pl.ds(start, C), :]
        k = k_ref[pl.ds(start, C), :]
        v = v_ref[pl.ds(start, C), :]
        g = g_ref[pl.ds(start, C), :].astype(F32)
        qk = lax.dot_general(q, k, (((1,), (1,)), ((), ())), preferred_element_type=F32)
        inner = (qk * decay).astype(BF16)
        inner_o = jnp.dot(inner, v, preferred_element_type=F32)
        state = state_ref[...]
        cross = jnp.dot(q, state.astype(BF16), preferred_element_type=F32)
        y = inner_o + cross * xi
        kz_t = (k.astype(F32) * zeta).T.astype(BF16)
        kv = jnp.dot(kz_t, v, preferred_element_type=F32)
        state_ref[...] = chunk_decay * state + kv
        mu = jnp.mean(y, axis=-1, keepdims=True)
        yc = y - mu
        var = jnp.mean(yc * yc, axis=-1, keepdims=True)
        yn = yc * lax.rsqrt(var + LN_EPS)
        o_ref[pl.ds(start, C), :] = (g * jax.nn.sigmoid(g) * yn).astype(BF16)
        return carry

    lax.fori_loop(0, q_ref.shape[0] // C, body, 0, unroll=8)


def _retention(qkv, batch, seq):
    n = qkv.shape[0]
    lg = np.log(1.0 - np.power(2.0, -5.0 - np.arange(RET_HEADS, dtype=np.float64)))
    lg_tab = np.broadcast_to(lg[:, None, None], (RET_HEADS, SUBLANES, LANES)).astype(np.float32)
    qb0 = OFF_QR // RET_K_DIM
    kb0 = OFF_KR // RET_K_DIM
    vb0 = OFF_VR // RET_V_DIM
    gb0 = OFF_GR // RET_V_DIM
    return pl.pallas_call(
        _retention_kernel,
        grid=(batch, RET_HEADS),
        in_specs=[pl.BlockSpec((seq, RET_K_DIM), lambda b, h: (b, qb0 + h)),
                  pl.BlockSpec((seq, RET_K_DIM), lambda b, h: (b, kb0 + h)),
                  pl.BlockSpec((seq, RET_V_DIM), lambda b, h: (b, vb0 + h)),
                  pl.BlockSpec((seq, RET_V_DIM), lambda b, h: (b, gb0 + h)),
                  pl.BlockSpec((1, SUBLANES, LANES), lambda b, h: (h, 0, 0))],
        out_specs=pl.BlockSpec((seq, RET_V_DIM), lambda b, h: (b, h)),
        out_shape=jax.ShapeDtypeStruct((n, RET_WIDTH), BF16),
        scratch_shapes=[pltpu.VMEM((RET_K_DIM, RET_V_DIM), F32)],
        compiler_params=_cparams(("arbitrary", "arbitrary")),
        name="retention",
    )(qkv, qkv, qkv, qkv, jnp.asarray(lg_tab))


def _layer_norm(z, g, b):
    mu = jnp.mean(z, axis=-1, keepdims=True)
    zc = z - mu
    var = jnp.mean(zc * zc, axis=-1, keepdims=True)
    return zc * lax.rsqrt(var + LN_EPS) * g + b


def _split_bf16(a):
    hi = a.astype(BF16)
    lo = (a - hi.astype(F32)).astype(BF16)
    return hi, lo


def _out_proj_kernel(oda_ref, or_ref, x_ref, w_ref, g_ref, b_ref, wr_ref, br_ref,
                     x1_ref, route_ref, cnt_ref, run_ref):
    i = pl.program_id(0)

    @pl.when(i == 0)
    def _():
        run_ref[...] = jnp.zeros(run_ref.shape, F32)

    tm = x_ref.shape[0]
    run = run_ref[...]
    h = jnp.dot(oda_ref[...], w_ref[0:DA_WIDTH, :], preferred_element_type=F32)
    h = h + jnp.dot(or_ref[...], w_ref[DA_WIDTH:DA_WIDTH + RET_WIDTH, :], preferred_element_type=F32)
    x1 = _layer_norm(DEEPNORM_ALPHA * x_ref[...] + h, g_ref[...], b_ref[...])
    x1_ref[...] = x1

    xh, xl = _split_bf16(x1)
    wh, wl = _split_bf16(wr_ref[...])
    hh_hl = jnp.dot(xh, jnp.concatenate([wh, wl], axis=1), preferred_element_type=F32)
    logits = (hh_hl[:, 0:LANES] + hh_hl[:, LANES:2 * LANES]
              + jnp.dot(xl, wh, preferred_element_type=F32)) + br_ref[...]
    lane = lax.broadcasted_iota(jnp.int32, logits.shape, 1).astype(F32)
    neg = jnp.float32(-jnp.inf)
    big = jnp.float32(1 << 20)
    is_g = lane < N_GROUPS
    gl = jnp.where(is_g, logits, neg)
    gmax = jnp.max(gl, axis=-1, keepdims=True)
    g_idx = jnp.min(jnp.where(is_g & (gl == gmax), lane, big), axis=-1, keepdims=True)
    g_w = 1.0 / jnp.sum(jnp.where(is_g, jnp.exp(gl - gmax), 0.0), axis=-1, keepdims=True)
    e_lane = lane - N_GROUPS
    in_grp = (e_lane >= g_idx * EXPERTS_PER_GROUP) & (e_lane < (g_idx + 1.0) * EXPERTS_PER_GROUP)
    el = jnp.where(in_grp, logits, neg)
    v1 = jnp.max(el, axis=-1, keepdims=True)
    l1 = jnp.min(jnp.where(in_grp & (el == v1), lane, big), axis=-1, keepdims=True)
    el2 = jnp.where(lane == l1, neg, el)
    v2 = jnp.max(el2, axis=-1, keepdims=True)
    l2 = jnp.min(jnp.where(in_grp & (lane != l1) & (el2 == v2), lane, big), axis=-1, keepdims=True)
    t = jnp.exp(v2 - v1)
    fw1 = g_w / (1.0 + t)
    fw2 = g_w * t / (1.0 + t)
    e1 = l1 - N_GROUPS
    e2 = l2 - N_GROUPS

    oh1 = (lane == e1).astype(F32)
    oh2 = (lane == e2).astype(F32)
    cnt = oh1 + oh2
    r_i = lax.broadcasted_iota(jnp.int32, (tm, tm), 0)
    c_i = lax.broadcasted_iota(jnp.int32, (tm, tm), 1)
    tri = (c_i < r_i).astype(BF16)
    before = jnp.dot(tri, cnt.astype(BF16), preferred_element_type=F32) + run
    rank1 = jnp.sum(oh1 * before, axis=-1, keepdims=True)
    rank2 = jnp.sum(oh2 * before, axis=-1, keepdims=True)

    route = jnp.where(lane == 0, e1, 0.0)
    route = jnp.where(lane == 1, e2, route)
    route = jnp.where(lane == 2, rank1, route)
    route = jnp.where(lane == 3, rank2, route)
    route = jnp.where(lane == 4, fw1, route)
    route = jnp.where(lane == 5, fw2, route)
    route_ref[...] = route
    run = run + jnp.sum(cnt, axis=0, keepdims=True)
    run_ref[...] = run
    cnt_ref[...] = jnp.broadcast_to(run, cnt_ref.shape)


def _out_proj(oda, o_r, x2, w_out_b, ln_g, ln_b, w_router, b_router):
    n = x2.shape[0]
    tm = TM_OUT
    row = lambda w: pl.BlockSpec((1, w), lambda i: (0, 0))
    return pl.pallas_call(
        _out_proj_kernel,
        grid=(n // tm,),
        in_specs=[pl.BlockSpec((tm, DA_WIDTH), lambda i: (i, 0)),
                  pl.BlockSpec((tm, RET_WIDTH), lambda i: (i, 0)),
                  pl.BlockSpec((tm, D_MODEL), lambda i: (i, 0)),
                  pl.BlockSpec((D_MODEL, D_MODEL), lambda i: (0, 0), pipeline_mode=pl.Buffered(1)),
                  row(D_MODEL), row(D_MODEL),
                  pl.BlockSpec((D_MODEL, LANES), lambda i: (0, 0)),
                  row(LANES)],
        out_specs=[pl.BlockSpec((tm, D_MODEL), lambda i: (i, 0)),
                   pl.BlockSpec((tm, LANES), lambda i: (i, 0)),
                   pl.BlockSpec((SUBLANES, LANES), lambda i: (0, 0))],
        out_shape=[jax.ShapeDtypeStruct((n, D_MODEL), F32),
                   jax.ShapeDtypeStruct((n, LANES), F32),
                   jax.ShapeDtypeStruct((SUBLANES, LANES), F32)],
        scratch_shapes=[pltpu.VMEM((1, LANES), F32)],
        compiler_params=_cparams(("arbitrary",), VMEM_LIMIT),
        name="out_proj_ln1_router",
    )(oda, o_r, x2, w_out_b, ln_g, ln_b, w_router, b_router)


def _slot_kernel(route_ref, cnt_ref, pos_ref):
    route = route_ref[...]
    lane = lax.broadcasted_iota(jnp.int32, (1, LANES), 1)
    cnt = jnp.where(lane < N_EXPERTS, cnt_ref[0:1, :], 0.0)
    padded = jnp.floor((cnt + (TM_EXP - 1)) * (1.0 / TM_EXP)) * TM_EXP
    ends = padded
    for sh in (1, 2, 4, 8, 16):
        ends = ends + jnp.where(lane >= sh, pltpu.roll(ends, sh, 1), 0.0)
    offs = ends - padded
    lane_f = lax.broadcasted_iota(jnp.int32, route.shape, 1).astype(F32)
    off1 = jnp.sum(jnp.where(lane_f == route[:, 0:1], offs, 0.0), axis=-1, keepdims=True)
    off2 = jnp.sum(jnp.where(lane_f == route[:, 1:2], offs, 0.0), axis=-1, keepdims=True)
    slots = jnp.where(lane_f == 0.0, off1 + route[:, 2:3], 0.0)
    slots = jnp.where(lane_f == 1.0, off2 + route[:, 3:4], slots)
    pos_ref[...] = slots.T[0:SUBLANES, :].astype(jnp.int32)


def _slots(route, counts):
    n = route.shape[0]
    tm = 1024
    return pl.pallas_call(
        _slot_kernel,
        grid=(n // tm,),
        in_specs=[pl.BlockSpec((tm, LANES), lambda i: (i, 0)),
                  pl.BlockSpec((SUBLANES, LANES), lambda i: (0, 0))],
        out_specs=pl.BlockSpec((SUBLANES, tm), lambda i: (0, i)),
        out_shape=jax.ShapeDtypeStruct((SUBLANES, n), jnp.int32),
        compiler_params=_cparams(("arbitrary",)),
        name="moe_slots",
    )(route, counts)


def _inverse_kernel(n_tok, pos1_ref, pos2_ref, inv_ref):
    i = pl.program_id(0)
    tm = pos1_ref.shape[0]
    n_slots = inv_ref.shape[0]

    @pl.when(i == 0)
    def _():
        for base in range(0, n_slots, n_tok):
            def fill(j, carry, base=base):
                inv_ref[base + j] = j
                return carry
            lax.fori_loop(0, min(n_tok, n_slots - base), fill, 0, unroll=8)

    def scatter(r, carry):
        tok = i * tm + r
        inv_ref[pos1_ref[r]] = tok
        inv_ref[pos2_ref[r]] = tok
        return carry

    lax.fori_loop(0, tm, scatter, 0, unroll=8)


def _inverse_map(pos1, pos2, p_rows):
    n = pos1.shape[0]
    tm = 1024
    smem = pl.BlockSpec((tm,), lambda i: (i,), memory_space=pltpu.SMEM)
    return pl.pallas_call(
        functools.partial(_inverse_kernel, n),
        grid=(n // tm,),
        in_specs=[smem, smem],
        out_specs=pl.BlockSpec(memory_space=pltpu.SMEM),
        out_shape=jax.ShapeDtypeStruct((p_rows,), jnp.int32),
        compiler_params=_cparams(("arbitrary",)),
        name="moe_inverse_map",
    )(pos1, pos2)


def _expert_ffn_kernel(t_max, tb_ref, tv_ref, inv_ref, x1_ref, wg_ref, wu_ref, wd_ref, ys_ref,
                       xbuf_ref, ybuf_ref, zbuf_ref, wgb_ref, wub_ref, wdb_ref, gsem, osem, zsem):
    e = pl.program_id(0)
    tm = TM_EXP
    g_first = tb_ref[e]
    g_end = tb_ref[e + 1]
    n_total = tb_ref[N_EXPERTS]

    def gathered_rows(g):
        return ((tv_ref[g] + (GATHER_GROUP - 1)) // GATHER_GROUP) * GATHER_GROUP

    def gather_rows(g, buf_slot):
        def issue(q, carry):
            for k in range(GATHER_GROUP):
                r = q * GATHER_GROUP + k
                pltpu.make_async_copy(x1_ref.at[pl.ds(inv_ref[g * tm + r], 1), :],
                                      xbuf_ref.at[buf_slot, pl.ds(r, 1), :], gsem.at[buf_slot]).start()
            return carry
        lax.fori_loop(0, gathered_rows(g) // GATHER_GROUP, issue, 0)

    def wait_rows(g, buf_slot):
        rows = gathered_rows(g)
        k = GATHER_GROUP
        while k <= tm:
            @pl.when((rows & k) != 0)
            def _(k=k):
                pltpu.make_async_copy(x1_ref.at[pl.ds(0, k), :], xbuf_ref.at[buf_slot, pl.ds(0, k), :],
                                      gsem.at[buf_slot]).wait()
            k *= 2

    def tile_out(g):
        return ys_ref.at[pl.ds(pl.multiple_of(g * tm, tm), tm), :]

    @pl.when(e == 0)
    def _():
        xbuf_ref[...] = jnp.zeros(xbuf_ref.shape, F32)
        gather_rows(0, 0)

        @pl.when(n_total > 1)
        def _():
            gather_rows(1, 1)
        zbuf_ref[...] = jnp.zeros(zbuf_ref.shape, F32)

        def fill(g, carry):
            pltpu.make_async_copy(zbuf_ref, tile_out(g), zsem).start()
            return carry
        lax.fori_loop(n_total, t_max, fill, 0)

    @pl.when(g_end > g_first)
    def _():
        wgb_ref[...] = wg_ref[0].astype(BF16)
        wub_ref[...] = wu_ref[0].astype(BF16)
        wdb_ref[...] = wd_ref[0].astype(BF16)

    def tile(g, carry):
        slot = g % 2
        xslot = g % GATHER_SLOTS

        @pl.when(g + (GATHER_SLOTS - 1) < n_total)
        def _():
            gather_rows(g + (GATHER_SLOTS - 1), (g + (GATHER_SLOTS - 1)) % GATHER_SLOTS)

        wait_rows(g, xslot)
        x = xbuf_ref[xslot].astype(BF16)
        gate = jnp.dot(x, wgb_ref[...], preferred_element_type=F32)
        up = jnp.dot(x, wub_ref[...], preferred_element_type=F32)
        hmid = (gate * jax.nn.sigmoid(gate) * up).astype(BF16)
        y = jnp.dot(hmid, wdb_ref[...], preferred_element_type=F32)

        @pl.when(g >= 2)
        def _():
            pltpu.make_async_copy(ybuf_ref.at[slot], tile_out(g), osem.at[slot]).wait()

        ybuf_ref[slot] = y
        pltpu.make_async_copy(ybuf_ref.at[slot], tile_out(g), osem.at[slot]).start()
        return carry

    lax.fori_loop(g_first, g_end, tile, 0)

    @pl.when(e == pl.num_programs(0) - 1)
    def _():
        pltpu.make_async_copy(ybuf_ref.at[0], tile_out(0), osem.at[0]).wait()
        pltpu.make_async_copy(ybuf_ref.at[1], tile_out(0), osem.at[1]).wait()

        def drain(g, carry):
            pltpu.make_async_copy(zbuf_ref, tile_out(g), zsem).wait()
            return carry
        lax.fori_loop(n_total, t_max, drain, 0)


def _expert_ffn(tile_base, tile_valid, inv, x1, w_gate, w_up, w_down, t_max):
    tm = TM_EXP
    assert 2 * x1.shape[0] >= 2 * tm
    grid_spec = pltpu.PrefetchScalarGridSpec(
        num_scalar_prefetch=3,
        grid=(N_EXPERTS,),
        in_specs=[pl.BlockSpec(memory_space=pl.ANY),
                  pl.BlockSpec((1, D_MODEL, EXPERT_FF), lambda e, tb, tv, inv: (e, 0, 0)),
                  pl.BlockSpec((1, D_MODEL, EXPERT_FF), lambda e, tb, tv, inv: (e, 0, 0)),
                  pl.BlockSpec((1, EXPERT_FF, D_MODEL), lambda e, tb, tv, inv: (e, 0, 0))],
        out_specs=pl.BlockSpec(memory_space=pl.ANY),
        scratch_shapes=[pltpu.VMEM((GATHER_SLOTS, tm, D_MODEL), F32),
                        pltpu.VMEM((2, tm, D_MODEL), F32),
                        pltpu.VMEM((tm, D_MODEL), F32),
                        pltpu.VMEM((D_MODEL, EXPERT_FF), BF16),
                        pltpu.VMEM((D_MODEL, EXPERT_FF), BF16),
                        pltpu.VMEM((EXPERT_FF, D_MODEL), BF16),
                        pltpu.SemaphoreType.DMA((GATHER_SLOTS,)),
                        pltpu.SemaphoreType.DMA((2,)),
                        pltpu.SemaphoreType.DMA],
    )
    return pl.pallas_call(
        functools.partial(_expert_ffn_kernel, t_max),
        grid_spec=grid_spec,
        out_shape=jax.ShapeDtypeStruct((t_max * tm, D_MODEL), F32),
        compiler_params=_cparams(("arbitrary",), VMEM_LIMIT),
        name="moe_expert_ffn",
    )(tile_base, tile_valid, inv, x1, w_gate, w_up, w_down)


def _combine_kernel(pos1_ref, pos2_ref, pos1n_ref, pos2n_ref, x1_ref, route_ref, ys_ref, p_ref, wpg_ref,
                    wpp_ref, g_ref, b_ref, o_ref, y1_ref, y2_ref, sem):
    i = pl.program_id(0)
    tm = route_ref.shape[0]
    slot = i % 2

    def gather_rows(p1_ref, p2_ref, buf_slot):
        def issue(r, carry):
            pltpu.make_async_copy(ys_ref.at[pl.ds(p1_ref[r], 1), :],
                                  y1_ref.at[buf_slot, pl.ds(r, 1), :], sem.at[buf_slot]).start()
            pltpu.make_async_copy(ys_ref.at[pl.ds(p2_ref[r], 1), :],
                                  y2_ref.at[buf_slot, pl.ds(r, 1), :], sem.at[buf_slot]).start()
            return carry
        lax.fori_loop(0, tm, issue, 0, unroll=8)

    @pl.when(i == 0)
    def _():
        gather_rows(pos1_ref, pos2_ref, 0)

    @pl.when(i + 1 < pl.num_programs(0))
    def _():
        gather_rows(pos1n_ref, pos2n_ref, 1 - slot)

    pltpu.make_async_copy(ys_ref.at[pl.ds(0, tm), :], y1_ref.at[slot], sem.at[slot]).wait()
    pltpu.make_async_copy(ys_ref.at[pl.ds(0, tm), :], y2_ref.at[slot], sem.at[slot]).wait()

    for s in range(tm // SUB_COMB):
        rows = pl.ds(s * SUB_COMB, SUB_COMB)
        route = route_ref[rows, :]
        m = route[:, 4:5] * y1_ref[slot, rows, :] + route[:, 5:6] * y2_ref[slot, rows, :]
        x2 = _layer_norm(DEEPNORM_ALPHA * x1_ref[rows, :] + m, g_ref[...], b_ref[...])
        gate = jax.nn.sigmoid(jnp.dot(x2.astype(BF16), wpg_ref[...], preferred_element_type=F32))
        proj = jnp.dot(p_ref[rows, :].astype(BF16), wpp_ref[...], preferred_element_type=F32)
        o_ref[rows, :] = x2 + gate * proj


def _combine(pos1, pos2, x1, route, ys, p2, w_ple_gate_b, w_ple_proj_b, ln_g, ln_b):
    n = route.shape[0]
    tm = TM_ROW
    smem = pl.BlockSpec((tm,), lambda i: (i,), memory_space=pltpu.SMEM)
    smem_next = pl.BlockSpec((tm,), lambda i: (jnp.minimum(i + 1, n // tm - 1),), memory_space=pltpu.SMEM)
    row = lambda w: pl.BlockSpec((1, w), lambda i: (0, 0))
    return pl.pallas_call(
        _combine_kernel,
        grid=(n // tm,),
        in_specs=[smem, smem, smem_next, smem_next,
                  pl.BlockSpec((tm, D_MODEL), lambda i: (i, 0)),
                  pl.BlockSpec((tm, LANES), lambda i: (i, 0)),
                  pl.BlockSpec(memory_space=pl.ANY),
                  pl.BlockSpec((tm, PLE_DIM), lambda i: (i, 0)),
                  pl.BlockSpec((D_MODEL, D_MODEL), lambda i: (0, 0), pipeline_mode=pl.Buffered(1)),
                  pl.BlockSpec((PLE_DIM, D_MODEL), lambda i: (0, 0)),
                  row(D_MODEL), row(D_MODEL)],
        out_specs=pl.BlockSpec((tm, D_MODEL), lambda i: (i, 0)),
        out_shape=jax.ShapeDtypeStruct((n, D_MODEL), F32),
        scratch_shapes=[pltpu.VMEM((2, tm, D_MODEL), F32), pltpu.VMEM((2, tm, D_MODEL), F32),
                        pltpu.SemaphoreType.DMA((2,))],
        compiler_params=_cparams(("arbitrary",), VMEM_LIMIT),
        name="moe_combine_ln2_ple",
    )(pos1, pos2, pos1, pos2, x1, route, ys, p2, w_ple_gate_b, w_ple_proj_b, ln_g, ln_b)


def _tile_plan(counts, t_max):
    tm = TM_EXP
    cnt = counts[0, :N_EXPERTS].astype(jnp.int32)
    tiles = (cnt + tm - 1) // tm
    tile_base = jnp.concatenate([jnp.zeros((1,), jnp.int32), jnp.cumsum(tiles).astype(jnp.int32)])
    g = jnp.arange(t_max, dtype=jnp.int32)
    in_e = (g[:, None] >= tile_base[None, :-1]) & (g[:, None] < tile_base[None, 1:])
    rows = cnt[None, :] - (g[:, None] - tile_base[None, :-1]) * tm
    tile_valid = jnp.sum(jnp.where(in_e, jnp.clip(rows, 0, tm), 0), axis=1).astype(jnp.int32)
    return tile_base, tile_valid


def kernel(x, p, positions, w_in, w_out, da_lambda_q1, da_lambda_k1, da_lambda_q2, da_lambda_k2, da_subln_w, ln1_g, ln1_b, w_router_group, b_router_group, w_router_expert, b_router_expert, w_exp_gate, w_exp_up, w_exp_down, ln2_g, ln2_b, w_ple_gate, w_ple_proj):
    batch, seq, d = x.shape
    n = batch * seq
    assert d == D_MODEL and w_in.shape[0] == DEPTH == 1
    l = 0
    lam_init = 0.8 - 0.6 * math.exp(-0.3 * l)
    x2 = x.reshape(n, d)

    tab = _rope_tables(positions)
    qkv = _in_proj(x2, w_in[l].astype(BF16), tab)
    oda = _diff_attn(qkv, da_lambda_q1[l], da_lambda_k1[l], da_lambda_q2[l], da_lambda_k2[l],
                     da_subln_w[l], lam_init, batch, seq)
    o_r = _retention(qkv, batch, seq)

    pad = LANES - N_GROUPS - N_EXPERTS
    w_router = jnp.concatenate([w_router_group[l], w_router_expert[l], jnp.zeros((d, pad), F32)], axis=1)
    b_router = jnp.concatenate([b_router_group[l], b_router_expert[l], jnp.zeros((pad,), F32)]).reshape(1, LANES)
    x1, route, counts = _out_proj(oda, o_r, x2, w_out[l].astype(BF16),
                                  ln1_g[l].reshape(1, d), ln1_b[l].reshape(1, d), w_router, b_router)

    t_max = (2 * n) // TM_EXP + N_EXPERTS
    tile_base, tile_valid = _tile_plan(counts, t_max)
    pos = _slots(route, counts)
    pos1, pos2 = pos[0], pos[1]
    inv = _inverse_map(pos1, pos2, t_max * TM_EXP)
    ys = _expert_ffn(tile_base, tile_valid, inv, x1, w_exp_gate[l], w_exp_up[l], w_exp_down[l], t_max)
    out = _combine(pos1, pos2, x1, route, ys, p[l].reshape(n, PLE_DIM),
                   w_ple_gate[l].astype(BF16), w_ple_proj[l].astype(BF16),
                   ln2_g[l].reshape(1, d), ln2_b[l].reshape(1, d))
    return out.reshape(batch, seq, d)
```

```python
import functools
import math

import numpy as np
import jax
import jax.numpy as jnp
from jax import lax
from jax.experimental import pallas as pl
from jax.experimental.pallas import tpu as pltpu

D_MODEL = 2048
DA_HEADS = 8
DA_HEAD_DIM = 64
DA_V_DIM = 128
RET_HEADS = 4
RET_K_DIM = 128
RET_V_DIM = 256
RET_CHUNK = 128
ROPE_THETA = 10000.0
PLE_DIM = 256
N_GROUPS = 4
EXPERTS_PER_GROUP = 8
N_EXPERTS = 32
EXPERT_FF = 512
DEPTH = 1
DEEPNORM_ALPHA = (2 * DEPTH) ** 0.25
LN_EPS = 1e-5
IN_WIDTH = 6144
DA_WIDTH = 1024
RET_WIDTH = 1024

OFF_QDA, OFF_KDA, OFF_VDA, OFF_QR, OFF_KR, OFF_VR, OFF_GR = 0, 1024, 2048, 3072, 3584, 4096, 5120

LANES = 128
SUBLANES = 8
VMEM_LIMIT = 56 * 1024 * 1024

BF16 = jnp.bfloat16
F32 = jnp.float32

TM_PROJ = 512
TN_PROJ = 512
TQ = 256
TM_OUT = 512
SUB_COMB = 256
TM_ROW = 512
TM_EXP = 256
GATHER_GROUP = 8
GATHER_SLOTS = 3


def _cparams(sem, vmem=None):
    return pltpu.CompilerParams(dimension_semantics=sem, vmem_limit_bytes=vmem)


def _rope_tables(pos, inv, sgn):
    ang = pos.astype(F32) * inv
    c = jnp.cos(ang)
    s = jnp.sin(ang)
    low = lax.broadcasted_iota(jnp.int32, ang.shape, 1) < 64
    cr = pltpu.roll(c, 64, 1)
    sr = pltpu.roll(s, 64, 1)
    cos64, sin64 = jnp.where(low, cr, c), jnp.where(low, sr, s) * sgn[0:1, :]
    cos128, sin128 = jnp.where(low, c, cr), jnp.where(low, s, sr) * sgn[1:2, :]
    return cos64, sin64, cos128, sin128


def _rope_constants():
    i128 = np.power(ROPE_THETA, -np.arange(0, 128, 2, dtype=np.float64) / 128)
    i64 = np.power(ROPE_THETA, -np.arange(0, 64, 2, dtype=np.float64) / 64)
    inv = np.concatenate([i128, i64, i64]).astype(np.float32)[None, :]
    lane = np.arange(128)
    sgn = np.stack([np.where(lane % 64 < 32, -1.0, 1.0), np.where(lane < 64, -1.0, 1.0)]).astype(np.float32)
    sgn = np.concatenate([sgn, np.zeros((6, 128), np.float32)])
    return jnp.asarray(inv), jnp.asarray(sgn)


def _rope_cols(acc, cos, sin, half, scale):
    outs = []
    lane = lax.broadcasted_iota(jnp.int32, (acc.shape[0], LANES), 1)
    for c in range(acc.shape[1] // LANES):
        t = acc[:, c * LANES:(c + 1) * LANES]
        if half == 64:
            partner = pltpu.roll(t, 64, 1)
        else:
            partner = jnp.where((lane & 32) == 0, pltpu.roll(t, 96, 1), pltpu.roll(t, 32, 1))
        o = t * cos + partner * sin
        if scale != 1.0:
            o = o * scale
        outs.append(o)
    return jnp.concatenate(outs, axis=1)


def _in_proj_kernel(x_ref, w_ref, pos_ref, inv_ref, sgn_ref, o_ref):
    xb = x_ref[...].astype(BF16)
    cos64, sin64, cos128, sin128 = _rope_tables(pos_ref[...], inv_ref[...], sgn_ref[...])
    n_tiles = IN_WIDTH // TN_PROJ
    plain = [j for j in range(n_tiles) if OFF_VDA <= j * TN_PROJ < OFF_QR or j * TN_PROJ >= OFF_VR]
    for j in plain + [j for j in range(n_tiles) if j not in plain]:
        lo = j * TN_PROJ
        acc = jnp.dot(xb, w_ref[:, lo:lo + TN_PROJ], preferred_element_type=F32)
        if lo < OFF_VDA:
            acc = _rope_cols(acc, cos64, sin64, 32, 1.0)
        elif OFF_QR <= lo < OFF_KR:
            acc = _rope_cols(acc, cos128, sin128, 64, 1.0)
        elif OFF_KR <= lo < OFF_VR:
            acc = _rope_cols(acc, cos128, sin128, 64, RET_K_DIM ** -0.5)
        o_ref[:, lo:lo + TN_PROJ] = acc.astype(BF16)


def _in_proj(x2, w_in_b, positions):
    n = x2.shape[0]
    inv, sgn = _rope_constants()
    return pl.pallas_call(
        _in_proj_kernel,
        grid=(n // TM_PROJ,),
        in_specs=[pl.BlockSpec((TM_PROJ, D_MODEL), lambda i: (i, 0)),
                  pl.BlockSpec((D_MODEL, IN_WIDTH), lambda i: (0, 0), pipeline_mode=pl.Buffered(1)),
                  pl.BlockSpec((TM_PROJ, 1), lambda i: (i, 0)),
                  pl.BlockSpec((1, LANES), lambda i: (0, 0)),
                  pl.BlockSpec((SUBLANES, LANES), lambda i: (0, 0))],
        out_specs=pl.BlockSpec((TM_PROJ, IN_WIDTH), lambda i: (i, 0)),
        out_shape=jax.ShapeDtypeStruct((n, IN_WIDTH), BF16),
        compiler_params=_cparams(("arbitrary",), VMEM_LIMIT),
        name="in_proj",
    )(x2, w_in_b, positions.reshape(n, 1), inv, sgn)


def _diff_attn_kernel(lam_init, q_ref, k_ref, v_ref, lq1_ref, lk1_ref, lq2_ref, lk2_ref, sw_ref,
                      o_ref, m_ref, l_ref, acc_ref):
    seq = q_ref.shape[0]
    lam = (jnp.exp(jnp.sum(lq1_ref[...] * lk1_ref[...], axis=-1, keepdims=True))
           - jnp.exp(jnp.sum(lq2_ref[...] * lk2_ref[...], axis=-1, keepdims=True)) + lam_init)
    scale = jnp.asarray(DA_HEAD_DIM ** -0.5, BF16)
    lane = lax.broadcasted_iota(jnp.int32, (TQ, LANES), 1)
    row_i = lax.broadcasted_iota(jnp.int32, (2 * TQ, TQ), 0)
    col_i = lax.broadcasted_iota(jnp.int32, (2 * TQ, TQ), 1)
    causal = col_i <= jnp.where(row_i >= TQ, row_i - TQ, row_i)

    def step(qs, start, width, masked):
        k = k_ref[pl.ds(start, width), :]
        v = v_ref[pl.ds(start, width), :]
        s = lax.dot_general(qs, k, (((1,), (1,)), ((), ())), preferred_element_type=F32)
        if masked:
            s = jnp.where(causal, s, jnp.finfo(F32).min)
        chunks = [s[:, c * LANES:(c + 1) * LANES] for c in range(width // LANES)]
        m_old = m_ref[...]
        m_new = jnp.maximum(m_old, jnp.max(functools.reduce(jnp.maximum, chunks), axis=-1, keepdims=True))
        alpha = jnp.exp(m_old - m_new)
        p = jnp.concatenate([jnp.exp(c - m_new).astype(BF16) for c in chunks], axis=1)
        v_ext = jnp.concatenate([v, jnp.ones_like(v)], axis=1)
        pv = jnp.dot(p, v_ext, preferred_element_type=F32)
        acc_ref[...] = alpha * acc_ref[...] + pv[:, 0:LANES]
        l_ref[...] = alpha * l_ref[...] + pv[:, LANES:2 * LANES]
        m_ref[...] = m_new

    for qi in range(seq // TQ):
        q = q_ref[qi * TQ:(qi + 1) * TQ, :]
        zero = jnp.zeros_like(q)
        qs = jnp.concatenate([jnp.where(lane < 64, q, zero), jnp.where(lane >= 64, q, zero)], axis=0) * scale
        m_ref[...] = jnp.full(m_ref.shape, -jnp.inf, F32)
        l_ref[...] = jnp.zeros(l_ref.shape, F32)
        acc_ref[...] = jnp.zeros(acc_ref.shape, F32)

        def pair(j, carry, qs=qs):
            step(qs, pl.multiple_of(j * (2 * TQ), 2 * TQ), 2 * TQ, False)
            return carry

        lax.fori_loop(0, qi // 2, pair, 0, unroll=True)
        if qi % 2:
            step(qs, (qi - 1) * TQ, TQ, False)
        step(qs, qi * TQ, TQ, True)

        a = acc_ref[...] / l_ref[...]
        o = a[0:TQ, :] - lam * a[TQ:2 * TQ, :]
        o = o * lax.rsqrt(jnp.mean(o * o, axis=-1, keepdims=True) + LN_EPS)
        o = o * sw_ref[...] * (1.0 - lam_init)
        o_ref[qi * TQ:(qi + 1) * TQ, :] = o.astype(BF16)


def _diff_attn(qkv, lq1, lk1, lq2, lk2, subln_w, lam_init, batch, seq):
    n = qkv.shape[0]
    qb0 = OFF_QDA // LANES
    kb0 = OFF_KDA // LANES
    vb0 = OFF_VDA // LANES
    vec = lambda a: a.reshape(1, -1).astype(F32)
    small = lambda w: pl.BlockSpec((1, w), lambda b, h: (0, 0))
    return pl.pallas_call(
        functools.partial(_diff_attn_kernel, lam_init),
        grid=(batch, DA_HEADS),
        in_specs=[pl.BlockSpec((seq, LANES), lambda b, h: (b, qb0 + h)),
                  pl.BlockSpec((seq, LANES), lambda b, h: (b, kb0 + h)),
                  pl.BlockSpec((seq, LANES), lambda b, h: (b, vb0 + h)),
                  small(64), small(64), small(64), small(64), small(128)],
        out_specs=pl.BlockSpec((seq, LANES), lambda b, h: (b, h)),
        out_shape=jax.ShapeDtypeStruct((n, DA_WIDTH), BF16),
        scratch_shapes=[pltpu.VMEM((2 * TQ, LANES), F32), pltpu.VMEM((2 * TQ, LANES), F32),
                        pltpu.VMEM((2 * TQ, LANES), F32)],
        compiler_params=_cparams(("arbitrary", "arbitrary")),
        name="diff_attn",
    )(qkv, qkv, qkv, vec(lq1), vec(lk1), vec(lq2), vec(lk2), vec(subln_w))


def _retention_kernel(q_ref, k_ref, v_ref, g_ref, lg_ref, o_ref, state_ref):
    C = RET_CHUNK
    lg = lg_ref[0, 0:1, :]
    lg_col = lg[:, 0:1]
    n_i = lax.broadcasted_iota(jnp.int32, (C, C), 0)
    m_i = lax.broadcasted_iota(jnp.int32, (C, C), 1)
    rel = (n_i - m_i).astype(F32)
    decay = jnp.where(rel >= 0, jnp.exp(rel * lg_col), 0.0)
    n_col = lax.broadcasted_iota(jnp.int32, (C, 1), 0).astype(F32)
    zeta = jnp.exp((C - 1.0 - n_col) * lg_col)
    xi = jnp.exp((n_col + 1.0) * lg_col)
    chunk_decay = jnp.exp(C * lg_col)
    state_ref[...] = jnp.zeros(state_ref.shape, F32)

    def body(c, carry):
        start = pl.multiple_of(c * C, C)
        q = q_ref[pl.ds(start, C), :]
        k = k_ref[pl.ds(start, C), :]
        v = v_ref[pl.ds(start, C), :]
        g = g_ref[pl.ds(start, C), :].astype(F32)
        qk = lax.dot_general(q, k, (((1,), (1,)), ((), ())), preferred_element_type=F32)
        inner = (qk * decay).astype(BF16)
        inner_o = jnp.dot(inner, v, preferred_element_type=F32)
        state = state_ref[...]
        cross = jnp.dot(q, state.astype(BF16), preferred_element_type=F32)
        y = inner_o + cross * xi
        kz_t = (k.astype(F32) * zeta).T.astype(BF16)
        kv = jnp.dot(kz_t, v, preferred_element_type=F32)
        state_ref[...] = chunk_decay * state + kv
        mu = jnp.mean(y, axis=-1, keepdims=True)
        yc = y - mu
        var = jnp.mean(yc * yc, axis=-1, keepdims=True)
        yn = yc * lax.rsqrt(var + LN_EPS)
        o_ref[pl.ds(start, C), :] = (g * jax.nn.sigmoid(g) * yn).astype(BF16)
        return carry

    lax.fori_loop(0, q_ref.shape[0] // C, body, 0, unroll=8)


def _retention(qkv, batch, seq):
    n = qkv.shape[0]
    lg = np.log(1.0 - np.power(2.0, -5.0 - np.arange(RET_HEADS, dtype=np.float64)))
    lg_tab = np.broadcast_to(lg[:, None, None], (RET_HEADS, SUBLANES, LANES)).astype(np.float32)
    qb0 = OFF_QR // RET_K_DIM
    kb0 = OFF_KR // RET_K_DIM
    vb0 = OFF_VR // RET_V_DIM
    gb0 = OFF_GR // RET_V_DIM
    return pl.pallas_call(
        _retention_kernel,
        grid=(batch, RET_HEADS),
        in_specs=[pl.BlockSpec((seq, RET_K_DIM), lambda b, h: (b, qb0 + h)),
                  pl.BlockSpec((seq, RET_K_DIM), lambda b, h: (b, kb0 + h)),
                  pl.BlockSpec((seq, RET_V_DIM), lambda b, h: (b, vb0 + h)),
                  pl.BlockSpec((seq, RET_V_DIM), lambda b, h: (b, gb0 + h)),
                  pl.BlockSpec((1, SUBLANES, LANES), lambda b, h: (h, 0, 0))],
        out_specs=pl.BlockSpec((seq, RET_V_DIM), lambda b, h: (b, h)),
        out_shape=jax.ShapeDtypeStruct((n, RET_WIDTH), BF16),
        scratch_shapes=[pltpu.VMEM((RET_K_DIM, RET_V_DIM), F32)],
        compiler_params=_cparams(("arbitrary", "arbitrary")),
        name="retention",
    )(qkv, qkv, qkv, qkv, jnp.asarray(lg_tab))


def _layer_norm(z, g, b):
    mu = jnp.mean(z, axis=-1, keepdims=True)
    zc = z - mu
    var = jnp.mean(zc * zc, axis=-1, keepdims=True)
    return zc * lax.rsqrt(var + LN_EPS) * g + b


def _split_bf16(a):
    hi = a.astype(BF16)
    lo = (a - hi.astype(F32)).astype(BF16)
    return hi, lo


def _out_proj_kernel(oda_ref, or_ref, x_ref, w_ref, g_ref, b_ref, wr_ref, br_ref,
                     x1_ref, route_ref, cnt_ref, run_ref):
    i = pl.program_id(0)

    @pl.when(i == 0)
    def _():
        run_ref[...] = jnp.zeros(run_ref.shape, F32)

    tm = x_ref.shape[0]
    run = run_ref[...]
    h = jnp.dot(oda_ref[...], w_ref[0:DA_WIDTH, :], preferred_element_type=F32)
    h = h + jnp.dot(or_ref[...], w_ref[DA_WIDTH:DA_WIDTH + RET_WIDTH, :], preferred_element_type=F32)
    x1 = _layer_norm(DEEPNORM_ALPHA * x_ref[...] + h, g_ref[...], b_ref[...])
    x1_ref[...] = x1

    xh, xl = _split_bf16(x1)
    wh, wl = _split_bf16(wr_ref[...])
    hh_hl = jnp.dot(xh, jnp.concatenate([wh, wl], axis=1), preferred_element_type=F32)
    logits = (hh_hl[:, 0:LANES] + hh_hl[:, LANES:2 * LANES]
              + jnp.dot(xl, wh, preferred_element_type=F32)) + br_ref[...]
    lane = lax.broadcasted_iota(jnp.int32, logits.shape, 1).astype(F32)
    neg = jnp.float32(-jnp.inf)
    big = jnp.float32(1 << 20)
    is_g = lane < N_GROUPS
    gl = jnp.where(is_g, logits, neg)
    gmax = jnp.max(gl, axis=-1, keepdims=True)
    g_idx = jnp.min(jnp.where(is_g & (gl == gmax), lane, big), axis=-1, keepdims=True)
    g_w = 1.0 / jnp.sum(jnp.where(is_g, jnp.exp(gl - gmax), 0.0), axis=-1, keepdims=True)
    e_lane = lane - N_GROUPS
    in_grp = (e_lane >= g_idx * EXPERTS_PER_GROUP) & (e_lane < (g_idx + 1.0) * EXPERTS_PER_GROUP)
    el = jnp.where(in_grp, logits, neg)
    v1 = jnp.max(el, axis=-1, keepdims=True)
    l1 = jnp.min(jnp.where(in_grp & (el == v1), lane, big), axis=-1, keepdims=True)
    el2 = jnp.where(lane == l1, neg, el)
    v2 = jnp.max(el2, axis=-1, keepdims=True)
    l2 = jnp.min(jnp.where(in_grp & (lane != l1) & (el2 == v2), lane, big), axis=-1, keepdims=True)
    t = jnp.exp(v2 - v1)
    fw1 = g_w / (1.0 + t)
    fw2 = g_w * t / (1.0 + t)
    e1 = l1 - N_GROUPS
    e2 = l2 - N_GROUPS

    oh1 = (lane == e1).astype(F32)
    oh2 = (lane == e2).astype(F32)
    cnt = oh1 + oh2
    r_i = lax.broadcasted_iota(jnp.int32, (tm, tm), 0)
    c_i = lax.broadcasted_iota(jnp.int32, (tm, tm), 1)
    tri = (c_i < r_i).astype(BF16)
    before = jnp.dot(tri, cnt.astype(BF16), preferred_element_type=F32) + run
    rank1 = jnp.sum(oh1 * before, axis=-1, keepdims=True)
    rank2 = jnp.sum(oh2 * before, axis=-1, keepdims=True)

    route = jnp.where(lane == 0, e1, 0.0)
    route = jnp.where(lane == 1, e2, route)
    route = jnp.where(lane == 2, rank1, route)
    route = jnp.where(lane == 3, rank2, route)
    route = jnp.where(lane == 4, fw1, route)
    route = jnp.where(lane == 5, fw2, route)
    route_ref[...] = route
    run = run + jnp.sum(cnt, axis=0, keepdims=True)
    run_ref[...] = run
    cnt_ref[...] = jnp.broadcast_to(run, cnt_ref.shape)


def _out_proj(oda, o_r, x2, w_out_b, ln_g, ln_b, w_router, b_router):
    n = x2.shape[0]
    tm = TM_OUT
    row = lambda w: pl.BlockSpec((1, w), lambda i: (0, 0))
    return pl.pallas_call(
        _out_proj_kernel,
        grid=(n // tm,),
        in_specs=[pl.BlockSpec((tm, DA_WIDTH), lambda i: (i, 0)),
                  pl.BlockSpec((tm, RET_WIDTH), lambda i: (i, 0)),
                  pl.BlockSpec((tm, D_MODEL), lambda i: (i, 0)),
                  pl.BlockSpec((D_MODEL, D_MODEL), lambda i: (0, 0), pipeline_mode=pl.Buffered(1)),
                  row(D_MODEL), row(D_MODEL),
                  pl.BlockSpec((D_MODEL, LANES), lambda i: (0, 0)),
                  row(LANES)],
        out_specs=[pl.BlockSpec((tm, D_MODEL), lambda i: (i, 0)),
                   pl.BlockSpec((tm, LANES), lambda i: (i, 0)),
                   pl.BlockSpec((SUBLANES, LANES), lambda i: (0, 0))],
        out_shape=[jax.ShapeDtypeStruct((n, D_MODEL), F32),
                   jax.ShapeDtypeStruct((n, LANES), F32),
                   jax.ShapeDtypeStruct((SUBLANES, LANES), F32)],
        scratch_shapes=[pltpu.VMEM((1, LANES), F32)],
        compiler_params=_cparams(("arbitrary",), VMEM_LIMIT),
        name="out_proj_ln1_router",
    )(oda, o_r, x2, w_out_b, ln_g, ln_b, w_router, b_router)


def _slot_kernel(route_ref, cnt_ref, pos_ref):
    route = route_ref[...]
    lane = lax.broadcasted_iota(jnp.int32, (1, LANES), 1)
    cnt = jnp.where(lane < N_EXPERTS, cnt_ref[0:1, :], 0.0)
    padded = jnp.floor((cnt + (TM_EXP - 1)) * (1.0 / TM_EXP)) * TM_EXP
    ends = padded
    for sh in (1, 2, 4, 8, 16):
        ends = ends + jnp.where(lane >= sh, pltpu.roll(ends, sh, 1), 0.0)
    offs = ends - padded
    lane_f = lax.broadcasted_iota(jnp.int32, route.shape, 1).astype(F32)
    off1 = jnp.sum(jnp.where(lane_f == route[:, 0:1], offs, 0.0), axis=-1, keepdims=True)
    off2 = jnp.sum(jnp.where(lane_f == route[:, 1:2], offs, 0.0), axis=-1, keepdims=True)
    slots = jnp.where(lane_f == 0.0, off1 + route[:, 2:3], 0.0)
    slots = jnp.where(lane_f == 1.0, off2 + route[:, 3:4], slots)
    pos_ref[...] = slots.T[0:SUBLANES, :].astype(jnp.int32)


def _slots(route, counts):
    n = route.shape[0]
    tm = 1024
    return pl.pallas_call(
        _slot_kernel,
        grid=(n // tm,),
        in_specs=[pl.BlockSpec((tm, LANES), lambda i: (i, 0)),
                  pl.BlockSpec((SUBLANES, LANES), lambda i: (0, 0))],
        out_specs=pl.BlockSpec((SUBLANES, tm), lambda i: (0, i)),
        out_shape=jax.ShapeDtypeStruct((SUBLANES, n), jnp.int32),
        compiler_params=_cparams(("arbitrary",)),
        name="moe_slots",
    )(route, counts)


def _inverse_kernel(pos1_ref, pos2_ref, fill_ref, inv_ref, sem):
    i = pl.program_id(0)
    tm = pos1_ref.shape[0]

    @pl.when(i == 0)
    def _():
        fill = pltpu.make_async_copy(fill_ref, inv_ref, sem)
        fill.start()
        fill.wait()

    def scatter(r, carry):
        tok = i * tm + r
        inv_ref[pos1_ref[r]] = tok
        inv_ref[pos2_ref[r]] = tok
        return carry

    lax.fori_loop(0, tm, scatter, 0, unroll=8)


def _inverse_map(pos1, pos2, p_rows):
    n = pos1.shape[0]
    tm = 1024
    smem = pl.BlockSpec((tm,), lambda i: (i,), memory_space=pltpu.SMEM)
    fill = jnp.asarray(np.arange(p_rows, dtype=np.int32) % n)
    return pl.pallas_call(
        _inverse_kernel,
        grid=(n // tm,),
        in_specs=[smem, smem, pl.BlockSpec(memory_space=pl.ANY)],
        out_specs=pl.BlockSpec(memory_space=pltpu.SMEM),
        out_shape=jax.ShapeDtypeStruct((p_rows,), jnp.int32),
        scratch_shapes=[pltpu.SemaphoreType.DMA],
        compiler_params=_cparams(("arbitrary",)),
        name="moe_inverse_map",
    )(pos1, pos2, fill)


def _expert_ffn_kernel(t_max, tb_ref, tv_ref, inv_ref, x1_ref, wg_ref, wu_ref, wd_ref, ys_ref,
                       xbuf_ref, ybuf_ref, zbuf_ref, wgb_ref, wub_ref, wdb_ref, gsem, osem, zsem):
    e = pl.program_id(0)
    tm = TM_EXP
    g_first = tb_ref[e]
    g_end = tb_ref[e + 1]
    n_total = tb_ref[N_EXPERTS]

    def gathered_rows(g):
        return ((tv_ref[g] + (GATHER_GROUP - 1)) // GATHER_GROUP) * GATHER_GROUP

    def gather_rows(g, buf_slot):
        def issue(q, carry):
            for k in range(GATHER_GROUP):
                r = q * GATHER_GROUP + k
                pltpu.make_async_copy(x1_ref.at[pl.ds(inv_ref[g * tm + r], 1), :],
                                      xbuf_ref.at[buf_slot, pl.ds(r, 1), :], gsem.at[buf_slot]).start()
            return carry
        lax.fori_loop(0, gathered_rows(g) // GATHER_GROUP, issue, 0)

    def wait_rows(g, buf_slot):
        rows = gathered_rows(g)
        k = GATHER_GROUP
        while k <= tm:
            @pl.when((rows & k) != 0)
            def _(k=k):
                pltpu.make_async_copy(x1_ref.at[pl.ds(0, k), :], xbuf_ref.at[buf_slot, pl.ds(0, k), :],
                                      gsem.at[buf_slot]).wait()
            k *= 2

    def tile_out(g):
        return ys_ref.at[pl.ds(pl.multiple_of(g * tm, tm), tm), :]

    @pl.when(e == 0)
    def _():
        xbuf_ref[...] = jnp.zeros(xbuf_ref.shape, F32)
        gather_rows(0, 0)

        @pl.when(n_total > 1)
        def _():
            gather_rows(1, 1)
        zbuf_ref[...] = jnp.zeros(zbuf_ref.shape, F32)

        def fill(g, carry):
            pltpu.make_async_copy(zbuf_ref, tile_out(g), zsem).start()
            return carry
        lax.fori_loop(n_total, t_max, fill, 0)

    @pl.when(g_end > g_first)
    def _():
        wgb_ref[...] = wg_ref[0].astype(BF16)
        wub_ref[...] = wu_ref[0].astype(BF16)
        wdb_ref[...] = wd_ref[0].astype(BF16)

    def tile(g, carry):
        slot = g % 2
        xslot = g % GATHER_SLOTS

        @pl.when(g + (GATHER_SLOTS - 1) < n_total)
        def _():
            gather_rows(g + (GATHER_SLOTS - 1), (g + (GATHER_SLOTS - 1)) % GATHER_SLOTS)

        wait_rows(g, xslot)

        @pl.when(g >= 2)
        def _():
            pltpu.make_async_copy(ybuf_ref.at[slot], tile_out(g), osem.at[slot]).wait()

        def ffn(rows):
            x = xbuf_ref[xslot, 0:rows, :].astype(BF16)
            gate = jnp.dot(x, wgb_ref[...], preferred_element_type=F32)
            up = jnp.dot(x, wub_ref[...], preferred_element_type=F32)
            hmid = (gate * jax.nn.sigmoid(gate) * up).astype(BF16)
            ybuf_ref[slot, 0:rows, :] = jnp.dot(hmid, wdb_ref[...], preferred_element_type=F32)

        half = tm // 2

        @pl.when(tv_ref[g] > half)
        def _():
            ffn(tm)

        @pl.when(tv_ref[g] <= half)
        def _():
            ffn(half)
            ybuf_ref[slot, half:tm, :] = jnp.zeros((tm - half, D_MODEL), F32)

        pltpu.make_async_copy(ybuf_ref.at[slot], tile_out(g), osem.at[slot]).start()
        return carry

    lax.fori_loop(g_first, g_end, tile, 0)

    @pl.when(e == pl.num_programs(0) - 1)
    def _():
        pltpu.make_async_copy(ybuf_ref.at[0], tile_out(0), osem.at[0]).wait()
        pltpu.make_async_copy(ybuf_ref.at[1], tile_out(0), osem.at[1]).wait()

        def drain(g, carry):
            pltpu.make_async_copy(zbuf_ref, tile_out(g), zsem).wait()
            return carry
        lax.fori_loop(n_total, t_max, drain, 0)


def _expert_ffn(tile_base, tile_valid, inv, x1, w_gate, w_up, w_down, t_max):
    tm = TM_EXP
    assert 2 * x1.shape[0] >= 2 * tm
    grid_spec = pltpu.PrefetchScalarGridSpec(
        num_scalar_prefetch=3,
        grid=(N_EXPERTS,),
        in_specs=[pl.BlockSpec(memory_space=pl.ANY),
                  pl.BlockSpec((1, D_MODEL, EXPERT_FF), lambda e, tb, tv, inv: (e, 0, 0)),
                  pl.BlockSpec((1, D_MODEL, EXPERT_FF), lambda e, tb, tv, inv: (e, 0, 0)),
                  pl.BlockSpec((1, EXPERT_FF, D_MODEL), lambda e, tb, tv, inv: (e, 0, 0))],
        out_specs=pl.BlockSpec(memory_space=pl.ANY),
        scratch_shapes=[pltpu.VMEM((GATHER_SLOTS, tm, D_MODEL), F32),
                        pltpu.VMEM((2, tm, D_MODEL), F32),
                        pltpu.VMEM((tm, D_MODEL), F32),
                        pltpu.VMEM((D_MODEL, EXPERT_FF), BF16),
                        pltpu.VMEM((D_MODEL, EXPERT_FF), BF16),
                        pltpu.VMEM((EXPERT_FF, D_MODEL), BF16),
                        pltpu.SemaphoreType.DMA((GATHER_SLOTS,)),
                        pltpu.SemaphoreType.DMA((2,)),
                        pltpu.SemaphoreType.DMA],
    )
    return pl.pallas_call(
        functools.partial(_expert_ffn_kernel, t_max),
        grid_spec=grid_spec,
        out_shape=jax.ShapeDtypeStruct((t_max * tm, D_MODEL), F32),
        compiler_params=_cparams(("arbitrary",), VMEM_LIMIT),
        name="moe_expert_ffn",
    )(tile_base, tile_valid, inv, x1, w_gate, w_up, w_down)


def _combine_kernel(pos1_ref, pos2_ref, pos1n_ref, pos2n_ref, x1_ref, route_ref, ys_ref, p_ref, wpg_ref,
                    wpp_ref, g_ref, b_ref, o_ref, y1_ref, y2_ref, sem):
    i = pl.program_id(0)
    tm = route_ref.shape[0]
    slot = i % 2

    def gather_rows(p1_ref, p2_ref, buf_slot):
        def issue(r, carry):
            pltpu.make_async_copy(ys_ref.at[pl.ds(p1_ref[r], 1), :],
                                  y1_ref.at[buf_slot, pl.ds(r, 1), :], sem.at[buf_slot]).start()
            pltpu.make_async_copy(ys_ref.at[pl.ds(p2_ref[r], 1), :],
                                  y2_ref.at[buf_slot, pl.ds(r, 1), :], sem.at[buf_slot]).start()
            return carry
        lax.fori_loop(0, tm, issue, 0, unroll=8)

    @pl.when(i == 0)
    def _():
        gather_rows(pos1_ref, pos2_ref, 0)

    @pl.when(i + 1 < pl.num_programs(0))
    def _():
        gather_rows(pos1n_ref, pos2n_ref, 1 - slot)

    pltpu.make_async_copy(ys_ref.at[pl.ds(0, tm), :], y1_ref.at[slot], sem.at[slot]).wait()
    pltpu.make_async_copy(ys_ref.at[pl.ds(0, tm), :], y2_ref.at[slot], sem.at[slot]).wait()

    for s in range(tm // SUB_COMB):
        rows = pl.ds(s * SUB_COMB, SUB_COMB)
        route = route_ref[rows, :]
        m = route[:, 4:5] * y1_ref[slot, rows, :] + route[:, 5:6] * y2_ref[slot, rows, :]
        x2 = _layer_norm(DEEPNORM_ALPHA * x1_ref[rows, :] + m, g_ref[...], b_ref[...])
        gate = jax.nn.sigmoid(jnp.dot(x2.astype(BF16), wpg_ref[...], preferred_element_type=F32))
        proj = jnp.dot(p_ref[rows, :].astype(BF16), wpp_ref[...], preferred_element_type=F32)
        o_ref[rows, :] = x2 + gate * proj


def _combine(pos1, pos2, x1, route, ys, p2, w_ple_gate_b, w_ple_proj_b, ln_g, ln_b):
    n = route.shape[0]
    tm = TM_ROW
    smem = pl.BlockSpec((tm,), lambda i: (i,), memory_space=pltpu.SMEM)
    smem_next = pl.BlockSpec((tm,), lambda i: (jnp.minimum(i + 1, n // tm - 1),), memory_space=pltpu.SMEM)
    row = lambda w: pl.BlockSpec((1, w), lambda i: (0, 0))
    return pl.pallas_call(
        _combine_kernel,
        grid=(n // tm,),
        in_specs=[smem, smem, smem_next, smem_next,
                  pl.BlockSpec((tm, D_MODEL), lambda i: (i, 0)),
                  pl.BlockSpec((tm, LANES), lambda i: (i, 0)),
                  pl.BlockSpec(memory_space=pl.ANY),
                  pl.BlockSpec((tm, PLE_DIM), lambda i: (i, 0)),
                  pl.BlockSpec((D_MODEL, D_MODEL), lambda i: (0, 0), pipeline_mode=pl.Buffered(1)),
                  pl.BlockSpec((PLE_DIM, D_MODEL), lambda i: (0, 0)),
                  row(D_MODEL), row(D_MODEL)],
        out_specs=pl.BlockSpec((tm, D_MODEL), lambda i: (i, 0)),
        out_shape=jax.ShapeDtypeStruct((n, D_MODEL), F32),
        scratch_shapes=[pltpu.VMEM((2, tm, D_MODEL), F32), pltpu.VMEM((2, tm, D_MODEL), F32),
                        pltpu.SemaphoreType.DMA((2,))],
        compiler_params=_cparams(("arbitrary",), VMEM_LIMIT),
        name="moe_combine_ln2_ple",
    )(pos1, pos2, pos1, pos2, x1, route, ys, p2, w_ple_gate_b, w_ple_proj_b, ln_g, ln_b)


def _tile_plan(counts, t_max):
    tm = TM_EXP
    cnt = counts[0, :N_EXPERTS].astype(jnp.int32)
    tiles = (cnt + tm - 1) // tm
    tile_base = jnp.concatenate([jnp.zeros((1,), jnp.int32), jnp.cumsum(tiles).astype(jnp.int32)])
    g = jnp.arange(t_max, dtype=jnp.int32)
    in_e = (g[:, None] >= tile_base[None, :-1]) & (g[:, None] < tile_base[None, 1:])
    rows = cnt[None, :] - (g[:, None] - tile_base[None, :-1]) * tm
    tile_valid = jnp.sum(jnp.where(in_e, jnp.clip(rows, 0, tm), 0), axis=1).astype(jnp.int32)
    return tile_base, tile_valid


def kernel(x, p, positions, w_in, w_out, da_lambda_q1, da_lambda_k1, da_lambda_q2, da_lambda_k2, da_subln_w, ln1_g, ln1_b, w_router_group, b_router_group, w_router_expert, b_router_expert, w_exp_gate, w_exp_up, w_exp_down, ln2_g, ln2_b, w_ple_gate, w_ple_proj):
    batch, seq, d = x.shape
    n = batch * seq
    assert d == D_MODEL and w_in.shape[0] == DEPTH == 1
    l = 0
    lam_init = 0.8 - 0.6 * math.exp(-0.3 * l)
    x2 = x.reshape(n, d)

    qkv = _in_proj(x2, w_in[l].astype(BF16), positions)
    oda = _diff_attn(qkv, da_lambda_q1[l], da_lambda_k1[l], da_lambda_q2[l], da_lambda_k2[l],
                     da_subln_w[l], lam_init, batch, seq)
    o_r = _retention(qkv, batch, seq)

    pad = LANES - N_GROUPS - N_EXPERTS
    w_router = jnp.concatenate([w_router_group[l], w_router_expert[l], jnp.zeros((d, pad), F32)], axis=1)
    b_router = jnp.concatenate([b_router_group[l], b_router_expert[l], jnp.zeros((pad,), F32)]).reshape(1, LANES)
    x1, route, counts = _out_proj(oda, o_r, x2, w_out[l].astype(BF16),
                                  ln1_g[l].reshape(1, d), ln1_b[l].reshape(1, d), w_router, b_router)

    t_max = (2 * n) // TM_EXP + N_EXPERTS
    tile_base, tile_valid = _tile_plan(counts, t_max)
    pos = _slots(route, counts)
    pos1, pos2 = pos[0], pos[1]
    inv = _inverse_map(pos1, pos2, t_max * TM_EXP)
    ys = _expert_ffn(tile_base, tile_valid, inv, x1, w_exp_gate[l], w_exp_up[l], w_exp_down[l], t_max)
    out = _combine(pos1, pos2, x1, route, ys, p[l].reshape(n, PLE_DIM),
                   w_ple_gate[l].astype(BF16), w_ple_proj[l].astype(BF16),
                   ln2_g[l].reshape(1, d), ln2_b[l].reshape(1, d))
    return out.reshape(batch, seq, d)
```

```python
import functools
import math

import numpy as np
import jax
import jax.numpy as jnp
from jax import lax
from jax.experimental import pallas as pl
from jax.experimental.pallas import tpu as pltpu

D_MODEL = 2048
DA_HEADS = 8
DA_HEAD_DIM = 64
DA_V_DIM = 128
RET_HEADS = 4
RET_K_DIM = 128
RET_V_DIM = 256
RET_CHUNK = 128
ROPE_THETA = 10000.0
PLE_DIM = 256
N_GROUPS = 4
EXPERTS_PER_GROUP = 8
N_EXPERTS = 32
EXPERT_FF = 512
DEPTH = 1
DEEPNORM_ALPHA = (2 * DEPTH) ** 0.25
LN_EPS = 1e-5
IN_WIDTH = 6144
DA_WIDTH = 1024
RET_WIDTH = 1024

OFF_QDA, OFF_KDA, OFF_VDA, OFF_QR, OFF_KR, OFF_VR, OFF_GR = 0, 1024, 2048, 3072, 3584, 4096, 5120

LANES = 128
SUBLANES = 8
VMEM_LIMIT = 56 * 1024 * 1024

BF16 = jnp.bfloat16
F32 = jnp.float32

TM_PROJ = 512
TN_PROJ = 512
TQ = 256
TM_OUT = 512
SUB_COMB = 256
TM_ROW = 512
TM_EXP = 256
GATHER_GROUP = 8
GATHER_SLOTS = 3


def _cparams(sem, vmem=None):
    return pltpu.CompilerParams(dimension_semantics=sem, vmem_limit_bytes=vmem)


def _rope_tables(pos, inv, sgn):
    ang = pos.astype(F32) * inv
    c = jnp.cos(ang)
    s = jnp.sin(ang)
    low = lax.broadcasted_iota(jnp.int32, ang.shape, 1) < 64
    cr = pltpu.roll(c, 64, 1)
    sr = pltpu.roll(s, 64, 1)
    cos64, sin64 = jnp.where(low, cr, c), jnp.where(low, sr, s) * sgn[0:1, :]
    cos128, sin128 = jnp.where(low, c, cr), jnp.where(low, s, sr) * sgn[1:2, :]
    return cos64, sin64, cos128, sin128


def _rope_constants():
    i128 = np.power(ROPE_THETA, -np.arange(0, 128, 2, dtype=np.float64) / 128)
    i64 = np.power(ROPE_THETA, -np.arange(0, 64, 2, dtype=np.float64) / 64)
    inv = np.concatenate([i128, i64, i64]).astype(np.float32)[None, :]
    lane = np.arange(128)
    sgn = np.stack([np.where(lane % 64 < 32, -1.0, 1.0), np.where(lane < 64, -1.0, 1.0)]).astype(np.float32)
    sgn = np.concatenate([sgn, np.zeros((6, 128), np.float32)])
    return jnp.asarray(inv), jnp.asarray(sgn)


def _rope_cols(acc, cos, sin, half, scale):
    outs = []
    lane = lax.broadcasted_iota(jnp.int32, (acc.shape[0], LANES), 1)
    for c in range(acc.shape[1] // LANES):
        t = acc[:, c * LANES:(c + 1) * LANES]
        if half == 64:
            partner = pltpu.roll(t, 64, 1)
        else:
            partner = jnp.where((lane & 32) == 0, pltpu.roll(t, 96, 1), pltpu.roll(t, 32, 1))
        o = t * cos + partner * sin
        if scale != 1.0:
            o = o * scale
        outs.append(o)
    return jnp.concatenate(outs, axis=1)


def _in_proj_kernel(x_ref, w_ref, pos_ref, inv_ref, sgn_ref, o_ref):
    xb = x_ref[...].astype(BF16)
    cos64, sin64, cos128, sin128 = _rope_tables(pos_ref[...], inv_ref[...], sgn_ref[...])
    n_tiles = IN_WIDTH // TN_PROJ
    plain = [j for j in range(n_tiles) if OFF_VDA <= j * TN_PROJ < OFF_QR or j * TN_PROJ >= OFF_VR]
    rotary = [j for j in range(n_tiles) if j not in plain]
    for j in plain[:-2] + rotary + plain[-2:]:
        lo = j * TN_PROJ
        acc = jnp.dot(xb, w_ref[:, lo:lo + TN_PROJ], preferred_element_type=F32)
        if lo < OFF_VDA:
            acc = _rope_cols(acc, cos64, sin64, 32, 1.0)
        elif OFF_QR <= lo < OFF_KR:
            acc = _rope_cols(acc, cos128, sin128, 64, 1.0)
        elif OFF_KR <= lo < OFF_VR:
            acc = _rope_cols(acc, cos128, sin128, 64, RET_K_DIM ** -0.5)
        o_ref[:, lo:lo + TN_PROJ] = acc.astype(BF16)


def _in_proj(x2, w_in_b, positions):
    n = x2.shape[0]
    inv, sgn = _rope_constants()
    return pl.pallas_call(
        _in_proj_kernel,
        grid=(n // TM_PROJ,),
        in_specs=[pl.BlockSpec((TM_PROJ, D_MODEL), lambda i: (i, 0)),
                  pl.BlockSpec((D_MODEL, IN_WIDTH), lambda i: (0, 0), pipeline_mode=pl.Buffered(1)),
                  pl.BlockSpec((TM_PROJ, 1), lambda i: (i, 0)),
                  pl.BlockSpec((1, LANES), lambda i: (0, 0)),
                  pl.BlockSpec((SUBLANES, LANES), lambda i: (0, 0))],
        out_specs=pl.BlockSpec((TM_PROJ, IN_WIDTH), lambda i: (i, 0)),
        out_shape=jax.ShapeDtypeStruct((n, IN_WIDTH), BF16),
        compiler_params=_cparams(("arbitrary",), VMEM_LIMIT),
        name="in_proj",
    )(x2, w_in_b, positions.reshape(n, 1), inv, sgn)


def _diff_attn_kernel(lam_init, q_ref, k_ref, v_ref, lq1_ref, lk1_ref, lq2_ref, lk2_ref, sw_ref,
                      o_ref, m_ref, l_ref, acc_ref):
    seq = q_ref.shape[0]
    lam = (jnp.exp(jnp.sum(lq1_ref[...] * lk1_ref[...], axis=-1, keepdims=True))
           - jnp.exp(jnp.sum(lq2_ref[...] * lk2_ref[...], axis=-1, keepdims=True)) + lam_init)
    scale = jnp.asarray(DA_HEAD_DIM ** -0.5, BF16)
    lane = lax.broadcasted_iota(jnp.int32, (TQ, LANES), 1)
    row_i = lax.broadcasted_iota(jnp.int32, (2 * TQ, TQ), 0)
    col_i = lax.broadcasted_iota(jnp.int32, (2 * TQ, TQ), 1)
    causal = col_i <= jnp.where(row_i >= TQ, row_i - TQ, row_i)

    def step(qs, start, width, masked):
        k = k_ref[pl.ds(start, width), :]
        v = v_ref[pl.ds(start, width), :]
        s = lax.dot_general(qs, k, (((1,), (1,)), ((), ())), preferred_element_type=F32)
        if masked:
            s = jnp.where(causal, s, jnp.finfo(F32).min)
        chunks = [s[:, c * LANES:(c + 1) * LANES] for c in range(width // LANES)]
        m_old = m_ref[...]
        m_new = jnp.maximum(m_old, jnp.max(functools.reduce(jnp.maximum, chunks), axis=-1, keepdims=True))
        alpha = jnp.exp(m_old - m_new)
        p = jnp.concatenate([jnp.exp(c - m_new).astype(BF16) for c in chunks], axis=1)
        v_ext = jnp.concatenate([v, jnp.ones_like(v)], axis=1)
        pv = jnp.dot(p, v_ext, preferred_element_type=F32)
        acc_ref[...] = alpha * acc_ref[...] + pv[:, 0:LANES]
        l_ref[...] = alpha * l_ref[...] + pv[:, LANES:2 * LANES]
        m_ref[...] = m_new

    for qi in range(seq // TQ):
        q = q_ref[qi * TQ:(qi + 1) * TQ, :]
        zero = jnp.zeros_like(q)
        qs = jnp.concatenate([jnp.where(lane < 64, q, zero), jnp.where(lane >= 64, q, zero)], axis=0) * scale
        m_ref[...] = jnp.full(m_ref.shape, -jnp.inf, F32)
        l_ref[...] = jnp.zeros(l_ref.shape, F32)
        acc_ref[...] = jnp.zeros(acc_ref.shape, F32)

        def pair(j, carry, qs=qs):
            step(qs, pl.multiple_of(j * (2 * TQ), 2 * TQ), 2 * TQ, False)
            return carry

        lax.fori_loop(0, qi // 2, pair, 0, unroll=True)
        if qi % 2:
            step(qs, (qi - 1) * TQ, TQ, False)
        step(qs, qi * TQ, TQ, True)

        a = acc_ref[...] / l_ref[...]
        o = a[0:TQ, :] - lam * a[TQ:2 * TQ, :]
        o = o * lax.rsqrt(jnp.mean(o * o, axis=-1, keepdims=True) + LN_EPS)
        o = o * sw_ref[...] * (1.0 - lam_init)
        o_ref[qi * TQ:(qi + 1) * TQ, :] = o.astype(BF16)


def _diff_attn(qkv, lq1, lk1, lq2, lk2, subln_w, lam_init, batch, seq):
    n = qkv.shape[0]
    qb0 = OFF_QDA // LANES
    kb0 = OFF_KDA // LANES
    vb0 = OFF_VDA // LANES
    vec = lambda a: a.reshape(1, -1).astype(F32)
    small = lambda w: pl.BlockSpec((1, w), lambda b, h: (0, 0))
    return pl.pallas_call(
        functools.partial(_diff_attn_kernel, lam_init),
        grid=(batch, DA_HEADS),
        in_specs=[pl.BlockSpec((seq, LANES), lambda b, h: (b, qb0 + h)),
                  pl.BlockSpec((seq, LANES), lambda b, h: (b, kb0 + h)),
                  pl.BlockSpec((seq, LANES), lambda b, h: (b, vb0 + h)),
                  small(64), small(64), small(64), small(64), small(128)],
        out_specs=pl.BlockSpec((seq, LANES), lambda b, h: (b, h)),
        out_shape=jax.ShapeDtypeStruct((n, DA_WIDTH), BF16),
        scratch_shapes=[pltpu.VMEM((2 * TQ, LANES), F32), pltpu.VMEM((2 * TQ, LANES), F32),
                        pltpu.VMEM((2 * TQ, LANES), F32)],
        compiler_params=_cparams(("arbitrary", "arbitrary")),
        name="diff_attn",
    )(qkv, qkv, qkv, vec(lq1), vec(lk1), vec(lq2), vec(lk2), vec(subln_w))


def _retention_kernel(q_ref, k_ref, v_ref, g_ref, lg_ref, o_ref, state_ref):
    C = RET_CHUNK
    lg = lg_ref[0, 0:1, :]
    lg_col = lg[:, 0:1]
    n_i = lax.broadcasted_iota(jnp.int32, (C, C), 0)
    m_i = lax.broadcasted_iota(jnp.int32, (C, C), 1)
    rel = (n_i - m_i).astype(F32)
    decay = jnp.where(rel >= 0, jnp.exp(rel * lg_col), 0.0)
    n_col = lax.broadcasted_iota(jnp.int32, (C, 1), 0).astype(F32)
    zeta = jnp.exp((C - 1.0 - n_col) * lg_col)
    xi = jnp.exp((n_col + 1.0) * lg_col)
    chunk_decay = jnp.exp(C * lg_col)
    state_ref[...] = jnp.zeros(state_ref.shape, F32)

    def body(c, carry):
        start = pl.multiple_of(c * C, C)
        q = q_ref[pl.ds(start, C), :]
        k = k_ref[pl.ds(start, C), :]
        v = v_ref[pl.ds(start, C), :]
        g = g_ref[pl.ds(start, C), :].astype(F32)
        qk = lax.dot_general(q, k, (((1,), (1,)), ((), ())), preferred_element_type=F32)
        inner = (qk * decay).astype(BF16)
        inner_o = jnp.dot(inner, v, preferred_element_type=F32)
        state = state_ref[...]
        cross = jnp.dot(q, state.astype(BF16), preferred_element_type=F32)
        y = inner_o + cross * xi
        kz_t = (k.astype(F32) * zeta).T.astype(BF16)
        kv = jnp.dot(kz_t, v, preferred_element_type=F32)
        state_ref[...] = chunk_decay * state + kv
        mu = jnp.mean(y, axis=-1, keepdims=True)
        yc = y - mu
        var = jnp.mean(yc * yc, axis=-1, keepdims=True)
        yn = yc * lax.rsqrt(var + LN_EPS)
        o_ref[pl.ds(start, C), :] = (g * jax.nn.sigmoid(g) * yn).astype(BF16)
        return carry

    lax.fori_loop(0, q_ref.shape[0] // C, body, 0, unroll=True)


def _retention(qkv, batch, seq):
    n = qkv.shape[0]
    lg = np.log(1.0 - np.power(2.0, -5.0 - np.arange(RET_HEADS, dtype=np.float64)))
    lg_tab = np.broadcast_to(lg[:, None, None], (RET_HEADS, SUBLANES, LANES)).astype(np.float32)
    qb0 = OFF_QR // RET_K_DIM
    kb0 = OFF_KR // RET_K_DIM
    vb0 = OFF_VR // RET_V_DIM
    gb0 = OFF_GR // RET_V_DIM
    return pl.pallas_call(
        _retention_kernel,
        grid=(batch, RET_HEADS),
        in_specs=[pl.BlockSpec((seq, RET_K_DIM), lambda b, h: (b, qb0 + h)),
                  pl.BlockSpec((seq, RET_K_DIM), lambda b, h: (b, kb0 + h)),
                  pl.BlockSpec((seq, RET_V_DIM), lambda b, h: (b, vb0 + h)),
                  pl.BlockSpec((seq, RET_V_DIM), lambda b, h: (b, gb0 + h)),
                  pl.BlockSpec((1, SUBLANES, LANES), lambda b, h: (h, 0, 0))],
        out_specs=pl.BlockSpec((seq, RET_V_DIM), lambda b, h: (b, h)),
        out_shape=jax.ShapeDtypeStruct((n, RET_WIDTH), BF16),
        scratch_shapes=[pltpu.VMEM((RET_K_DIM, RET_V_DIM), F32)],
        compiler_params=_cparams(("arbitrary", "arbitrary")),
        name="retention",
    )(qkv, qkv, qkv, qkv, jnp.asarray(lg_tab))


def _layer_norm(z, g, b):
    mu = jnp.mean(z, axis=-1, keepdims=True)
    zc = z - mu
    var = jnp.mean(zc * zc, axis=-1, keepdims=True)
    return zc * lax.rsqrt(var + LN_EPS) * g + b


def _dot_f32_weight(a, w):
    return lax.dot_general(a, w, (((1,), (0,)), ((), ())), preferred_element_type=F32)


def _split_bf16(a):
    hi = a.astype(BF16)
    lo = (a - hi.astype(F32)).astype(BF16)
    return hi, lo


def _out_proj_kernel(oda_ref, or_ref, x_ref, w_ref, g_ref, b_ref, wr_ref, br_ref,
                     x1_ref, route_ref, cnt_ref, run_ref):
    i = pl.program_id(0)

    @pl.when(i == 0)
    def _():
        run_ref[...] = jnp.zeros(run_ref.shape, F32)

    tm = x_ref.shape[0]
    run = run_ref[...]
    h = _dot_f32_weight(oda_ref[...], w_ref[0:DA_WIDTH, :])
    h = h + _dot_f32_weight(or_ref[...], w_ref[DA_WIDTH:DA_WIDTH + RET_WIDTH, :])
    x1 = _layer_norm(DEEPNORM_ALPHA * x_ref[...] + h, g_ref[...], b_ref[...])
    x1_ref[...] = x1

    xh, xl = _split_bf16(x1)
    wh, wl = _split_bf16(wr_ref[...])
    hh_hl = jnp.dot(xh, jnp.concatenate([wh, wl], axis=1), preferred_element_type=F32)
    logits = (hh_hl[:, 0:LANES] + hh_hl[:, LANES:2 * LANES]
              + jnp.dot(xl, wh, preferred_element_type=F32)) + br_ref[...]
    lane = lax.broadcasted_iota(jnp.int32, logits.shape, 1).astype(F32)
    neg = jnp.float32(-jnp.inf)
    big = jnp.float32(1 << 20)
    is_g = lane < N_GROUPS
    gl = jnp.where(is_g, logits, neg)
    gmax = jnp.max(gl, axis=-1, keepdims=True)
    g_idx = jnp.min(jnp.where(is_g & (gl == gmax), lane, big), axis=-1, keepdims=True)
    g_w = 1.0 / jnp.sum(jnp.where(is_g, jnp.exp(gl - gmax), 0.0), axis=-1, keepdims=True)
    e_lane = lane - N_GROUPS
    in_grp = (e_lane >= g_idx * EXPERTS_PER_GROUP) & (e_lane < (g_idx + 1.0) * EXPERTS_PER_GROUP)
    el = jnp.where(in_grp, logits, neg)
    v1 = jnp.max(el, axis=-1, keepdims=True)
    l1 = jnp.min(jnp.where(in_grp & (el == v1), lane, big), axis=-1, keepdims=True)
    el2 = jnp.where(lane == l1, neg, el)
    v2 = jnp.max(el2, axis=-1, keepdims=True)
    l2 = jnp.min(jnp.where(in_grp & (lane != l1) & (el2 == v2), lane, big), axis=-1, keepdims=True)
    t = jnp.exp(v2 - v1)
    fw1 = g_w / (1.0 + t)
    fw2 = g_w * t / (1.0 + t)
    e1 = l1 - N_GROUPS
    e2 = l2 - N_GROUPS

    oh1 = (lane == e1).astype(F32)
    oh2 = (lane == e2).astype(F32)
    cnt = oh1 + oh2
    r_i = lax.broadcasted_iota(jnp.int32, (tm, tm), 0)
    c_i = lax.broadcasted_iota(jnp.int32, (tm, tm), 1)
    tri = (c_i < r_i).astype(BF16)
    before = jnp.dot(tri, cnt.astype(BF16), preferred_element_type=F32) + run
    rank1 = jnp.sum(oh1 * before, axis=-1, keepdims=True)
    rank2 = jnp.sum(oh2 * before, axis=-1, keepdims=True)

    route = jnp.where(lane == 0, e1, 0.0)
    route = jnp.where(lane == 1, e2, route)
    route = jnp.where(lane == 2, rank1, route)
    route = jnp.where(lane == 3, rank2, route)
    route = jnp.where(lane == 4, fw1, route)
    route = jnp.where(lane == 5, fw2, route)
    route_ref[...] = route
    run = run + jnp.sum(cnt, axis=0, keepdims=True)
    run_ref[...] = run
    cnt_ref[...] = jnp.broadcast_to(run, cnt_ref.shape)


def _out_proj(oda, o_r, x2, w_out_b, ln_g, ln_b, w_router, b_router):
    n = x2.shape[0]
    tm = TM_OUT
    row = lambda w: pl.BlockSpec((1, w), lambda i: (0, 0))
    return pl.pallas_call(
        _out_proj_kernel,
        grid=(n // tm,),
        in_specs=[pl.BlockSpec((tm, DA_WIDTH), lambda i: (i, 0)),
                  pl.BlockSpec((tm, RET_WIDTH), lambda i: (i, 0)),
                  pl.BlockSpec((tm, D_MODEL), lambda i: (i, 0)),
                  pl.BlockSpec((D_MODEL, D_MODEL), lambda i: (0, 0), pipeline_mode=pl.Buffered(1)),
                  row(D_MODEL), row(D_MODEL),
                  pl.BlockSpec((D_MODEL, LANES), lambda i: (0, 0)),
                  row(LANES)],
        out_specs=[pl.BlockSpec((tm, D_MODEL), lambda i: (i, 0)),
                   pl.BlockSpec((tm, LANES), lambda i: (i, 0)),
                   pl.BlockSpec((SUBLANES, LANES), lambda i: (0, 0))],
        out_shape=[jax.ShapeDtypeStruct((n, D_MODEL), F32),
                   jax.ShapeDtypeStruct((n, LANES), F32),
                   jax.ShapeDtypeStruct((SUBLANES, LANES), F32)],
        scratch_shapes=[pltpu.VMEM((1, LANES), F32)],
        compiler_params=_cparams(("arbitrary",), VMEM_LIMIT),
        name="out_proj_ln1_router",
    )(oda, o_r, x2, w_out_b, ln_g, ln_b, w_router, b_router)


def _slot_kernel(route_ref, cnt_ref, pos_ref):
    route = route_ref[...]
    lane = lax.broadcasted_iota(jnp.int32, (1, LANES), 1)
    cnt = jnp.where(lane < N_EXPERTS, cnt_ref[0:1, :], 0.0)
    padded = jnp.floor((cnt + (TM_EXP - 1)) * (1.0 / TM_EXP)) * TM_EXP
    ends = padded
    for sh in (1, 2, 4, 8, 16):
        ends = ends + jnp.where(lane >= sh, pltpu.roll(ends, sh, 1), 0.0)
    offs = ends - padded
    lane_f = lax.broadcasted_iota(jnp.int32, route.shape, 1).astype(F32)
    off1 = jnp.sum(jnp.where(lane_f == route[:, 0:1], offs, 0.0), axis=-1, keepdims=True)
    off2 = jnp.sum(jnp.where(lane_f == route[:, 1:2], offs, 0.0), axis=-1, keepdims=True)
    slots = jnp.where(lane_f == 0.0, off1 + route[:, 2:3], 0.0)
    slots = jnp.where(lane_f == 1.0, off2 + route[:, 3:4], slots)
    pos_ref[...] = slots.T[0:SUBLANES, :].astype(jnp.int32)


def _slots(route, counts):
    n = route.shape[0]
    tm = 1024
    return pl.pallas_call(
        _slot_kernel,
        grid=(n // tm,),
        in_specs=[pl.BlockSpec((tm, LANES), lambda i: (i, 0)),
                  pl.BlockSpec((SUBLANES, LANES), lambda i: (0, 0))],
        out_specs=pl.BlockSpec((SUBLANES, tm), lambda i: (0, i)),
        out_shape=jax.ShapeDtypeStruct((SUBLANES, n), jnp.int32),
        compiler_params=_cparams(("arbitrary",)),
        name="moe_slots",
    )(route, counts)


def _inverse_kernel(pos1_ref, pos2_ref, fill_ref, inv_ref, sem):
    i = pl.program_id(0)
    tm = pos1_ref.shape[0]

    @pl.when(i == 0)
    def _():
        fill = pltpu.make_async_copy(fill_ref, inv_ref, sem)
        fill.start()
        fill.wait()

    def scatter(r, carry):
        tok = i * tm + r
        inv_ref[pos1_ref[r]] = tok
        inv_ref[pos2_ref[r]] = tok
        return carry

    lax.fori_loop(0, tm, scatter, 0, unroll=8)


def _inverse_map(pos1, pos2, p_rows):
    n = pos1.shape[0]
    tm = 1024
    smem = pl.BlockSpec((tm,), lambda i: (i,), memory_space=pltpu.SMEM)
    fill = jnp.asarray(np.arange(p_rows, dtype=np.int32) % n)
    return pl.pallas_call(
        _inverse_kernel,
        grid=(n // tm,),
        in_specs=[smem, smem, pl.BlockSpec(memory_space=pl.ANY)],
        out_specs=pl.BlockSpec(memory_space=pltpu.SMEM),
        out_shape=jax.ShapeDtypeStruct((p_rows,), jnp.int32),
        scratch_shapes=[pltpu.SemaphoreType.DMA],
        compiler_params=_cparams(("arbitrary",)),
        name="moe_inverse_map",
    )(pos1, pos2, fill)


def _expert_ffn_kernel(t_max, tb_ref, tv_ref, inv_ref, x1_ref, wg_ref, wu_ref, wd_ref, ys_ref,
                       xbuf_ref, ybuf_ref, zbuf_ref, gsem, osem, zsem):
    e = pl.program_id(0)
    tm = TM_EXP
    g_first = tb_ref[e]
    g_end = tb_ref[e + 1]
    n_total = tb_ref[N_EXPERTS]

    def gathered_rows(g):
        return ((tv_ref[g] + (GATHER_GROUP - 1)) // GATHER_GROUP) * GATHER_GROUP

    def gather_rows(g, buf_slot):
        def issue(q, carry):
            for k in range(GATHER_GROUP):
                r = q * GATHER_GROUP + k
                pltpu.make_async_copy(x1_ref.at[pl.ds(inv_ref[g * tm + r], 1), :],
                                      xbuf_ref.at[buf_slot, pl.ds(r, 1), :], gsem.at[buf_slot]).start()
            return carry
        lax.fori_loop(0, gathered_rows(g) // GATHER_GROUP, issue, 0)

    def wait_rows(g, buf_slot):
        rows = gathered_rows(g)
        k = GATHER_GROUP
        while k <= tm:
            @pl.when((rows & k) != 0)
            def _(k=k):
                pltpu.make_async_copy(x1_ref.at[pl.ds(0, k), :], xbuf_ref.at[buf_slot, pl.ds(0, k), :],
                                      gsem.at[buf_slot]).wait()
            k *= 2

    def tile_out(g):
        return ys_ref.at[pl.ds(pl.multiple_of(g * tm, tm), tm), :]

    @pl.when(e == 0)
    def _():
        xbuf_ref[...] = jnp.zeros(xbuf_ref.shape, F32)
        gather_rows(0, 0)

        @pl.when(n_total > 1)
        def _():
            gather_rows(1, 1)
        zbuf_ref[...] = jnp.zeros(zbuf_ref.shape, F32)

        def fill(g, carry):
            pltpu.make_async_copy(zbuf_ref, tile_out(g), zsem).start()
            return carry
        lax.fori_loop(n_total, t_max, fill, 0)

    def tile(g, carry):
        slot = g % 2
        xslot = g % GATHER_SLOTS

        @pl.when(g + (GATHER_SLOTS - 1) < n_total)
        def _():
            gather_rows(g + (GATHER_SLOTS - 1), (g + (GATHER_SLOTS - 1)) % GATHER_SLOTS)

        wait_rows(g, xslot)

        @pl.when(g >= 2)
        def _():
            pltpu.make_async_copy(ybuf_ref.at[slot], tile_out(g), osem.at[slot]).wait()

        def ffn(rows):
            x = xbuf_ref[xslot, 0:rows, :].astype(BF16)
            gate = _dot_f32_weight(x, wg_ref[0])
            up = _dot_f32_weight(x, wu_ref[0])
            hmid = (gate * jax.nn.sigmoid(gate) * up).astype(BF16)
            ybuf_ref[slot, 0:rows, :] = _dot_f32_weight(hmid, wd_ref[0])

        half = tm // 2

        @pl.when(tv_ref[g] > half)
        def _():
            ffn(tm)

        @pl.when(tv_ref[g] <= half)
        def _():
            ffn(half)
            ybuf_ref[slot, half:tm, :] = jnp.zeros((tm - half, D_MODEL), F32)

        pltpu.make_async_copy(ybuf_ref.at[slot], tile_out(g), osem.at[slot]).start()
        return carry

    lax.fori_loop(g_first, g_end, tile, 0)

    @pl.when(e == pl.num_programs(0) - 1)
    def _():
        pltpu.make_async_copy(ybuf_ref.at[0], tile_out(0), osem.at[0]).wait()
        pltpu.make_async_copy(ybuf_ref.at[1], tile_out(0), osem.at[1]).wait()

        def drain(g, carry):
            pltpu.make_async_copy(zbuf_ref, tile_out(g), zsem).wait()
            return carry
        lax.fori_loop(n_total, t_max, drain, 0)


def _expert_ffn(tile_base, tile_valid, inv, x1, w_gate, w_up, w_down, t_max):
    tm = TM_EXP
    assert 2 * x1.shape[0] >= 2 * tm
    grid_spec = pltpu.PrefetchScalarGridSpec(
        num_scalar_prefetch=3,
        grid=(N_EXPERTS,),
        in_specs=[pl.BlockSpec(memory_space=pl.ANY),
                  pl.BlockSpec((1, D_MODEL, EXPERT_FF), lambda e, tb, tv, inv: (e, 0, 0)),
                  pl.BlockSpec((1, D_MODEL, EXPERT_FF), lambda e, tb, tv, inv: (e, 0, 0)),
                  pl.BlockSpec((1, EXPERT_FF, D_MODEL), lambda e, tb, tv, inv: (e, 0, 0))],
        out_specs=pl.BlockSpec(memory_space=pl.ANY),
        scratch_shapes=[pltpu.VMEM((GATHER_SLOTS, tm, D_MODEL), F32),
                        pltpu.VMEM((2, tm, D_MODEL), F32),
                        pltpu.VMEM((tm, D_MODEL), F32),
                        pltpu.SemaphoreType.DMA((GATHER_SLOTS,)),
                        pltpu.SemaphoreType.DMA((2,)),
                        pltpu.SemaphoreType.DMA],
    )
    return pl.pallas_call(
        functools.partial(_expert_ffn_kernel, t_max),
        grid_spec=grid_spec,
        out_shape=jax.ShapeDtypeStruct((t_max * tm, D_MODEL), F32),
        compiler_params=_cparams(("arbitrary",), VMEM_LIMIT),
        name="moe_expert_ffn",
    )(tile_base, tile_valid, inv, x1, w_gate, w_up, w_down)


def _combine_kernel(pos1_ref, pos2_ref, pos1n_ref, pos2n_ref, x1_ref, route_ref, ys_ref, p_ref, wpg_ref,
                    wpp_ref, g_ref, b_ref, o_ref, y1_ref, y2_ref, sem):
    i = pl.program_id(0)
    tm = route_ref.shape[0]
    slot = i % 2

    def gather_rows(p1_ref, p2_ref, buf_slot):
        def issue(r, carry):
            pltpu.make_async_copy(ys_ref.at[pl.ds(p1_ref[r], 1), :],
                                  y1_ref.at[buf_slot, pl.ds(r, 1), :], sem.at[buf_slot]).start()
            pltpu.make_async_copy(ys_ref.at[pl.ds(p2_ref[r], 1), :],
                                  y2_ref.at[buf_slot, pl.ds(r, 1), :], sem.at[buf_slot]).start()
            return carry
        lax.fori_loop(0, tm, issue, 0, unroll=8)

    @pl.when(i == 0)
    def _():
        gather_rows(pos1_ref, pos2_ref, 0)

    @pl.when(i + 1 < pl.num_programs(0))
    def _():
        gather_rows(pos1n_ref, pos2n_ref, 1 - slot)

    pltpu.make_async_copy(ys_ref.at[pl.ds(0, tm), :], y1_ref.at[slot], sem.at[slot]).wait()
    pltpu.make_async_copy(ys_ref.at[pl.ds(0, tm), :], y2_ref.at[slot], sem.at[slot]).wait()

    for s in range(tm // SUB_COMB):
        rows = pl.ds(s * SUB_COMB, SUB_COMB)
        route = route_ref[rows, :]
        m = route[:, 4:5] * y1_ref[slot, rows, :] + route[:, 5:6] * y2_ref[slot, rows, :]
        x2 = _layer_norm(DEEPNORM_ALPHA * x1_ref[rows, :] + m, g_ref[...], b_ref[...])
        gate = jax.nn.sigmoid(_dot_f32_weight(x2.astype(BF16), wpg_ref[...]))
        proj = _dot_f32_weight(p_ref[rows, :].astype(BF16), wpp_ref[...])
        o_ref[rows, :] = x2 + gate * proj


def _combine(pos1, pos2, x1, route, ys, p2, w_ple_gate_b, w_ple_proj_b, ln_g, ln_b):
    n = route.shape[0]
    tm = TM_ROW
    smem = pl.BlockSpec((tm,), lambda i: (i,), memory_space=pltpu.SMEM)
    smem_next = pl.BlockSpec((tm,), lambda i: (jnp.minimum(i + 1, n // tm - 1),), memory_space=pltpu.SMEM)
    row = lambda w: pl.BlockSpec((1, w), lambda i: (0, 0))
    return pl.pallas_call(
        _combine_kernel,
        grid=(n // tm,),
        in_specs=[smem, smem, smem_next, smem_next,
                  pl.BlockSpec((tm, D_MODEL), lambda i: (i, 0)),
                  pl.BlockSpec((tm, LANES), lambda i: (i, 0)),
                  pl.BlockSpec(memory_space=pl.ANY),
                  pl.BlockSpec((tm, PLE_DIM), lambda i: (i, 0)),
                  pl.BlockSpec((D_MODEL, D_MODEL), lambda i: (0, 0), pipeline_mode=pl.Buffered(1)),
                  pl.BlockSpec((PLE_DIM, D_MODEL), lambda i: (0, 0)),
                  row(D_MODEL), row(D_MODEL)],
        out_specs=pl.BlockSpec((tm, D_MODEL), lambda i: (i, 0)),
        out_shape=jax.ShapeDtypeStruct((n, D_MODEL), F32),
        scratch_shapes=[pltpu.VMEM((2, tm, D_MODEL), F32), pltpu.VMEM((2, tm, D_MODEL), F32),
                        pltpu.SemaphoreType.DMA((2,))],
        compiler_params=_cparams(("arbitrary",), VMEM_LIMIT),
        name="moe_combine_ln2_ple",
    )(pos1, pos2, pos1, pos2, x1, route, ys, p2, w_ple_gate_b, w_ple_proj_b, ln_g, ln_b)


def _tile_plan(counts, t_max):
    tm = TM_EXP
    cnt = counts[0, :N_EXPERTS].astype(jnp.int32)
    tiles = (cnt + tm - 1) // tm
    tile_base = jnp.concatenate([jnp.zeros((1,), jnp.int32), jnp.cumsum(tiles).astype(jnp.int32)])
    g = jnp.arange(t_max, dtype=jnp.int32)
    in_e = (g[:, None] >= tile_base[None, :-1]) & (g[:, None] < tile_base[None, 1:])
    rows = cnt[None, :] - (g[:, None] - tile_base[None, :-1]) * tm
    tile_valid = jnp.sum(jnp.where(in_e, jnp.clip(rows, 0, tm), 0), axis=1).astype(jnp.int32)
    return tile_base, tile_valid


def kernel(x, p, positions, w_in, w_out, da_lambda_q1, da_lambda_k1, da_lambda_q2, da_lambda_k2, da_subln_w, ln1_g, ln1_b, w_router_group, b_router_group, w_router_expert, b_router_expert, w_exp_gate, w_exp_up, w_exp_down, ln2_g, ln2_b, w_ple_gate, w_ple_proj):
    batch, seq, d = x.shape
    n = batch * seq
    assert d == D_MODEL and w_in.shape[0] == DEPTH == 1
    l = 0
    lam_init = 0.8 - 0.6 * math.exp(-0.3 * l)
    x2 = x.reshape(n, d)

    qkv = _in_proj(x2, w_in[l].astype(BF16), positions)
    oda = _diff_attn(qkv, da_lambda_q1[l], da_lambda_k1[l], da_lambda_q2[l], da_lambda_k2[l],
                     da_subln_w[l], lam_init, batch, seq)
    o_r = _retention(qkv, batch, seq)

    pad = LANES - N_GROUPS - N_EXPERTS
    w_router = jnp.concatenate([w_router_group[l], w_router_expert[l], jnp.zeros((d, pad), F32)], axis=1)
    b_router = jnp.concatenate([b_router_group[l], b_router_expert[l], jnp.zeros((pad,), F32)]).reshape(1, LANES)
    x1, route, counts = _out_proj(oda, o_r, x2, w_out[l],
                                  ln1_g[l].reshape(1, d), ln1_b[l].reshape(1, d), w_router, b_router)

    t_max = (2 * n) // TM_EXP + N_EXPERTS
    tile_base, tile_valid = _tile_plan(counts, t_max)
    pos = _slots(route, counts)
    pos1, pos2 = pos[0], pos[1]
    inv = _inverse_map(pos1, pos2, t_max * TM_EXP)
    ys = _expert_ffn(tile_base, tile_valid, inv, x1, w_exp_gate[l], w_exp_up[l], w_exp_down[l], t_max)
    out = _combine(pos1, pos2, x1, route, ys, p[l].reshape(n, PLE_DIM),
                   w_ple_gate[l].astype(BF16), w_ple_proj[l],
                   ln2_g[l].reshape(1, d), ln2_b[l].reshape(1, d))
    return out.reshape(batch, seq, d)
```

```python
import functools
import math

import numpy as np
import jax
import jax.numpy as jnp
from jax import lax
from jax.experimental import pallas as pl
from jax.experimental.pallas import tpu as pltpu

D_MODEL = 2048
DA_HEADS = 8
DA_HEAD_DIM = 64
DA_V_DIM = 128
RET_HEADS = 4
RET_K_DIM = 128
RET_V_DIM = 256
RET_CHUNK = 128
ROPE_THETA = 10000.0
PLE_DIM = 256
N_GROUPS = 4
EXPERTS_PER_GROUP = 8
N_EXPERTS = 32
EXPERT_FF = 512
DEPTH = 1
DEEPNORM_ALPHA = (2 * DEPTH) ** 0.25
LN_EPS = 1e-5
IN_WIDTH = 6144
DA_WIDTH = 1024
RET_WIDTH = 1024

OFF_QDA, OFF_KDA, OFF_VDA, OFF_QR, OFF_KR, OFF_VR, OFF_GR = 0, 1024, 2048, 3072, 3584, 4096, 5120

LANES = 128
SUBLANES = 8
VMEM_LIMIT = 56 * 1024 * 1024

BF16 = jnp.bfloat16
F32 = jnp.float32

TM_PROJ = 512
TN_PROJ = 512
W_CHUNK = 256
TQ = 256
TM_OUT = 512
SUB_COMB = 256
TM_ROW = 512
TM_EXP = 256
GATHER_GROUP = 8
GATHER_SLOTS = 3


def _cparams(sem, vmem=None):
    return pltpu.CompilerParams(dimension_semantics=sem, vmem_limit_bytes=vmem)


def _rope_tables(pos, inv, sgn):
    ang = pos.astype(F32) * inv
    c = jnp.cos(ang)
    s = jnp.sin(ang)
    low = lax.broadcasted_iota(jnp.int32, ang.shape, 1) < 64
    cr = pltpu.roll(c, 64, 1)
    sr = pltpu.roll(s, 64, 1)
    cos64, sin64 = jnp.where(low, cr, c), jnp.where(low, sr, s) * sgn[0:1, :]
    cos128, sin128 = jnp.where(low, c, cr), jnp.where(low, s, sr) * sgn[1:2, :]
    return cos64, sin64, cos128, sin128


def _rope_constants():
    i128 = np.power(ROPE_THETA, -np.arange(0, 128, 2, dtype=np.float64) / 128)
    i64 = np.power(ROPE_THETA, -np.arange(0, 64, 2, dtype=np.float64) / 64)
    inv = np.concatenate([i128, i64, i64]).astype(np.float32)[None, :]
    lane = np.arange(128)
    sgn = np.stack([np.where(lane % 64 < 32, -1.0, 1.0), np.where(lane < 64, -1.0, 1.0)]).astype(np.float32)
    sgn = np.concatenate([sgn, np.zeros((6, 128), np.float32)])
    return jnp.asarray(inv), jnp.asarray(sgn)


def _rope_cols(acc, cos, sin, half, scale):
    outs = []
    lane = lax.broadcasted_iota(jnp.int32, (acc.shape[0], LANES), 1)
    for c in range(acc.shape[1] // LANES):
        t = acc[:, c * LANES:(c + 1) * LANES]
        if half == 64:
            partner = pltpu.roll(t, 64, 1)
        else:
            partner = jnp.where((lane & 32) == 0, pltpu.roll(t, 96, 1), pltpu.roll(t, 32, 1))
        o = t * cos + partner * sin
        if scale != 1.0:
            o = o * scale
        outs.append(o)
    return jnp.concatenate(outs, axis=1)


def _in_proj_kernel(x_ref, w_hbm_ref, pos_ref, inv_ref, sgn_ref, o_ref, w_ref, stage_ref, sem):
    @pl.when(pl.program_id(0) == 0)
    def _():
        n_chunks = IN_WIDTH // W_CHUNK

        def fetch(c):
            return pltpu.make_async_copy(w_hbm_ref.at[:, c * W_CHUNK:(c + 1) * W_CHUNK],
                                         stage_ref.at[c % 2], sem.at[c % 2])
        fetch(0).start()
        for c in range(n_chunks):
            if c + 1 < n_chunks:
                fetch(c + 1).start()
            fetch(c).wait()
            w_ref[:, c * W_CHUNK:(c + 1) * W_CHUNK] = stage_ref[c % 2].astype(BF16)

    xb = x_ref[...].astype(BF16)
    cos64, sin64, cos128, sin128 = _rope_tables(pos_ref[...], inv_ref[...], sgn_ref[...])
    n_tiles = IN_WIDTH // TN_PROJ
    plain = [j for j in range(n_tiles) if OFF_VDA <= j * TN_PROJ < OFF_QR or j * TN_PROJ >= OFF_VR]
    rotary = [j for j in range(n_tiles) if j not in plain]
    for j in plain[:-2] + rotary + plain[-2:]:
        lo = j * TN_PROJ
        acc = jnp.dot(xb, w_ref[:, lo:lo + TN_PROJ], preferred_element_type=F32)
        if lo < OFF_VDA:
            acc = _rope_cols(acc, cos64, sin64, 32, 1.0)
        elif OFF_QR <= lo < OFF_KR:
            acc = _rope_cols(acc, cos128, sin128, 64, 1.0)
        elif OFF_KR <= lo < OFF_VR:
            acc = _rope_cols(acc, cos128, sin128, 64, RET_K_DIM ** -0.5)
        o_ref[:, lo:lo + TN_PROJ] = acc.astype(BF16)


def _in_proj(x2, w_in, positions):
    n = x2.shape[0]
    inv, sgn = _rope_constants()
    return pl.pallas_call(
        _in_proj_kernel,
        grid=(n // TM_PROJ,),
        in_specs=[pl.BlockSpec((TM_PROJ, D_MODEL), lambda i: (i, 0)),
                  pl.BlockSpec(memory_space=pl.ANY),
                  pl.BlockSpec((TM_PROJ, 1), lambda i: (i, 0)),
                  pl.BlockSpec((1, LANES), lambda i: (0, 0)),
                  pl.BlockSpec((SUBLANES, LANES), lambda i: (0, 0))],
        out_specs=pl.BlockSpec((TM_PROJ, IN_WIDTH), lambda i: (i, 0)),
        out_shape=jax.ShapeDtypeStruct((n, IN_WIDTH), BF16),
        scratch_shapes=[pltpu.VMEM((D_MODEL, IN_WIDTH), BF16),
                        pltpu.VMEM((2, D_MODEL, W_CHUNK), F32),
                        pltpu.SemaphoreType.DMA((2,))],
        compiler_params=_cparams(("arbitrary",), VMEM_LIMIT),
        name="in_proj",
    )(x2, w_in, positions.reshape(n, 1), inv, sgn)


def _diff_attn_kernel(lam_init, q_ref, k_ref, v_ref, lq1_ref, lk1_ref, lq2_ref, lk2_ref, sw_ref,
                      o_ref, m_ref, l_ref, acc_ref):
    seq = q_ref.shape[0]
    lam = (jnp.exp(jnp.sum(lq1_ref[...] * lk1_ref[...], axis=-1, keepdims=True))
           - jnp.exp(jnp.sum(lq2_ref[...] * lk2_ref[...], axis=-1, keepdims=True)) + lam_init)
    scale = DA_HEAD_DIM ** -0.5 * math.log2(math.e)
    lane = lax.broadcasted_iota(jnp.int32, (TQ, LANES), 1)
    row_i = lax.broadcasted_iota(jnp.int32, (2 * TQ, TQ), 0)
    col_i = lax.broadcasted_iota(jnp.int32, (2 * TQ, TQ), 1)
    causal = col_i <= jnp.where(row_i >= TQ, row_i - TQ, row_i)

    def step(qs, start, width, masked):
        k = k_ref[pl.ds(start, width), :]
        v = v_ref[pl.ds(start, width), :]
        s = lax.dot_general(qs, k, (((1,), (1,)), ((), ())), preferred_element_type=F32)
        if masked:
            s = jnp.where(causal, s, jnp.finfo(F32).min)
        chunks = [s[:, c * LANES:(c + 1) * LANES] for c in range(width // LANES)]
        m_old = m_ref[...]
        m_new = jnp.maximum(m_old, jnp.max(functools.reduce(jnp.maximum, chunks), axis=-1, keepdims=True))
        alpha = jnp.exp2(m_old - m_new)
        p = jnp.concatenate([jnp.exp2(c - m_new).astype(BF16) for c in chunks], axis=1)
        v_ext = jnp.concatenate([v, jnp.ones_like(v)], axis=1)
        pv = jnp.dot(p, v_ext, preferred_element_type=F32)
        acc_ref[...] = alpha * acc_ref[...] + pv[:, 0:LANES]
        l_ref[...] = alpha * l_ref[...] + pv[:, LANES:2 * LANES]
        m_ref[...] = m_new

    for qi in range(seq // TQ):
        q = q_ref[qi * TQ:(qi + 1) * TQ, :]
        zero = jnp.zeros_like(q)
        qs = jnp.concatenate([jnp.where(lane < 64, q, zero), jnp.where(lane >= 64, q, zero)], axis=0)
        qs = (qs.astype(F32) * scale).astype(BF16)
        m_ref[...] = jnp.full(m_ref.shape, -jnp.inf, F32)
        l_ref[...] = jnp.zeros(l_ref.shape, F32)
        acc_ref[...] = jnp.zeros(acc_ref.shape, F32)

        def pair(j, carry, qs=qs):
            step(qs, pl.multiple_of(j * (2 * TQ), 2 * TQ), 2 * TQ, False)
            return carry

        lax.fori_loop(0, qi // 2, pair, 0, unroll=True)
        if qi % 2:
            step(qs, (qi - 1) * TQ, TQ, False)
        step(qs, qi * TQ, TQ, True)

        a = acc_ref[...] / l_ref[...]
        o = a[0:TQ, :] - lam * a[TQ:2 * TQ, :]
        o = o * lax.rsqrt(jnp.mean(o * o, axis=-1, keepdims=True) + LN_EPS)
        o = o * sw_ref[...] * (1.0 - lam_init)
        o_ref[qi * TQ:(qi + 1) * TQ, :] = o.astype(BF16)


def _diff_attn(qkv, lq1, lk1, lq2, lk2, subln_w, lam_init, batch, seq):
    n = qkv.shape[0]
    qb0 = OFF_QDA // LANES
    kb0 = OFF_KDA // LANES
    vb0 = OFF_VDA // LANES
    vec = lambda a: a.reshape(1, -1).astype(F32)
    small = lambda w: pl.BlockSpec((1, w), lambda b, h: (0, 0))
    return pl.pallas_call(
        functools.partial(_diff_attn_kernel, lam_init),
        grid=(batch, DA_HEADS),
        in_specs=[pl.BlockSpec((seq, LANES), lambda b, h: (b, qb0 + h)),
                  pl.BlockSpec((seq, LANES), lambda b, h: (b, kb0 + h)),
                  pl.BlockSpec((seq, LANES), lambda b, h: (b, vb0 + h)),
                  small(64), small(64), small(64), small(64), small(128)],
        out_specs=pl.BlockSpec((seq, LANES), lambda b, h: (b, h)),
        out_shape=jax.ShapeDtypeStruct((n, DA_WIDTH), BF16),
        scratch_shapes=[pltpu.VMEM((2 * TQ, LANES), F32), pltpu.VMEM((2 * TQ, LANES), F32),
                        pltpu.VMEM((2 * TQ, LANES), F32)],
        compiler_params=_cparams(("arbitrary", "arbitrary")),
        name="diff_attn",
    )(qkv, qkv, qkv, vec(lq1), vec(lk1), vec(lq2), vec(lk2), vec(subln_w))


def _retention_kernel(q_ref, k_ref, v_ref, g_ref, lg_ref, o_ref, state_ref):
    C = RET_CHUNK
    lg = lg_ref[0, 0:1, :]
    lg_col = lg[:, 0:1]
    n_i = lax.broadcasted_iota(jnp.int32, (C, C), 0)
    m_i = lax.broadcasted_iota(jnp.int32, (C, C), 1)
    rel = (n_i - m_i).astype(F32)
    decay = jnp.where(rel >= 0, jnp.exp(rel * lg_col), 0.0)
    n_col = lax.broadcasted_iota(jnp.int32, (C, 1), 0).astype(F32)
    zeta = jnp.exp((C - 1.0 - n_col) * lg_col)
    xi = jnp.exp((n_col + 1.0) * lg_col)
    chunk_decay = jnp.exp(C * lg_col)
    state_ref[...] = jnp.zeros(state_ref.shape, F32)

    def body(c, carry):
        start = pl.multiple_of(c * C, C)
        q = q_ref[pl.ds(start, C), :]
        k = k_ref[pl.ds(start, C), :]
        v = v_ref[pl.ds(start, C), :]
        g = g_ref[pl.ds(start, C), :].astype(F32)
        qk = lax.dot_general(q, k, (((1,), (1,)), ((), ())), preferred_element_type=F32)
        inner = (qk * decay).astype(BF16)
        inner_o = jnp.dot(inner, v, preferred_element_type=F32)
        state = state_ref[...]
        cross = jnp.dot(q, state.astype(BF16), preferred_element_type=F32)
        y = inner_o + cross * xi
        kz_t = (k.astype(F32) * zeta).T.astype(BF16)
        kv = jnp.dot(kz_t, v, preferred_element_type=F32)
        state_ref[...] = chunk_decay * state + kv
        mu = jnp.mean(y, axis=-1, keepdims=True)
        yc = y - mu
        var = jnp.mean(yc * yc, axis=-1, keepdims=True)
        yn = yc * lax.rsqrt(var + LN_EPS)
        o_ref[pl.ds(start, C), :] = (g * jax.nn.sigmoid(g) * yn).astype(BF16)
        return carry

    lax.fori_loop(0, q_ref.shape[0] // C, body, 0, unroll=True)


def _retention(qkv, batch, seq):
    n = qkv.shape[0]
    lg = np.log(1.0 - np.power(2.0, -5.0 - np.arange(RET_HEADS, dtype=np.float64)))
    lg_tab = np.broadcast_to(lg[:, None, None], (RET_HEADS, SUBLANES, LANES)).astype(np.float32)
    qb0 = OFF_QR // RET_K_DIM
    kb0 = OFF_KR // RET_K_DIM
    vb0 = OFF_VR // RET_V_DIM
    gb0 = OFF_GR // RET_V_DIM
    return pl.pallas_call(
        _retention_kernel,
        grid=(batch, RET_HEADS),
        in_specs=[pl.BlockSpec((seq, RET_K_DIM), lambda b, h: (b, qb0 + h)),
                  pl.BlockSpec((seq, RET_K_DIM), lambda b, h: (b, kb0 + h)),
                  pl.BlockSpec((seq, RET_V_DIM), lambda b, h: (b, vb0 + h)),
                  pl.BlockSpec((seq, RET_V_DIM), lambda b, h: (b, gb0 + h)),
                  pl.BlockSpec((1, SUBLANES, LANES), lambda b, h: (h, 0, 0))],
        out_specs=pl.BlockSpec((seq, RET_V_DIM), lambda b, h: (b, h)),
        out_shape=jax.ShapeDtypeStruct((n, RET_WIDTH), BF16),
        scratch_shapes=[pltpu.VMEM((RET_K_DIM, RET_V_DIM), F32)],
        compiler_params=_cparams(("arbitrary", "arbitrary")),
        name="retention",
    )(qkv, qkv, qkv, qkv, jnp.asarray(lg_tab))


def _layer_norm(z, g, b):
    mu = jnp.mean(z, axis=-1, keepdims=True)
    zc = z - mu
    var = jnp.mean(zc * zc, axis=-1, keepdims=True)
    return zc * lax.rsqrt(var + LN_EPS) * g + b


def _dot_f32_weight(a, w):
    return lax.dot_general(a, w, (((1,), (0,)), ((), ())), preferred_element_type=F32)


def _split_bf16(a):
    hi = a.astype(BF16)
    lo = (a - hi.astype(F32)).astype(BF16)
    return hi, lo


def _out_proj_kernel(oda_ref, or_ref, x_ref, w_ref, g_ref, b_ref, wr_ref, br_ref,
                     x1_ref, route_ref, cnt_ref, run_ref):
    i = pl.program_id(0)

    @pl.when(i == 0)
    def _():
        run_ref[...] = jnp.zeros(run_ref.shape, F32)

    tm = x_ref.shape[0]
    run = run_ref[...]
    h = _dot_f32_weight(oda_ref[...], w_ref[0:DA_WIDTH, :])
    h = h + _dot_f32_weight(or_ref[...], w_ref[DA_WIDTH:DA_WIDTH + RET_WIDTH, :])
    x1 = _layer_norm(DEEPNORM_ALPHA * x_ref[...] + h, g_ref[...], b_ref[...])
    x1_ref[...] = x1

    xh, xl = _split_bf16(x1)
    wh, wl = _split_bf16(wr_ref[...])
    hh_hl = jnp.dot(xh, jnp.concatenate([wh, wl], axis=1), preferred_element_type=F32)
    logits = (hh_hl[:, 0:LANES] + hh_hl[:, LANES:2 * LANES]
              + jnp.dot(xl, wh, preferred_element_type=F32)) + br_ref[...]
    lane = lax.broadcasted_iota(jnp.int32, logits.shape, 1).astype(F32)
    neg = jnp.float32(-jnp.inf)
    big = jnp.float32(1 << 20)
    is_g = lane < N_GROUPS
    gl = jnp.where(is_g, logits, neg)
    gmax = jnp.max(gl, axis=-1, keepdims=True)
    g_idx = jnp.min(jnp.where(is_g & (gl == gmax), lane, big), axis=-1, keepdims=True)
    g_w = 1.0 / jnp.sum(jnp.where(is_g, jnp.exp(gl - gmax), 0.0), axis=-1, keepdims=True)
    e_lane = lane - N_GROUPS
    in_grp = (e_lane >= g_idx * EXPERTS_PER_GROUP) & (e_lane < (g_idx + 1.0) * EXPERTS_PER_GROUP)
    el = jnp.where(in_grp, logits, neg)
    v1 = jnp.max(el, axis=-1, keepdims=True)
    l1 = jnp.min(jnp.where(in_grp & (el == v1), lane, big), axis=-1, keepdims=True)
    el2 = jnp.where(lane == l1, neg, el)
    v2 = jnp.max(el2, axis=-1, keepdims=True)
    l2 = jnp.min(jnp.where(in_grp & (lane != l1) & (el2 == v2), lane, big), axis=-1, keepdims=True)
    t = jnp.exp(v2 - v1)
    fw1 = g_w / (1.0 + t)
    fw2 = g_w * t / (1.0 + t)
    e1 = l1 - N_GROUPS
    e2 = l2 - N_GROUPS

    oh1 = (lane == e1).astype(F32)
    oh2 = (lane == e2).astype(F32)
    cnt = oh1 + oh2
    r_i = lax.broadcasted_iota(jnp.int32, (tm, tm), 0)
    c_i = lax.broadcasted_iota(jnp.int32, (tm, tm), 1)
    tri = (c_i < r_i).astype(BF16)
    before = jnp.dot(tri, cnt.astype(BF16), preferred_element_type=F32) + run
    rank1 = jnp.sum(oh1 * before, axis=-1, keepdims=True)
    rank2 = jnp.sum(oh2 * before, axis=-1, keepdims=True)

    route = jnp.where(lane == 0, e1, 0.0)
    route = jnp.where(lane == 1, e2, route)
    route = jnp.where(lane == 2, rank1, route)
    route = jnp.where(lane == 3, rank2, route)
    route = jnp.where(lane == 4, fw1, route)
    route = jnp.where(lane == 5, fw2, route)
    route_ref[...] = route
    run = run + jnp.sum(cnt, axis=0, keepdims=True)
    run_ref[...] = run
    cnt_ref[...] = jnp.broadcast_to(run, cnt_ref.shape)


def _out_proj(oda, o_r, x2, w_out_b, ln_g, ln_b, w_router, b_router):
    n = x2.shape[0]
    tm = TM_OUT
    row = lambda w: pl.BlockSpec((1, w), lambda i: (0, 0))
    return pl.pallas_call(
        _out_proj_kernel,
        grid=(n // tm,),
        in_specs=[pl.BlockSpec((tm, DA_WIDTH), lambda i: (i, 0)),
                  pl.BlockSpec((tm, RET_WIDTH), lambda i: (i, 0)),
                  pl.BlockSpec((tm, D_MODEL), lambda i: (i, 0)),
                  pl.BlockSpec((D_MODEL, D_MODEL), lambda i: (0, 0), pipeline_mode=pl.Buffered(1)),
                  row(D_MODEL), row(D_MODEL),
                  pl.BlockSpec((D_MODEL, LANES), lambda i: (0, 0)),
                  row(LANES)],
        out_specs=[pl.BlockSpec((tm, D_MODEL), lambda i: (i, 0)),
                   pl.BlockSpec((tm, LANES), lambda i: (i, 0)),
                   pl.BlockSpec((SUBLANES, LANES), lambda i: (0, 0))],
        out_shape=[jax.ShapeDtypeStruct((n, D_MODEL), F32),
                   jax.ShapeDtypeStruct((n, LANES), F32),
                   jax.ShapeDtypeStruct((SUBLANES, LANES), F32)],
        scratch_shapes=[pltpu.VMEM((1, LANES), F32)],
        compiler_params=_cparams(("arbitrary",), VMEM_LIMIT),
        name="out_proj_ln1_router",
    )(oda, o_r, x2, w_out_b, ln_g, ln_b, w_router, b_router)


def _slot_kernel(route_ref, cnt_ref, pos_ref):
    route = route_ref[...]
    lane = lax.broadcasted_iota(jnp.int32, (1, LANES), 1)
    cnt = jnp.where(lane < N_EXPERTS, cnt_ref[0:1, :], 0.0)
    padded = jnp.floor((cnt + (TM_EXP - 1)) * (1.0 / TM_EXP)) * TM_EXP
    ends = padded
    for sh in (1, 2, 4, 8, 16):
        ends = ends + jnp.where(lane >= sh, pltpu.roll(ends, sh, 1), 0.0)
    offs = ends - padded
    lane_f = lax.broadcasted_iota(jnp.int32, route.shape, 1).astype(F32)
    off1 = jnp.sum(jnp.where(lane_f == route[:, 0:1], offs, 0.0), axis=-1, keepdims=True)
    off2 = jnp.sum(jnp.where(lane_f == route[:, 1:2], offs, 0.0), axis=-1, keepdims=True)
    slots = jnp.where(lane_f == 0.0, off1 + route[:, 2:3], 0.0)
    slots = jnp.where(lane_f == 1.0, off2 + route[:, 3:4], slots)
    pos_ref[...] = slots.T[0:SUBLANES, :].astype(jnp.int32)


def _slots(route, counts):
    n = route.shape[0]
    tm = 1024
    return pl.pallas_call(
        _slot_kernel,
        grid=(n // tm,),
        in_specs=[pl.BlockSpec((tm, LANES), lambda i: (i, 0)),
                  pl.BlockSpec((SUBLANES, LANES), lambda i: (0, 0))],
        out_specs=pl.BlockSpec((SUBLANES, tm), lambda i: (0, i)),
        out_shape=jax.ShapeDtypeStruct((SUBLANES, n), jnp.int32),
        compiler_params=_cparams(("arbitrary",)),
        name="moe_slots",
    )(route, counts)


def _inverse_kernel(pos1_ref, pos2_ref, fill_ref, inv_ref, sem):
    i = pl.program_id(0)
    tm = pos1_ref.shape[0]

    @pl.when(i == 0)
    def _():
        fill = pltpu.make_async_copy(fill_ref, inv_ref, sem)
        fill.start()
        fill.wait()

    def scatter(r, carry):
        tok = i * tm + r
        inv_ref[pos1_ref[r]] = tok
        inv_ref[pos2_ref[r]] = tok
        return carry

    lax.fori_loop(0, tm, scatter, 0, unroll=8)


def _inverse_map(pos1, pos2, p_rows):
    n = pos1.shape[0]
    tm = 1024
    smem = pl.BlockSpec((tm,), lambda i: (i,), memory_space=pltpu.SMEM)
    fill = jnp.asarray(np.arange(p_rows, dtype=np.int32) % n)
    return pl.pallas_call(
        _inverse_kernel,
        grid=(n // tm,),
        in_specs=[smem, smem, pl.BlockSpec(memory_space=pl.ANY)],
        out_specs=pl.BlockSpec(memory_space=pltpu.SMEM),
        out_shape=jax.ShapeDtypeStruct((p_rows,), jnp.int32),
        scratch_shapes=[pltpu.SemaphoreType.DMA],
        compiler_params=_cparams(("arbitrary",)),
        name="moe_inverse_map",
    )(pos1, pos2, fill)


def _expert_ffn_kernel(t_max, tb_ref, tv_ref, inv_ref, x1_ref, wg_ref, wu_ref, wd_ref, ys_ref,
                       xbuf_ref, ybuf_ref, zbuf_ref, gsem, osem, zsem):
    e = pl.program_id(0)
    tm = TM_EXP
    g_first = tb_ref[e]
    g_end = tb_ref[e + 1]
    n_total = tb_ref[N_EXPERTS]

    def gathered_rows(g):
        return ((tv_ref[g] + (GATHER_GROUP - 1)) // GATHER_GROUP) * GATHER_GROUP

    def gather_rows(g, buf_slot):
        def issue(q, carry):
            for k in range(GATHER_GROUP):
                r = q * GATHER_GROUP + k
                pltpu.make_async_copy(x1_ref.at[pl.ds(inv_ref[g * tm + r], 1), :],
                                      xbuf_ref.at[buf_slot, pl.ds(r, 1), :], gsem.at[buf_slot]).start()
            return carry
        lax.fori_loop(0, gathered_rows(g) // GATHER_GROUP, issue, 0)

    def wait_rows(g, buf_slot):
        rows = gathered_rows(g)
        k = GATHER_GROUP
        while k <= tm:
            @pl.when((rows & k) != 0)
            def _(k=k):
                pltpu.make_async_copy(x1_ref.at[pl.ds(0, k), :], xbuf_ref.at[buf_slot, pl.ds(0, k), :],
                                      gsem.at[buf_slot]).wait()
            k *= 2

    def tile_out(g):
        return ys_ref.at[pl.ds(pl.multiple_of(g * tm, tm), tm), :]

    @pl.when(e == 0)
    def _():
        xbuf_ref[...] = jnp.zeros(xbuf_ref.shape, F32)
        gather_rows(0, 0)

        @pl.when(n_total > 1)
        def _():
            gather_rows(1, 1)
        zbuf_ref[...] = jnp.zeros(zbuf_ref.shape, F32)

        def fill(g, carry):
            pltpu.make_async_copy(zbuf_ref, tile_out(g), zsem).start()
            return carry
        lax.fori_loop(n_total, t_max, fill, 0)

    def tile(g, carry):
        slot = g % 2
        xslot = g % GATHER_SLOTS

        @pl.when(g + (GATHER_SLOTS - 1) < n_total)
        def _():
            gather_rows(g + (GATHER_SLOTS - 1), (g + (GATHER_SLOTS - 1)) % GATHER_SLOTS)

        wait_rows(g, xslot)

        @pl.when(g >= 2)
        def _():
            pltpu.make_async_copy(ybuf_ref.at[slot], tile_out(g), osem.at[slot]).wait()

        def ffn(rows):
            x = xbuf_ref[xslot, 0:rows, :].astype(BF16)
            gate = _dot_f32_weight(x, wg_ref[0])
            up = _dot_f32_weight(x, wu_ref[0])
            hmid = (gate * jax.nn.sigmoid(gate) * up).astype(BF16)
            ybuf_ref[slot, 0:rows, :] = _dot_f32_weight(hmid, wd_ref[0])

        half = tm // 2

        @pl.when(tv_ref[g] > half)
        def _():
            ffn(tm)

        @pl.when(tv_ref[g] <= half)
        def _():
            ffn(half)
            ybuf_ref[slot, half:tm, :] = jnp.zeros((tm - half, D_MODEL), F32)

        pltpu.make_async_copy(ybuf_ref.at[slot], tile_out(g), osem.at[slot]).start()
        return carry

    lax.fori_loop(g_first, g_end, tile, 0)

    @pl.when(e == pl.num_programs(0) - 1)
    def _():
        pltpu.make_async_copy(ybuf_ref.at[0], tile_out(0), osem.at[0]).wait()
        pltpu.make_async_copy(ybuf_ref.at[1], tile_out(0), osem.at[1]).wait()

        def drain(g, carry):
            pltpu.make_async_copy(zbuf_ref, tile_out(g), zsem).wait()
            return carry
        lax.fori_loop(n_total, t_max, drain, 0)


def _expert_ffn(tile_base, tile_valid, inv, x1, w_gate, w_up, w_down, t_max):
    tm = TM_EXP
    assert 2 * x1.shape[0] >= 2 * tm
    grid_spec = pltpu.PrefetchScalarGridSpec(
        num_scalar_prefetch=3,
        grid=(N_EXPERTS,),
        in_specs=[pl.BlockSpec(memory_space=pl.ANY),
                  pl.BlockSpec((1, D_MODEL, EXPERT_FF), lambda e, tb, tv, inv: (e, 0, 0)),
                  pl.BlockSpec((1, D_MODEL, EXPERT_FF), lambda e, tb, tv, inv: (e, 0, 0)),
                  pl.BlockSpec((1, EXPERT_FF, D_MODEL), lambda e, tb, tv, inv: (e, 0, 0))],
        out_specs=pl.BlockSpec(memory_space=pl.ANY),
        scratch_shapes=[pltpu.VMEM((GATHER_SLOTS, tm, D_MODEL), F32),
                        pltpu.VMEM((2, tm, D_MODEL), F32),
                        pltpu.VMEM((tm, D_MODEL), F32),
                        pltpu.SemaphoreType.DMA((GATHER_SLOTS,)),
                        pltpu.SemaphoreType.DMA((2,)),
                        pltpu.SemaphoreType.DMA],
    )
    return pl.pallas_call(
        functools.partial(_expert_ffn_kernel, t_max),
        grid_spec=grid_spec,
        out_shape=jax.ShapeDtypeStruct((t_max * tm, D_MODEL), F32),
        compiler_params=_cparams(("arbitrary",), VMEM_LIMIT),
        name="moe_expert_ffn",
    )(tile_base, tile_valid, inv, x1, w_gate, w_up, w_down)


def _combine_kernel(pos1_ref, pos2_ref, pos1n_ref, pos2n_ref, x1_ref, route_ref, ys_ref, p_ref, wpg_ref,
                    wpp_ref, g_ref, b_ref, o_ref, y1_ref, y2_ref, sem):
    i = pl.program_id(0)
    tm = route_ref.shape[0]
    slot = i % 2

    def gather_rows(p1_ref, p2_ref, buf_slot):
        def issue(r, carry):
            pltpu.make_async_copy(ys_ref.at[pl.ds(p1_ref[r], 1), :],
                                  y1_ref.at[buf_slot, pl.ds(r, 1), :], sem.at[buf_slot]).start()
            pltpu.make_async_copy(ys_ref.at[pl.ds(p2_ref[r], 1), :],
                                  y2_ref.at[buf_slot, pl.ds(r, 1), :], sem.at[buf_slot]).start()
            return carry
        lax.fori_loop(0, tm, issue, 0, unroll=8)

    @pl.when(i == 0)
    def _():
        gather_rows(pos1_ref, pos2_ref, 0)

    @pl.when(i + 1 < pl.num_programs(0))
    def _():
        gather_rows(pos1n_ref, pos2n_ref, 1 - slot)

    pltpu.make_async_copy(ys_ref.at[pl.ds(0, tm), :], y1_ref.at[slot], sem.at[slot]).wait()
    pltpu.make_async_copy(ys_ref.at[pl.ds(0, tm), :], y2_ref.at[slot], sem.at[slot]).wait()

    for s in range(tm // SUB_COMB):
        rows = pl.ds(s * SUB_COMB, SUB_COMB)
        route = route_ref[rows, :]
        m = route[:, 4:5] * y1_ref[slot, rows, :] + route[:, 5:6] * y2_ref[slot, rows, :]
        x2 = _layer_norm(DEEPNORM_ALPHA * x1_ref[rows, :] + m, g_ref[...], b_ref[...])
        gate = jax.nn.sigmoid(_dot_f32_weight(x2.astype(BF16), wpg_ref[...]))
        proj = _dot_f32_weight(p_ref[rows, :].astype(BF16), wpp_ref[...])
        o_ref[rows, :] = x2 + gate * proj


def _combine(pos1, pos2, x1, route, ys, p2, w_ple_gate_b, w_ple_proj_b, ln_g, ln_b):
    n = route.shape[0]
    tm = TM_ROW
    smem = pl.BlockSpec((tm,), lambda i: (i,), memory_space=pltpu.SMEM)
    smem_next = pl.BlockSpec((tm,), lambda i: (jnp.minimum(i + 1, n // tm - 1),), memory_space=pltpu.SMEM)
    row = lambda w: pl.BlockSpec((1, w), lambda i: (0, 0))
    return pl.pallas_call(
        _combine_kernel,
        grid=(n // tm,),
        in_specs=[smem, smem, smem_next, smem_next,
                  pl.BlockSpec((tm, D_MODEL), lambda i: (i, 0)),
                  pl.BlockSpec((tm, LANES), lambda i: (i, 0)),
                  pl.BlockSpec(memory_space=pl.ANY),
                  pl.BlockSpec((tm, PLE_DIM), lambda i: (i, 0)),
                  pl.BlockSpec((D_MODEL, D_MODEL), lambda i: (0, 0), pipeline_mode=pl.Buffered(1)),
                  pl.BlockSpec((PLE_DIM, D_MODEL), lambda i: (0, 0)),
                  row(D_MODEL), row(D_MODEL)],
        out_specs=pl.BlockSpec((tm, D_MODEL), lambda i: (i, 0)),
        out_shape=jax.ShapeDtypeStruct((n, D_MODEL), F32),
        scratch_shapes=[pltpu.VMEM((2, tm, D_MODEL), F32), pltpu.VMEM((2, tm, D_MODEL), F32),
                        pltpu.SemaphoreType.DMA((2,))],
        compiler_params=_cparams(("arbitrary",), VMEM_LIMIT),
        name="moe_combine_ln2_ple",
    )(pos1, pos2, pos1, pos2, x1, route, ys, p2, w_ple_gate_b, w_ple_proj_b, ln_g, ln_b)


def _tile_plan(counts, t_max):
    tm = TM_EXP
    cnt = counts[0, :N_EXPERTS].astype(jnp.int32)
    tiles = (cnt + tm - 1) // tm
    tile_base = jnp.concatenate([jnp.zeros((1,), jnp.int32), jnp.cumsum(tiles).astype(jnp.int32)])
    g = jnp.arange(t_max, dtype=jnp.int32)
    in_e = (g[:, None] >= tile_base[None, :-1]) & (g[:, None] < tile_base[None, 1:])
    rows = cnt[None, :] - (g[:, None] - tile_base[None, :-1]) * tm
    tile_valid = jnp.sum(jnp.where(in_e, jnp.clip(rows, 0, tm), 0), axis=1).astype(jnp.int32)
    return tile_base, tile_valid


def kernel(x, p, positions, w_in, w_out, da_lambda_q1, da_lambda_k1, da_lambda_q2, da_lambda_k2, da_subln_w, ln1_g, ln1_b, w_router_group, b_router_group, w_router_expert, b_router_expert, w_exp_gate, w_exp_up, w_exp_down, ln2_g, ln2_b, w_ple_gate, w_ple_proj):
    batch, seq, d = x.shape
    n = batch * seq
    assert d == D_MODEL and w_in.shape[0] == DEPTH == 1
    l = 0
    lam_init = 0.8 - 0.6 * math.exp(-0.3 * l)
    x2 = x.reshape(n, d)

    qkv = _in_proj(x2, w_in[l], positions)
    oda = _diff_attn(qkv, da_lambda_q1[l], da_lambda_k1[l], da_lambda_q2[l], da_lambda_k2[l],
                     da_subln_w[l], lam_init, batch, seq)
    o_r = _retention(qkv, batch, seq)

    pad = LANES - N_GROUPS - N_EXPERTS
    w_router = jnp.concatenate([w_router_group[l], w_router_expert[l], jnp.zeros((d, pad), F32)], axis=1)
    b_router = jnp.concatenate([b_router_group[l], b_router_expert[l], jnp.zeros((pad,), F32)]).reshape(1, LANES)
    x1, route, counts = _out_proj(oda, o_r, x2, w_out[l],
                                  ln1_g[l].reshape(1, d), ln1_b[l].reshape(1, d), w_router, b_router)

    t_max = (2 * n) // TM_EXP + N_EXPERTS
    tile_base, tile_valid = _tile_plan(counts, t_max)
    pos = _slots(route, counts)
    pos1, pos2 = pos[0], pos[1]
    inv = _inverse_map(pos1, pos2, t_max * TM_EXP)
    ys = _expert_ffn(tile_base, tile_valid, inv, x1, w_exp_gate[l], w_exp_up[l], w_exp_down[l], t_max)
    out = _combine(pos1, pos2, x1, route, ys, p[l].reshape(n, PLE_DIM),
                   w_ple_gate[l].astype(BF16), w_ple_proj[l],
                   ln2_g[l].reshape(1, d), ln2_b[l].reshape(1, d))
    return out.reshape(batch, seq, d)
```

```python
import functools
import math

import numpy as np
import jax
import jax.numpy as jnp
from jax import lax
from jax.experimental import pallas as pl
from jax.experimental.pallas import tpu as pltpu

D_MODEL = 2048
DA_HEADS = 8
DA_HEAD_DIM = 64
DA_V_DIM = 128
RET_HEADS = 4
RET_K_DIM = 128
RET_V_DIM = 256
RET_CHUNK = 128
RET_SPLIT = DA_HEADS // RET_HEADS
ROPE_THETA = 10000.0
PLE_DIM = 256
N_GROUPS = 4
EXPERTS_PER_GROUP = 8
N_EXPERTS = 32
EXPERT_FF = 512
DEPTH = 1
DEEPNORM_ALPHA = (2 * DEPTH) ** 0.25
LN_EPS = 1e-5
IN_WIDTH = 6144
DA_WIDTH = 1024
RET_WIDTH = 1024

OFF_QDA, OFF_KDA, OFF_VDA, OFF_QR, OFF_KR, OFF_VR, OFF_GR = 0, 1024, 2048, 3072, 3584, 4096, 5120

LANES = 128
SUBLANES = 8
VMEM_LIMIT = 56 * 1024 * 1024

BF16 = jnp.bfloat16
F32 = jnp.float32

TM_PROJ = 512
TN_PROJ = 512
W_CHUNK = 256
TQ = 256
TM_OUT = 512
SUB_COMB = 256
TM_ROW = 512
TM_EXP = 256
GATHER_GROUP = 8
GATHER_SLOTS = 3


def _cparams(sem, vmem=None):
    return pltpu.CompilerParams(dimension_semantics=sem, vmem_limit_bytes=vmem)


def _rope_tables(pos, inv, sgn):
    ang = pos.astype(F32) * inv
    c = jnp.cos(ang)
    s = jnp.sin(ang)
    low = lax.broadcasted_iota(jnp.int32, ang.shape, 1) < 64
    cr = pltpu.roll(c, 64, 1)
    sr = pltpu.roll(s, 64, 1)
    cos64, sin64 = jnp.where(low, cr, c), jnp.where(low, sr, s) * sgn[0:1, :]
    cos128, sin128 = jnp.where(low, c, cr), jnp.where(low, s, sr) * sgn[1:2, :]
    return cos64, sin64, cos128, sin128


def _rope_constants():
    i128 = np.power(ROPE_THETA, -np.arange(0, 128, 2, dtype=np.float64) / 128)
    i64 = np.power(ROPE_THETA, -np.arange(0, 64, 2, dtype=np.float64) / 64)
    inv = np.concatenate([i128, i64, i64]).astype(np.float32)[None, :]
    lane = np.arange(128)
    sgn = np.stack([np.where(lane % 64 < 32, -1.0, 1.0), np.where(lane < 64, -1.0, 1.0)]).astype(np.float32)
    sgn = np.concatenate([sgn, np.zeros((6, 128), np.float32)])
    return jnp.asarray(inv), jnp.asarray(sgn)


def _rope_cols(acc, cos, sin, half, scale):
    outs = []
    lane = lax.broadcasted_iota(jnp.int32, (acc.shape[0], LANES), 1)
    for c in range(acc.shape[1] // LANES):
        t = acc[:, c * LANES:(c + 1) * LANES]
        if half == 64:
            partner = pltpu.roll(t, 64, 1)
        else:
            partner = jnp.where((lane & 32) == 0, pltpu.roll(t, 96, 1), pltpu.roll(t, 32, 1))
        o = t * cos + partner * sin
        if scale != 1.0:
            o = o * scale
        outs.append(o)
    return jnp.concatenate(outs, axis=1)


def _in_proj_kernel(x_ref, w_hbm_ref, pos_ref, inv_ref, sgn_ref, o_ref, w_ref, stage_ref, sem):
    @pl.when(pl.program_id(0) == 0)
    def _():
        n_chunks = IN_WIDTH // W_CHUNK

        def fetch(c):
            return pltpu.make_async_copy(w_hbm_ref.at[:, c * W_CHUNK:(c + 1) * W_CHUNK],
                                         stage_ref.at[c % 2], sem.at[c % 2])
        fetch(0).start()
        for c in range(n_chunks):
            if c + 1 < n_chunks:
                fetch(c + 1).start()
            fetch(c).wait()
            w_ref[:, c * W_CHUNK:(c + 1) * W_CHUNK] = stage_ref[c % 2].astype(BF16)

    xb = x_ref[...].astype(BF16)
    cos64, sin64, cos128, sin128 = _rope_tables(pos_ref[...], inv_ref[...], sgn_ref[...])
    n_tiles = IN_WIDTH // TN_PROJ
    plain = [j for j in range(n_tiles) if OFF_VDA <= j * TN_PROJ < OFF_QR or j * TN_PROJ >= OFF_VR]
    rotary = [j for j in range(n_tiles) if j not in plain]
    for j in plain[:-2] + rotary + plain[-2:]:
        lo = j * TN_PROJ
        acc = jnp.dot(xb, w_ref[:, lo:lo + TN_PROJ], preferred_element_type=F32)
        if lo < OFF_VDA:
            acc = _rope_cols(acc, cos64, sin64, 32, 1.0)
        elif OFF_QR <= lo < OFF_KR:
            acc = _rope_cols(acc, cos128, sin128, 64, 1.0)
        elif OFF_KR <= lo < OFF_VR:
            acc = _rope_cols(acc, cos128, sin128, 64, RET_K_DIM ** -0.5)
        o_ref[:, lo:lo + TN_PROJ] = acc.astype(BF16)


def _in_proj(x2, w_in, positions):
    n = x2.shape[0]
    inv, sgn = _rope_constants()
    return pl.pallas_call(
        _in_proj_kernel,
        grid=(n // TM_PROJ,),
        in_specs=[pl.BlockSpec((TM_PROJ, D_MODEL), lambda i: (i, 0)),
                  pl.BlockSpec(memory_space=pl.ANY),
                  pl.BlockSpec((TM_PROJ, 1), lambda i: (i, 0)),
                  pl.BlockSpec((1, LANES), lambda i: (0, 0)),
                  pl.BlockSpec((SUBLANES, LANES), lambda i: (0, 0))],
        out_specs=pl.BlockSpec((TM_PROJ, IN_WIDTH), lambda i: (i, 0)),
        out_shape=jax.ShapeDtypeStruct((n, IN_WIDTH), BF16),
        scratch_shapes=[pltpu.VMEM((D_MODEL, IN_WIDTH), BF16),
                        pltpu.VMEM((2, D_MODEL, W_CHUNK), F32),
                        pltpu.SemaphoreType.DMA((2,))],
        compiler_params=_cparams(("arbitrary",), VMEM_LIMIT),
        name="in_proj",
    )(x2, w_in, positions.reshape(n, 1), inv, sgn)


def _mixer_kernel(lam_init, q_ref, k_ref, v_ref, lq1_ref, lk1_ref, lq2_ref, lk2_ref, sw_ref,
                  rq_ref, rk_ref, rv_ref, rg_ref, lg_ref,
                  o_ref, ro_ref, m_ref, l_ref, acc_ref, state_ref):
    seq = q_ref.shape[0]
    n_q = seq // TQ
    ret_chunks = rq_ref.shape[0] // RET_CHUNK
    ret_consts = _retention_consts(lg_ref[0, 0:1, :])

    @pl.when(pl.program_id(1) % RET_SPLIT == 0)
    def _():
        state_ref[...] = jnp.zeros(state_ref.shape, F32)

    lam = (jnp.exp(jnp.sum(lq1_ref[...] * lk1_ref[...], axis=-1, keepdims=True))
           - jnp.exp(jnp.sum(lq2_ref[...] * lk2_ref[...], axis=-1, keepdims=True)) + lam_init)
    scale = DA_HEAD_DIM ** -0.5 * math.log2(math.e)
    lane = lax.broadcasted_iota(jnp.int32, (TQ, LANES), 1)
    row_i = lax.broadcasted_iota(jnp.int32, (2 * TQ, TQ), 0)
    col_i = lax.broadcasted_iota(jnp.int32, (2 * TQ, TQ), 1)
    causal = col_i <= jnp.where(row_i >= TQ, row_i - TQ, row_i)

    def step(qs, start, width, masked):
        k = k_ref[pl.ds(start, width), :]
        v = v_ref[pl.ds(start, width), :]
        s = lax.dot_general(qs, k, (((1,), (1,)), ((), ())), preferred_element_type=F32)
        if masked:
            s = jnp.where(causal, s, jnp.finfo(F32).min)
        chunks = [s[:, c * LANES:(c + 1) * LANES] for c in range(width // LANES)]
        m_old = m_ref[...]
        m_new = jnp.maximum(m_old, jnp.max(functools.reduce(jnp.maximum, chunks), axis=-1, keepdims=True))
        alpha = jnp.exp2(m_old - m_new)
        p = jnp.concatenate([jnp.exp2(c - m_new).astype(BF16) for c in chunks], axis=1)
        v_ext = jnp.concatenate([v, jnp.ones_like(v)], axis=1)
        pv = jnp.dot(p, v_ext, preferred_element_type=F32)
        acc_ref[...] = alpha * acc_ref[...] + pv[:, 0:LANES]
        l_ref[...] = alpha * l_ref[...] + pv[:, LANES:2 * LANES]
        m_ref[...] = m_new

    for qi in range(n_q):
        for c in range(qi * ret_chunks // n_q, (qi + 1) * ret_chunks // n_q):
            _retention_chunk(rq_ref, rk_ref, rv_ref, rg_ref, ro_ref, state_ref, ret_consts, c * RET_CHUNK)
        q = q_ref[qi * TQ:(qi + 1) * TQ, :]
        zero = jnp.zeros_like(q)
        qs = jnp.concatenate([jnp.where(lane < 64, q, zero), jnp.where(lane >= 64, q, zero)], axis=0)
        qs = (qs.astype(F32) * scale).astype(BF16)
        m_ref[...] = jnp.full(m_ref.shape, -jnp.inf, F32)
        l_ref[...] = jnp.zeros(l_ref.shape, F32)
        acc_ref[...] = jnp.zeros(acc_ref.shape, F32)

        def pair(j, carry, qs=qs):
            step(qs, pl.multiple_of(j * (2 * TQ), 2 * TQ), 2 * TQ, False)
            return carry

        lax.fori_loop(0, qi // 2, pair, 0, unroll=True)
        if qi % 2:
            step(qs, (qi - 1) * TQ, TQ, False)
        step(qs, qi * TQ, TQ, True)

        a = acc_ref[...] / l_ref[...]
        o = a[0:TQ, :] - lam * a[TQ:2 * TQ, :]
        o = o * lax.rsqrt(jnp.mean(o * o, axis=-1, keepdims=True) + LN_EPS)
        o = o * sw_ref[...] * (1.0 - lam_init)
        o_ref[qi * TQ:(qi + 1) * TQ, :] = o.astype(BF16)


def _mixer(qkv, lq1, lk1, lq2, lk2, subln_w, lam_init, batch, seq):
    n = qkv.shape[0]
    assert DA_HEADS == RET_HEADS * RET_SPLIT
    part = seq // RET_SPLIT
    qb0 = OFF_QDA // LANES
    kb0 = OFF_KDA // LANES
    vb0 = OFF_VDA // LANES
    rq0 = OFF_QR // RET_K_DIM
    rk0 = OFF_KR // RET_K_DIM
    rv0 = OFF_VR // RET_V_DIM
    rg0 = OFF_GR // RET_V_DIM
    lg = np.log(1.0 - np.power(2.0, -5.0 - np.arange(RET_HEADS, dtype=np.float64)))
    lg_tab = np.broadcast_to(lg[:, None, None], (RET_HEADS, SUBLANES, LANES)).astype(np.float32)
    vec = lambda a: a.reshape(1, -1).astype(F32)
    small = lambda w: pl.BlockSpec((1, w), lambda b, h: (0, 0))
    ret_rows = lambda b, h: b * RET_SPLIT + h % RET_SPLIT
    return pl.pallas_call(
        functools.partial(_mixer_kernel, lam_init),
        grid=(batch, DA_HEADS),
        in_specs=[pl.BlockSpec((seq, LANES), lambda b, h: (b, qb0 + h)),
                  pl.BlockSpec((seq, LANES), lambda b, h: (b, kb0 + h)),
                  pl.BlockSpec((seq, LANES), lambda b, h: (b, vb0 + h)),
                  small(64), small(64), small(64), small(64), small(128),
                  pl.BlockSpec((part, RET_K_DIM), lambda b, h: (ret_rows(b, h), rq0 + h // RET_SPLIT)),
                  pl.BlockSpec((part, RET_K_DIM), lambda b, h: (ret_rows(b, h), rk0 + h // RET_SPLIT)),
                  pl.BlockSpec((part, RET_V_DIM), lambda b, h: (ret_rows(b, h), rv0 + h // RET_SPLIT)),
                  pl.BlockSpec((part, RET_V_DIM), lambda b, h: (ret_rows(b, h), rg0 + h // RET_SPLIT)),
                  pl.BlockSpec((1, SUBLANES, LANES), lambda b, h: (h // RET_SPLIT, 0, 0))],
        out_specs=[pl.BlockSpec((seq, LANES), lambda b, h: (b, h)),
                   pl.BlockSpec((part, RET_V_DIM), lambda b, h: (ret_rows(b, h), h // RET_SPLIT))],
        out_shape=[jax.ShapeDtypeStruct((n, DA_WIDTH), BF16),
                   jax.ShapeDtypeStruct((n, RET_WIDTH), BF16)],
        scratch_shapes=[pltpu.VMEM((2 * TQ, LANES), F32), pltpu.VMEM((2 * TQ, LANES), F32),
                        pltpu.VMEM((2 * TQ, LANES), F32), pltpu.VMEM((RET_K_DIM, RET_V_DIM), F32)],
        compiler_params=_cparams(("arbitrary", "arbitrary")),
        name="attn_retention_mixer",
    )(qkv, qkv, qkv, vec(lq1), vec(lk1), vec(lq2), vec(lk2), vec(subln_w),
      qkv, qkv, qkv, qkv, jnp.asarray(lg_tab))


def _retention_consts(lg):
    C = RET_CHUNK
    lg_col = lg[:, 0:1]
    n_i = lax.broadcasted_iota(jnp.int32, (C, C), 0)
    m_i = lax.broadcasted_iota(jnp.int32, (C, C), 1)
    rel = (n_i - m_i).astype(F32)
    decay = jnp.where(rel >= 0, jnp.exp(rel * lg_col), 0.0)
    n_col = lax.broadcasted_iota(jnp.int32, (C, 1), 0).astype(F32)
    zeta = jnp.exp((C - 1.0 - n_col) * lg_col)
    xi = jnp.exp((n_col + 1.0) * lg_col)
    chunk_decay = jnp.exp(C * lg_col)
    return decay, zeta, xi, chunk_decay


def _retention_chunk(q_ref, k_ref, v_ref, g_ref, o_ref, state_ref, consts, start):
    C = RET_CHUNK
    decay, zeta, xi, chunk_decay = consts
    q = q_ref[pl.ds(start, C), :]
    k = k_ref[pl.ds(start, C), :]
    v = v_ref[pl.ds(start, C), :]
    g = g_ref[pl.ds(start, C), :].astype(F32)
    qk = lax.dot_general(q, k, (((1,), (1,)), ((), ())), preferred_element_type=F32)
    inner = (qk * decay).astype(BF16)
    inner_o = jnp.dot(inner, v, preferred_element_type=F32)
    state = state_ref[...]
    cross = jnp.dot(q, state.astype(BF16), preferred_element_type=F32)
    y = inner_o + cross * xi
    kz_t = (k.astype(F32) * zeta).T.astype(BF16)
    kv = jnp.dot(kz_t, v, preferred_element_type=F32)
    state_ref[...] = chunk_decay * state + kv
    mu = jnp.mean(y, axis=-1, keepdims=True)
    yc = y - mu
    var = jnp.mean(yc * yc, axis=-1, keepdims=True)
    yn = yc * lax.rsqrt(var + LN_EPS)
    o_ref[pl.ds(start, C), :] = (g * jax.nn.sigmoid(g) * yn).astype(BF16)


def _layer_norm(z, g, b):
    mu = jnp.mean(z, axis=-1, keepdims=True)
    zc = z - mu
    var = jnp.mean(zc * zc, axis=-1, keepdims=True)
    return zc * lax.rsqrt(var + LN_EPS) * g + b


def _dot_f32_weight(a, w):
    return lax.dot_general(a, w, (((1,), (0,)), ((), ())), preferred_element_type=F32)


def _split_bf16(a):
    hi = a.astype(BF16)
    lo = (a - hi.astype(F32)).astype(BF16)
    return hi, lo


def _out_proj_kernel(oda_ref, or_ref, x_ref, w_ref, g_ref, b_ref, wr_ref, br_ref,
                     x1_ref, route_ref, cnt_ref, run_ref):
    i = pl.program_id(0)

    @pl.when(i == 0)
    def _():
        run_ref[...] = jnp.zeros(run_ref.shape, F32)

    tm = x_ref.shape[0]
    run = run_ref[...]
    h = _dot_f32_weight(oda_ref[...], w_ref[0:DA_WIDTH, :])
    h = h + _dot_f32_weight(or_ref[...], w_ref[DA_WIDTH:DA_WIDTH + RET_WIDTH, :])
    x1 = _layer_norm(DEEPNORM_ALPHA * x_ref[...] + h, g_ref[...], b_ref[...])
    x1_ref[...] = x1

    xh, xl = _split_bf16(x1)
    wh, wl = _split_bf16(wr_ref[...])
    hh_hl = jnp.dot(xh, jnp.concatenate([wh, wl], axis=1), preferred_element_type=F32)
    logits = (hh_hl[:, 0:LANES] + hh_hl[:, LANES:2 * LANES]
              + jnp.dot(xl, wh, preferred_element_type=F32)) + br_ref[...]
    lane = lax.broadcasted_iota(jnp.int32, logits.shape, 1).astype(F32)
    neg = jnp.float32(-jnp.inf)
    big = jnp.float32(1 << 20)
    is_g = lane < N_GROUPS
    gl = jnp.where(is_g, logits, neg)
    gmax = jnp.max(gl, axis=-1, keepdims=True)
    g_idx = jnp.min(jnp.where(is_g & (gl == gmax), lane, big), axis=-1, keepdims=True)
    g_w = 1.0 / jnp.sum(jnp.where(is_g, jnp.exp(gl - gmax), 0.0), axis=-1, keepdims=True)
    e_lane = lane - N_GROUPS
    in_grp = (e_lane >= g_idx * EXPERTS_PER_GROUP) & (e_lane < (g_idx + 1.0) * EXPERTS_PER_GROUP)
    el = jnp.where(in_grp, logits, neg)
    v1 = jnp.max(el, axis=-1, keepdims=True)
    l1 = jnp.min(jnp.where(in_grp & (el == v1), lane, big), axis=-1, keepdims=True)
    el2 = jnp.where(lane == l1, neg, el)
    v2 = jnp.max(el2, axis=-1, keepdims=True)
    l2 = jnp.min(jnp.where(in_grp & (lane != l1) & (el2 == v2), lane, big), axis=-1, keepdims=True)
    t = jnp.exp(v2 - v1)
    fw1 = g_w / (1.0 + t)
    fw2 = g_w * t / (1.0 + t)
    e1 = l1 - N_GROUPS
    e2 = l2 - N_GROUPS

    oh1 = (lane == e1).astype(F32)
    oh2 = (lane == e2).astype(F32)
    cnt = oh1 + oh2
    r_i = lax.broadcasted_iota(jnp.int32, (tm, tm), 0)
    c_i = lax.broadcasted_iota(jnp.int32, (tm, tm), 1)
    tri = (c_i < r_i).astype(BF16)
    before = jnp.dot(tri, cnt.astype(BF16), preferred_element_type=F32) + run
    rank1 = jnp.sum(oh1 * before, axis=-1, keepdims=True)
    rank2 = jnp.sum(oh2 * before, axis=-1, keepdims=True)

    route = jnp.where(lane == 0, e1, 0.0)
    route = jnp.where(lane == 1, e2, route)
    route = jnp.where(lane == 2, rank1, route)
    route = jnp.where(lane == 3, rank2, route)
    route = jnp.where(lane == 4, fw1, route)
    route = jnp.where(lane == 5, fw2, route)
    route_ref[...] = route
    run = run + jnp.sum(cnt, axis=0, keepdims=True)
    run_ref[...] = run
    cnt_ref[...] = jnp.broadcast_to(run, cnt_ref.shape)


def _out_proj(oda, o_r, x2, w_out_b, ln_g, ln_b, w_router, b_router):
    n = x2.shape[0]
    tm = TM_OUT
    row = lambda w: pl.BlockSpec((1, w), lambda i: (0, 0))
    return pl.pallas_call(
        _out_proj_kernel,
        grid=(n // tm,),
        in_specs=[pl.BlockSpec((tm, DA_WIDTH), lambda i: (i, 0)),
                  pl.BlockSpec((tm, RET_WIDTH), lambda i: (i, 0)),
                  pl.BlockSpec((tm, D_MODEL), lambda i: (i, 0)),
                  pl.BlockSpec((D_MODEL, D_MODEL), lambda i: (0, 0), pipeline_mode=pl.Buffered(1)),
                  row(D_MODEL), row(D_MODEL),
                  pl.BlockSpec((D_MODEL, LANES), lambda i: (0, 0)),
                  row(LANES)],
        out_specs=[pl.BlockSpec((tm, D_MODEL), lambda i: (i, 0)),
                   pl.BlockSpec((tm, LANES), lambda i: (i, 0)),
                   pl.BlockSpec((SUBLANES, LANES), lambda i: (0, 0))],
        out_shape=[jax.ShapeDtypeStruct((n, D_MODEL), F32),
                   jax.ShapeDtypeStruct((n, LANES), F32),
                   jax.ShapeDtypeStruct((SUBLANES, LANES), F32)],
        scratch_shapes=[pltpu.VMEM((1, LANES), F32)],
        compiler_params=_cparams(("arbitrary",), VMEM_LIMIT),
        name="out_proj_ln1_router",
    )(oda, o_r, x2, w_out_b, ln_g, ln_b, w_router, b_router)


def _slot_kernel(route_ref, cnt_ref, pos_ref):
    route = route_ref[...]
    lane = lax.broadcasted_iota(jnp.int32, (1, LANES), 1)
    cnt = jnp.where(lane < N_EXPERTS, cnt_ref[0:1, :], 0.0)
    padded = jnp.floor((cnt + (TM_EXP - 1)) * (1.0 / TM_EXP)) * TM_EXP
    ends = padded
    for sh in (1, 2, 4, 8, 16):
        ends = ends + jnp.where(lane >= sh, pltpu.roll(ends, sh, 1), 0.0)
    offs = ends - padded
    lane_f = lax.broadcasted_iota(jnp.int32, route.shape, 1).astype(F32)
    off1 = jnp.sum(jnp.where(lane_f == route[:, 0:1], offs, 0.0), axis=-1, keepdims=True)
    off2 = jnp.sum(jnp.where(lane_f == route[:, 1:2], offs, 0.0), axis=-1, keepdims=True)
    slots = jnp.where(lane_f == 0.0, off1 + route[:, 2:3], 0.0)
    slots = jnp.where(lane_f == 1.0, off2 + route[:, 3:4], slots)
    pos_ref[...] = slots.T[0:SUBLANES, :].astype(jnp.int32)


def _slots(route, counts):
    n = route.shape[0]
    tm = 1024
    return pl.pallas_call(
        _slot_kernel,
        grid=(n // tm,),
        in_specs=[pl.BlockSpec((tm, LANES), lambda i: (i, 0)),
                  pl.BlockSpec((SUBLANES, LANES), lambda i: (0, 0))],
        out_specs=pl.BlockSpec((SUBLANES, tm), lambda i: (0, i)),
        out_shape=jax.ShapeDtypeStruct((SUBLANES, n), jnp.int32),
        compiler_params=_cparams(("arbitrary",)),
        name="moe_slots",
    )(route, counts)


def _inverse_kernel(pos1_ref, pos2_ref, fill_ref, inv_ref, sem):
    i = pl.program_id(0)
    tm = pos1_ref.shape[0]

    @pl.when(i == 0)
    def _():
        fill = pltpu.make_async_copy(fill_ref, inv_ref, sem)
        fill.start()
        fill.wait()

    def scatter(r, carry):
        tok = i * tm + r
        inv_ref[pos1_ref[r]] = tok
        inv_ref[pos2_ref[r]] = tok
        return carry

    lax.fori_loop(0, tm, scatter, 0, unroll=8)


def _inverse_map(pos1, pos2, p_rows):
    n = pos1.shape[0]
    tm = 1024
    smem = pl.BlockSpec((tm,), lambda i: (i,), memory_space=pltpu.SMEM)
    fill = jnp.asarray(np.arange(p_rows, dtype=np.int32) % n)
    return pl.pallas_call(
        _inverse_kernel,
        grid=(n // tm,),
        in_specs=[smem, smem, pl.BlockSpec(memory_space=pl.ANY)],
        out_specs=pl.BlockSpec(memory_space=pltpu.SMEM),
        out_shape=jax.ShapeDtypeStruct((p_rows,), jnp.int32),
        scratch_shapes=[pltpu.SemaphoreType.DMA],
        compiler_params=_cparams(("arbitrary",)),
        name="moe_inverse_map",
    )(pos1, pos2, fill)


def _expert_ffn_kernel(t_max, tb_ref, tv_ref, inv_ref, x1_ref, wg_ref, wu_ref, wd_ref, ys_ref,
                       xbuf_ref, ybuf_ref, zbuf_ref, gsem, osem, zsem):
    e = pl.program_id(0)
    tm = TM_EXP
    g_first = tb_ref[e]
    g_end = tb_ref[e + 1]
    n_total = tb_ref[N_EXPERTS]

    def gathered_rows(g):
        return ((tv_ref[g] + (GATHER_GROUP - 1)) // GATHER_GROUP) * GATHER_GROUP

    def gather_rows(g, buf_slot):
        def issue(q, carry):
            for k in range(GATHER_GROUP):
                r = q * GATHER_GROUP + k
                pltpu.make_async_copy(x1_ref.at[pl.ds(inv_ref[g * tm + r], 1), :],
                                      xbuf_ref.at[buf_slot, pl.ds(r, 1), :], gsem.at[buf_slot]).start()
            return carry
        lax.fori_loop(0, gathered_rows(g) // GATHER_GROUP, issue, 0)

    def wait_rows(g, buf_slot):
        rows = gathered_rows(g)
        k = GATHER_GROUP
        while k <= tm:
            @pl.when((rows & k) != 0)
            def _(k=k):
                pltpu.make_async_copy(x1_ref.at[pl.ds(0, k), :], xbuf_ref.at[buf_slot, pl.ds(0, k), :],
                                      gsem.at[buf_slot]).wait()
            k *= 2

    def tile_out(g):
        return ys_ref.at[pl.ds(pl.multiple_of(g * tm, tm), tm), :]

    @pl.when(e == 0)
    def _():
        xbuf_ref[...] = jnp.zeros(xbuf_ref.shape, F32)
        gather_rows(0, 0)

        @pl.when(n_total > 1)
        def _():
            gather_rows(1, 1)
        zbuf_ref[...] = jnp.zeros(zbuf_ref.shape, F32)

        def fill(g, carry):
            pltpu.make_async_copy(zbuf_ref, tile_out(g), zsem).start()
            return carry
        lax.fori_loop(n_total, t_max, fill, 0)

    def tile(g, carry):
        slot = g % 2
        xslot = g % GATHER_SLOTS

        @pl.when(g + (GATHER_SLOTS - 1) < n_total)
        def _():
            gather_rows(g + (GATHER_SLOTS - 1), (g + (GATHER_SLOTS - 1)) % GATHER_SLOTS)

        wait_rows(g, xslot)

        @pl.when(g >= 2)
        def _():
            pltpu.make_async_copy(ybuf_ref.at[slot], tile_out(g), osem.at[slot]).wait()

        def ffn(rows):
            x = xbuf_ref[xslot, 0:rows, :].astype(BF16)
            gate = _dot_f32_weight(x, wg_ref[0])
            up = _dot_f32_weight(x, wu_ref[0])
            hmid = (gate * jax.nn.sigmoid(gate) * up).astype(BF16)
            ybuf_ref[slot, 0:rows, :] = _dot_f32_weight(hmid, wd_ref[0])

        valid = tv_ref[g]
        lower = 0
        for rows in (tm // 2, tm):
            @pl.when((valid > lower) & (valid <= rows))
            def _(rows=rows):
                ffn(rows)
                if rows < tm:
                    ybuf_ref[slot, rows:tm, :] = jnp.zeros((tm - rows, D_MODEL), F32)
            lower = rows

        pltpu.make_async_copy(ybuf_ref.at[slot], tile_out(g), osem.at[slot]).start()
        return carry

    lax.fori_loop(g_first, g_end, tile, 0)

    @pl.when(e == pl.num_programs(0) - 1)
    def _():
        pltpu.make_async_copy(ybuf_ref.at[0], tile_out(0), osem.at[0]).wait()
        pltpu.make_async_copy(ybuf_ref.at[1], tile_out(0), osem.at[1]).wait()

        def drain(g, carry):
            pltpu.make_async_copy(zbuf_ref, tile_out(g), zsem).wait()
            return carry
        lax.fori_loop(n_total, t_max, drain, 0)


def _expert_ffn(tile_base, tile_valid, inv, x1, w_gate, w_up, w_down, t_max):
    tm = TM_EXP
    assert 2 * x1.shape[0] >= 2 * tm
    grid_spec = pltpu.PrefetchScalarGridSpec(
        num_scalar_prefetch=3,
        grid=(N_EXPERTS,),
        in_specs=[pl.BlockSpec(memory_space=pl.ANY),
                  pl.BlockSpec((1, D_MODEL, EXPERT_FF), lambda e, tb, tv, inv: (e, 0, 0)),
                  pl.BlockSpec((1, D_MODEL, EXPERT_FF), lambda e, tb, tv, inv: (e, 0, 0)),
                  pl.BlockSpec((1, EXPERT_FF, D_MODEL), lambda e, tb, tv, inv: (e, 0, 0))],
        out_specs=pl.BlockSpec(memory_space=pl.ANY),
        scratch_shapes=[pltpu.VMEM((GATHER_SLOTS, tm, D_MODEL), F32),
                        pltpu.VMEM((2, tm, D_MODEL), F32),
                        pltpu.VMEM((tm, D_MODEL), F32),
                        pltpu.SemaphoreType.DMA((GATHER_SLOTS,)),
                        pltpu.SemaphoreType.DMA((2,)),
                        pltpu.SemaphoreType.DMA],
    )
    return pl.pallas_call(
        functools.partial(_expert_ffn_kernel, t_max),
        grid_spec=grid_spec,
        out_shape=jax.ShapeDtypeStruct((t_max * tm, D_MODEL), F32),
        compiler_params=_cparams(("arbitrary",), VMEM_LIMIT),
        name="moe_expert_ffn",
    )(tile_base, tile_valid, inv, x1, w_gate, w_up, w_down)


def _combine_kernel(pos1_ref, pos2_ref, pos1n_ref, pos2n_ref, x1_ref, route_ref, ys_ref, p_ref, wpg_ref,
                    wpp_ref, g_ref, b_ref, o_ref, y1_ref, y2_ref, sem):
    i = pl.program_id(0)
    tm = route_ref.shape[0]
    slot = i % 2

    def gather_rows(p1_ref, p2_ref, buf_slot):
        def issue(r, carry):
            pltpu.make_async_copy(ys_ref.at[pl.ds(p1_ref[r], 1), :],
                                  y1_ref.at[buf_slot, pl.ds(r, 1), :], sem.at[buf_slot]).start()
            pltpu.make_async_copy(ys_ref.at[pl.ds(p2_ref[r], 1), :],
                                  y2_ref.at[buf_slot, pl.ds(r, 1), :], sem.at[buf_slot]).start()
            return carry
        lax.fori_loop(0, tm, issue, 0, unroll=8)

    @pl.when(i == 0)
    def _():
        gather_rows(pos1_ref, pos2_ref, 0)

    @pl.when(i + 1 < pl.num_programs(0))
    def _():
        gather_rows(pos1n_ref, pos2n_ref, 1 - slot)

    pltpu.make_async_copy(ys_ref.at[pl.ds(0, tm), :], y1_ref.at[slot], sem.at[slot]).wait()
    pltpu.make_async_copy(ys_ref.at[pl.ds(0, tm), :], y2_ref.at[slot], sem.at[slot]).wait()

    for s in range(tm // SUB_COMB):
        rows = pl.ds(s * SUB_COMB, SUB_COMB)
        route = route_ref[rows, :]
        m = route[:, 4:5] * y1_ref[slot, rows, :] + route[:, 5:6] * y2_ref[slot, rows, :]
        x2 = _layer_norm(DEEPNORM_ALPHA * x1_ref[rows, :] + m, g_ref[...], b_ref[...])
        gate = jax.nn.sigmoid(_dot_f32_weight(x2.astype(BF16), wpg_ref[...]))
        proj = _dot_f32_weight(p_ref[rows, :].astype(BF16), wpp_ref[...])
        o_ref[rows, :] = x2 + gate * proj


def _combine(pos1, pos2, x1, route, ys, p2, w_ple_gate_b, w_ple_proj_b, ln_g, ln_b):
    n = route.shape[0]
    tm = TM_ROW
    smem = pl.BlockSpec((tm,), lambda i: (i,), memory_space=pltpu.SMEM)
    smem_next = pl.BlockSpec((tm,), lambda i: (jnp.minimum(i + 1, n // tm - 1),), memory_space=pltpu.SMEM)
    row = lambda w: pl.BlockSpec((1, w), lambda i: (0, 0))
    return pl.pallas_call(
        _combine_kernel,
        grid=(n // tm,),
        in_specs=[smem, smem, smem_next, smem_next,
                  pl.BlockSpec((tm, D_MODEL), lambda i: (i, 0)),
                  pl.BlockSpec((tm, LANES), lambda i: (i, 0)),
                  pl.BlockSpec(memory_space=pl.ANY),
                  pl.BlockSpec((tm, PLE_DIM), lambda i: (i, 0)),
                  pl.BlockSpec((D_MODEL, D_MODEL), lambda i: (0, 0), pipeline_mode=pl.Buffered(1)),
                  pl.BlockSpec((PLE_DIM, D_MODEL), lambda i: (0, 0), pipeline_mode=pl.Buffered(1)),
                  row(D_MODEL), row(D_MODEL)],
        out_specs=pl.BlockSpec((tm, D_MODEL), lambda i: (i, 0)),
        out_shape=jax.ShapeDtypeStruct((n, D_MODEL), F32),
        scratch_shapes=[pltpu.VMEM((2, tm, D_MODEL), F32), pltpu.VMEM((2, tm, D_MODEL), F32),
                        pltpu.SemaphoreType.DMA((2,))],
        compiler_params=_cparams(("arbitrary",), VMEM_LIMIT),
        name="moe_combine_ln2_ple",
    )(pos1, pos2, pos1, pos2, x1, route, ys, p2, w_ple_gate_b, w_ple_proj_b, ln_g, ln_b)


def _tile_plan(counts, t_max):
    tm = TM_EXP
    cnt = counts[0, :N_EXPERTS].astype(jnp.int32)
    tiles = (cnt + tm - 1) // tm
    tile_base = jnp.concatenate([jnp.zeros((1,), jnp.int32), jnp.cumsum(tiles).astype(jnp.int32)])
    g = jnp.arange(t_max, dtype=jnp.int32)
    in_e = (g[:, None] >= tile_base[None, :-1]) & (g[:, None] < tile_base[None, 1:])
    rows = cnt[None, :] - (g[:, None] - tile_base[None, :-1]) * tm
    tile_valid = jnp.sum(jnp.where(in_e, jnp.clip(rows, 0, tm), 0), axis=1).astype(jnp.int32)
    return tile_base, tile_valid


def kernel(x, p, positions, w_in, w_out, da_lambda_q1, da_lambda_k1, da_lambda_q2, da_lambda_k2, da_subln_w, ln1_g, ln1_b, w_router_group, b_router_group, w_router_expert, b_router_expert, w_exp_gate, w_exp_up, w_exp_down, ln2_g, ln2_b, w_ple_gate, w_ple_proj):
    batch, seq, d = x.shape
    n = batch * seq
    assert d == D_MODEL and w_in.shape[0] == DEPTH == 1
    l = 0
    lam_init = 0.8 - 0.6 * math.exp(-0.3 * l)
    x2 = x.reshape(n, d)

    qkv = _in_proj(x2, w_in[l], positions)
    oda, o_r = _mixer(qkv, da_lambda_q1[l], da_lambda_k1[l], da_lambda_q2[l], da_lambda_k2[l],
                      da_subln_w[l], lam_init, batch, seq)

    pad = LANES - N_GROUPS - N_EXPERTS
    w_router = jnp.concatenate([w_router_group[l], w_router_expert[l], jnp.zeros((d, pad), F32)], axis=1)
    b_router = jnp.concatenate([b_router_group[l], b_router_expert[l], jnp.zeros((pad,), F32)]).reshape(1, LANES)
    x1, route, counts = _out_proj(oda, o_r, x2, w_out[l],
                                  ln1_g[l].reshape(1, d), ln1_b[l].reshape(1, d), w_router, b_router)

    t_max = (2 * n) // TM_EXP + N_EXPERTS
    tile_base, tile_valid = _tile_plan(counts, t_max)
    pos = _slots(route, counts)
    pos1, pos2 = pos[0], pos[1]
    inv = _inverse_map(pos1, pos2, t_max * TM_EXP)
    ys = _expert_ffn(tile_base, tile_valid, inv, x1, w_exp_gate[l], w_exp_up[l], w_exp_down[l], t_max)
    out = _combine(pos1, pos2, x1, route, ys, p[l].reshape(n, PLE_DIM),
                   w_ple_gate[l].astype(BF16), w_ple_proj[l],
                   ln2_g[l].reshape(1, d), ln2_b[l].reshape(1, d))
    return out.reshape(batch, seq, d)
```

```python
import functools
import math

import numpy as np
import jax
import jax.numpy as jnp
from jax import lax
from jax.experimental import pallas as pl
from jax.experimental.pallas import tpu as pltpu

D_MODEL = 2048
DA_HEADS = 8
DA_HEAD_DIM = 64
DA_V_DIM = 128
RET_HEADS = 4
RET_K_DIM = 128
RET_V_DIM = 256
RET_CHUNK = 128
RET_SPLIT = DA_HEADS // RET_HEADS
ROPE_THETA = 10000.0
PLE_DIM = 256
N_GROUPS = 4
EXPERTS_PER_GROUP = 8
N_EXPERTS = 32
EXPERT_FF = 512
DEPTH = 1
DEEPNORM_ALPHA = (2 * DEPTH) ** 0.25
LN_EPS = 1e-5
IN_WIDTH = 6144
DA_WIDTH = 1024
RET_WIDTH = 1024

OFF_QDA, OFF_KDA, OFF_VDA, OFF_QR, OFF_KR, OFF_VR, OFF_GR = 0, 1024, 2048, 3072, 3584, 4096, 5120

LANES = 128
SUBLANES = 8
VMEM_LIMIT = 56 * 1024 * 1024

BF16 = jnp.bfloat16
F32 = jnp.float32

TM_PROJ = 512
TN_PROJ = 512
W_CHUNK = 256
TQ = 256
TM_OUT = 512
SUB_COMB = 256
TM_ROW = 512
TM_EXP = 256
GATHER_GROUP = SUBLANES
GATHER_SLOTS = 3


def _cparams(sem, vmem=None):
    return pltpu.CompilerParams(dimension_semantics=sem, vmem_limit_bytes=vmem)


def _rope_tables(pos, inv, sgn):
    ang = pos.astype(F32) * inv
    c = jnp.cos(ang)
    s = jnp.sin(ang)
    low = lax.broadcasted_iota(jnp.int32, ang.shape, 1) < 64
    cr = pltpu.roll(c, 64, 1)
    sr = pltpu.roll(s, 64, 1)
    cos64, sin64 = jnp.where(low, cr, c), jnp.where(low, sr, s) * sgn[0:1, :]
    cos128, sin128 = jnp.where(low, c, cr), jnp.where(low, s, sr) * sgn[1:2, :]
    return cos64, sin64, cos128, sin128


def _rope_constants():
    i128 = np.power(ROPE_THETA, -np.arange(0, 128, 2, dtype=np.float64) / 128)
    i64 = np.power(ROPE_THETA, -np.arange(0, 64, 2, dtype=np.float64) / 64)
    inv = np.concatenate([i128, i64, i64]).astype(np.float32)[None, :]
    lane = np.arange(128)
    sgn = np.stack([np.where(lane % 64 < 32, -1.0, 1.0), np.where(lane < 64, -1.0, 1.0)]).astype(np.float32)
    sgn = np.concatenate([sgn, np.zeros((6, 128), np.float32)])
    return jnp.asarray(inv), jnp.asarray(sgn)


def _rope_cols(acc, cos, sin, half, scale):
    outs = []
    lane = lax.broadcasted_iota(jnp.int32, (acc.shape[0], LANES), 1)
    for c in range(acc.shape[1] // LANES):
        t = acc[:, c * LANES:(c + 1) * LANES]
        if half == 64:
            partner = pltpu.roll(t, 64, 1)
        else:
            partner = jnp.where((lane & 32) == 0, pltpu.roll(t, 96, 1), pltpu.roll(t, 32, 1))
        o = t * cos + partner * sin
        if scale != 1.0:
            o = o * scale
        outs.append(o)
    return jnp.concatenate(outs, axis=1)


def _in_proj_kernel(x_ref, w_hbm_ref, pos_ref, inv_ref, sgn_ref, o_ref, w_ref, stage_ref, sem):
    @pl.when(pl.program_id(0) == 0)
    def _():
        n_chunks = IN_WIDTH // W_CHUNK

        def fetch(c):
            return pltpu.make_async_copy(w_hbm_ref.at[:, c * W_CHUNK:(c + 1) * W_CHUNK],
                                         stage_ref.at[c % 2], sem.at[c % 2])
        fetch(0).start()
        for c in range(n_chunks):
            if c + 1 < n_chunks:
                fetch(c + 1).start()
            fetch(c).wait()
            w_ref[:, c * W_CHUNK:(c + 1) * W_CHUNK] = stage_ref[c % 2].astype(BF16)

    xb = x_ref[...].astype(BF16)
    cos64, sin64, cos128, sin128 = _rope_tables(pos_ref[...], inv_ref[...], sgn_ref[...])
    n_tiles = IN_WIDTH // TN_PROJ
    plain = [j for j in range(n_tiles) if OFF_VDA <= j * TN_PROJ < OFF_QR or j * TN_PROJ >= OFF_VR]
    rotary = [j for j in range(n_tiles) if j not in plain]
    for j in plain[:-2] + rotary + plain[-2:]:
        lo = j * TN_PROJ
        acc = jnp.dot(xb, w_ref[:, lo:lo + TN_PROJ], preferred_element_type=F32)
        if lo < OFF_VDA:
            acc = _rope_cols(acc, cos64, sin64, 32, 1.0)
        elif OFF_QR <= lo < OFF_KR:
            acc = _rope_cols(acc, cos128, sin128, 64, 1.0)
        elif OFF_KR <= lo < OFF_VR:
            acc = _rope_cols(acc, cos128, sin128, 64, RET_K_DIM ** -0.5)
        o_ref[:, lo:lo + TN_PROJ] = acc.astype(BF16)


def _in_proj(x2, w_in, positions):
    n = x2.shape[0]
    inv, sgn = _rope_constants()
    return pl.pallas_call(
        _in_proj_kernel,
        grid=(n // TM_PROJ,),
        in_specs=[pl.BlockSpec((TM_PROJ, D_MODEL), lambda i: (i, 0)),
                  pl.BlockSpec(memory_space=pl.ANY),
                  pl.BlockSpec((TM_PROJ, 1), lambda i: (i, 0)),
                  pl.BlockSpec((1, LANES), lambda i: (0, 0)),
                  pl.BlockSpec((SUBLANES, LANES), lambda i: (0, 0))],
        out_specs=pl.BlockSpec((TM_PROJ, IN_WIDTH), lambda i: (i, 0)),
        out_shape=jax.ShapeDtypeStruct((n, IN_WIDTH), BF16),
        scratch_shapes=[pltpu.VMEM((D_MODEL, IN_WIDTH), BF16),
                        pltpu.VMEM((2, D_MODEL, W_CHUNK), F32),
                        pltpu.SemaphoreType.DMA((2,))],
        compiler_params=_cparams(("arbitrary",), VMEM_LIMIT),
        name="in_proj",
    )(x2, w_in, positions.reshape(n, 1), inv, sgn)


def _mixer_kernel(lam_init, q_ref, k_ref, v_ref, lq1_ref, lk1_ref, lq2_ref, lk2_ref, sw_ref,
                  rq_ref, rk_ref, rv_ref, rg_ref, lg_ref,
                  o_ref, ro_ref, m_ref, l_ref, acc_ref, state_ref):
    seq = q_ref.shape[0]
    n_q = seq // TQ
    ret_chunks = rq_ref.shape[0] // RET_CHUNK
    ret_consts = _retention_consts(lg_ref[0, 0:1, :])

    @pl.when(pl.program_id(1) % RET_SPLIT == 0)
    def _():
        state_ref[...] = jnp.zeros(state_ref.shape, F32)

    lam = (jnp.exp(jnp.sum(lq1_ref[...] * lk1_ref[...], axis=-1, keepdims=True))
           - jnp.exp(jnp.sum(lq2_ref[...] * lk2_ref[...], axis=-1, keepdims=True)) + lam_init)
    scale = DA_HEAD_DIM ** -0.5 * math.log2(math.e)
    lane = lax.broadcasted_iota(jnp.int32, (TQ, LANES), 1)
    row_i = lax.broadcasted_iota(jnp.int32, (2 * TQ, TQ), 0)
    col_i = lax.broadcasted_iota(jnp.int32, (2 * TQ, TQ), 1)
    causal = col_i <= jnp.where(row_i >= TQ, row_i - TQ, row_i)

    def step(qs, start, width, masked):
        k = k_ref[pl.ds(start, width), :]
        v = v_ref[pl.ds(start, width), :]
        s = lax.dot_general(qs, k, (((1,), (1,)), ((), ())), preferred_element_type=F32)
        if masked:
            s = jnp.where(causal, s, jnp.finfo(F32).min)
        chunks = [s[:, c * LANES:(c + 1) * LANES] for c in range(width // LANES)]
        m_old = m_ref[...]
        m_new = jnp.maximum(m_old, jnp.max(functools.reduce(jnp.maximum, chunks), axis=-1, keepdims=True))
        alpha = jnp.exp2(m_old - m_new)
        p = jnp.concatenate([jnp.exp2(c - m_new).astype(BF16) for c in chunks], axis=1)
        v_ext = jnp.concatenate([v, jnp.ones_like(v)], axis=1)
        pv = jnp.dot(p, v_ext, preferred_element_type=F32)
        acc_ref[...] = alpha * acc_ref[...] + pv[:, 0:LANES]
        l_ref[...] = alpha * l_ref[...] + pv[:, LANES:2 * LANES]
        m_ref[...] = m_new

    for qi in range(n_q):
        for c in range(qi * ret_chunks // n_q, (qi + 1) * ret_chunks // n_q):
            _retention_chunk(rq_ref, rk_ref, rv_ref, rg_ref, ro_ref, state_ref, ret_consts, c * RET_CHUNK)
        q = q_ref[qi * TQ:(qi + 1) * TQ, :]
        zero = jnp.zeros_like(q)
        qs = jnp.concatenate([jnp.where(lane < 64, q, zero), jnp.where(lane >= 64, q, zero)], axis=0)
        qs = (qs.astype(F32) * scale).astype(BF16)
        m_ref[...] = jnp.full(m_ref.shape, -jnp.inf, F32)
        l_ref[...] = jnp.zeros(l_ref.shape, F32)
        acc_ref[...] = jnp.zeros(acc_ref.shape, F32)

        def pair(j, carry, qs=qs):
            step(qs, pl.multiple_of(j * (2 * TQ), 2 * TQ), 2 * TQ, False)
            return carry

        lax.fori_loop(0, qi // 2, pair, 0, unroll=True)
        if qi % 2:
            step(qs, (qi - 1) * TQ, TQ, False)
        step(qs, qi * TQ, TQ, True)

        a = acc_ref[...] / l_ref[...]
        o = a[0:TQ, :] - lam * a[TQ:2 * TQ, :]
        o = o * lax.rsqrt(jnp.mean(o * o, axis=-1, keepdims=True) + LN_EPS)
        o = o * sw_ref[...] * (1.0 - lam_init)
        o_ref[qi * TQ:(qi + 1) * TQ, :] = o.astype(BF16)


def _mixer(qkv, lq1, lk1, lq2, lk2, subln_w, lam_init, batch, seq):
    n = qkv.shape[0]
    assert DA_HEADS == RET_HEADS * RET_SPLIT
    part = seq // RET_SPLIT
    qb0 = OFF_QDA // LANES
    kb0 = OFF_KDA // LANES
    vb0 = OFF_VDA // LANES
    rq0 = OFF_QR // RET_K_DIM
    rk0 = OFF_KR // RET_K_DIM
    rv0 = OFF_VR // RET_V_DIM
    rg0 = OFF_GR // RET_V_DIM
    lg = np.log(1.0 - np.power(2.0, -5.0 - np.arange(RET_HEADS, dtype=np.float64)))
    lg_tab = np.broadcast_to(lg[:, None, None], (RET_HEADS, SUBLANES, LANES)).astype(np.float32)
    vec = lambda a: a.reshape(1, -1).astype(F32)
    small = lambda w: pl.BlockSpec((1, w), lambda b, h: (0, 0))
    ret_rows = lambda b, h: b * RET_SPLIT + h % RET_SPLIT
    return pl.pallas_call(
        functools.partial(_mixer_kernel, lam_init),
        grid=(batch, DA_HEADS),
        in_specs=[pl.BlockSpec((seq, LANES), lambda b, h: (b, qb0 + h)),
                  pl.BlockSpec((seq, LANES), lambda b, h: (b, kb0 + h)),
                  pl.BlockSpec((seq, LANES), lambda b, h: (b, vb0 + h)),
                  small(64), small(64), small(64), small(64), small(128),
                  pl.BlockSpec((part, RET_K_DIM), lambda b, h: (ret_rows(b, h), rq0 + h // RET_SPLIT)),
                  pl.BlockSpec((part, RET_K_DIM), lambda b, h: (ret_rows(b, h), rk0 + h // RET_SPLIT)),
                  pl.BlockSpec((part, RET_V_DIM), lambda b, h: (ret_rows(b, h), rv0 + h // RET_SPLIT)),
                  pl.BlockSpec((part, RET_V_DIM), lambda b, h: (ret_rows(b, h), rg0 + h // RET_SPLIT)),
                  pl.BlockSpec((1, SUBLANES, LANES), lambda b, h: (h // RET_SPLIT, 0, 0))],
        out_specs=[pl.BlockSpec((seq, LANES), lambda b, h: (b, h)),
                   pl.BlockSpec((part, RET_V_DIM), lambda b, h: (ret_rows(b, h), h // RET_SPLIT))],
        out_shape=[jax.ShapeDtypeStruct((n, DA_WIDTH), BF16),
                   jax.ShapeDtypeStruct((n, RET_WIDTH), BF16)],
        scratch_shapes=[pltpu.VMEM((2 * TQ, LANES), F32), pltpu.VMEM((2 * TQ, LANES), F32),
                        pltpu.VMEM((2 * TQ, LANES), F32), pltpu.VMEM((RET_K_DIM, RET_V_DIM), F32)],
        compiler_params=_cparams(("arbitrary", "arbitrary")),
        name="attn_retention_mixer",
    )(qkv, qkv, qkv, vec(lq1), vec(lk1), vec(lq2), vec(lk2), vec(subln_w),
      qkv, qkv, qkv, qkv, jnp.asarray(lg_tab))


def _retention_consts(lg):
    C = RET_CHUNK
    lg_col = lg[:, 0:1]
    n_i = lax.broadcasted_iota(jnp.int32, (C, C), 0)
    m_i = lax.broadcasted_iota(jnp.int32, (C, C), 1)
    rel = (n_i - m_i).astype(F32)
    decay = jnp.where(rel >= 0, jnp.exp(rel * lg_col), 0.0)
    n_col = lax.broadcasted_iota(jnp.int32, (C, 1), 0).astype(F32)
    zeta = jnp.exp((C - 1.0 - n_col) * lg_col)
    xi = jnp.exp((n_col + 1.0) * lg_col)
    chunk_decay = jnp.exp(C * lg_col)
    return decay, zeta, xi, chunk_decay


def _retention_chunk(q_ref, k_ref, v_ref, g_ref, o_ref, state_ref, consts, start):
    C = RET_CHUNK
    decay, zeta, xi, chunk_decay = consts
    q = q_ref[pl.ds(start, C), :]
    k = k_ref[pl.ds(start, C), :]
    v = v_ref[pl.ds(start, C), :]
    g = g_ref[pl.ds(start, C), :].astype(F32)
    qk = lax.dot_general(q, k, (((1,), (1,)), ((), ())), preferred_element_type=F32)
    inner = (qk * decay).astype(BF16)
    inner_o = jnp.dot(inner, v, preferred_element_type=F32)
    state = state_ref[...]
    cross = jnp.dot(q, state.astype(BF16), preferred_element_type=F32)
    y = inner_o + cross * xi
    kz_t = (k.astype(F32) * zeta).T.astype(BF16)
    kv = jnp.dot(kz_t, v, preferred_element_type=F32)
    state_ref[...] = chunk_decay * state + kv
    mu = jnp.mean(y, axis=-1, keepdims=True)
    yc = y - mu
    var = jnp.mean(yc * yc, axis=-1, keepdims=True)
    yn = yc * lax.rsqrt(var + LN_EPS)
    o_ref[pl.ds(start, C), :] = (g * jax.nn.sigmoid(g) * yn).astype(BF16)


def _layer_norm(z, g, b):
    mu = jnp.mean(z, axis=-1, keepdims=True)
    zc = z - mu
    var = jnp.mean(zc * zc, axis=-1, keepdims=True)
    return zc * lax.rsqrt(var + LN_EPS) * g + b


def _dot_f32_weight(a, w):
    return lax.dot_general(a, w, (((1,), (0,)), ((), ())), preferred_element_type=F32)


def _split_bf16(a):
    hi = a.astype(BF16)
    lo = (a - hi.astype(F32)).astype(BF16)
    return hi, lo


def _out_proj_kernel(oda_ref, or_ref, x_ref, w_ref, g_ref, b_ref, wr_ref, br_ref,
                     x1_ref, route_ref, cnt_ref, run_ref):
    i = pl.program_id(0)

    @pl.when(i == 0)
    def _():
        run_ref[...] = jnp.zeros(run_ref.shape, F32)

    tm = x_ref.shape[0]
    run = run_ref[...]
    h = _dot_f32_weight(oda_ref[...], w_ref[0:DA_WIDTH, :])
    h = h + _dot_f32_weight(or_ref[...], w_ref[DA_WIDTH:DA_WIDTH + RET_WIDTH, :])
    x1 = _layer_norm(DEEPNORM_ALPHA * x_ref[...] + h, g_ref[...], b_ref[...])
    x1_ref[...] = x1

    xh, xl = _split_bf16(x1)
    wh, wl = _split_bf16(wr_ref[...])
    hh_hl = jnp.dot(xh, jnp.concatenate([wh, wl], axis=1), preferred_element_type=F32)
    logits = (hh_hl[:, 0:LANES] + hh_hl[:, LANES:2 * LANES]
              + jnp.dot(xl, wh, preferred_element_type=F32)) + br_ref[...]
    lane = lax.broadcasted_iota(jnp.int32, logits.shape, 1).astype(F32)
    neg = jnp.float32(-jnp.inf)
    big = jnp.float32(1 << 20)
    is_g = lane < N_GROUPS
    gl = jnp.where(is_g, logits, neg)
    gmax = jnp.max(gl, axis=-1, keepdims=True)
    g_idx = jnp.min(jnp.where(is_g & (gl == gmax), lane, big), axis=-1, keepdims=True)
    g_w = 1.0 / jnp.sum(jnp.where(is_g, jnp.exp(gl - gmax), 0.0), axis=-1, keepdims=True)
    e_lane = lane - N_GROUPS
    in_grp = (e_lane >= g_idx * EXPERTS_PER_GROUP) & (e_lane < (g_idx + 1.0) * EXPERTS_PER_GROUP)
    el = jnp.where(in_grp, logits, neg)
    v1 = jnp.max(el, axis=-1, keepdims=True)
    l1 = jnp.min(jnp.where(in_grp & (el == v1), lane, big), axis=-1, keepdims=True)
    el2 = jnp.where(lane == l1, neg, el)
    v2 = jnp.max(el2, axis=-1, keepdims=True)
    l2 = jnp.min(jnp.where(in_grp & (lane != l1) & (el2 == v2), lane, big), axis=-1, keepdims=True)
    t = jnp.exp(v2 - v1)
    fw1 = g_w / (1.0 + t)
    fw2 = g_w * t / (1.0 + t)
    e1 = l1 - N_GROUPS
    e2 = l2 - N_GROUPS

    oh1 = (lane == e1).astype(F32)
    oh2 = (lane == e2).astype(F32)
    cnt = oh1 + oh2
    r_i = lax.broadcasted_iota(jnp.int32, (tm, tm), 0)
    c_i = lax.broadcasted_iota(jnp.int32, (tm, tm), 1)
    tri = (c_i < r_i).astype(BF16)
    before = jnp.dot(tri, cnt.astype(BF16), preferred_element_type=F32) + run
    rank1 = jnp.sum(oh1 * before, axis=-1, keepdims=True)
    rank2 = jnp.sum(oh2 * before, axis=-1, keepdims=True)

    route = jnp.where(lane == 0, e1, 0.0)
    route = jnp.where(lane == 1, e2, route)
    route = jnp.where(lane == 2, rank1, route)
    route = jnp.where(lane == 3, rank2, route)
    route = jnp.where(lane == 4, fw1, route)
    route = jnp.where(lane == 5, fw2, route)
    route_ref[...] = route
    run = run + jnp.sum(cnt, axis=0, keepdims=True)
    run_ref[...] = run
    cnt_ref[...] = jnp.broadcast_to(run, cnt_ref.shape)


def _out_proj(oda, o_r, x2, w_out_b, ln_g, ln_b, w_router, b_router):
    n = x2.shape[0]
    tm = TM_OUT
    row = lambda w: pl.BlockSpec((1, w), lambda i: (0, 0))
    return pl.pallas_call(
        _out_proj_kernel,
        grid=(n // tm,),
        in_specs=[pl.BlockSpec((tm, DA_WIDTH), lambda i: (i, 0)),
                  pl.BlockSpec((tm, RET_WIDTH), lambda i: (i, 0)),
                  pl.BlockSpec((tm, D_MODEL), lambda i: (i, 0)),
                  pl.BlockSpec((D_MODEL, D_MODEL), lambda i: (0, 0), pipeline_mode=pl.Buffered(1)),
                  row(D_MODEL), row(D_MODEL),
                  pl.BlockSpec((D_MODEL, LANES), lambda i: (0, 0)),
                  row(LANES)],
        out_specs=[pl.BlockSpec((tm, D_MODEL), lambda i: (i, 0)),
                   pl.BlockSpec((tm, LANES), lambda i: (i, 0)),
                   pl.BlockSpec((SUBLANES, LANES), lambda i: (0, 0))],
        out_shape=[jax.ShapeDtypeStruct((n, D_MODEL), F32),
                   jax.ShapeDtypeStruct((n, LANES), F32),
                   jax.ShapeDtypeStruct((SUBLANES, LANES), F32)],
        scratch_shapes=[pltpu.VMEM((1, LANES), F32)],
        compiler_params=_cparams(("arbitrary",), VMEM_LIMIT),
        name="out_proj_ln1_router",
    )(oda, o_r, x2, w_out_b, ln_g, ln_b, w_router, b_router)


def _slot_kernel(route_ref, cnt_ref, pos_ref):
    route = route_ref[...]
    lane = lax.broadcasted_iota(jnp.int32, (1, LANES), 1)
    cnt = jnp.where(lane < N_EXPERTS, cnt_ref[0:1, :], 0.0)
    padded = jnp.floor((cnt + (TM_EXP - 1)) * (1.0 / TM_EXP)) * TM_EXP
    ends = padded
    for sh in (1, 2, 4, 8, 16):
        ends = ends + jnp.where(lane >= sh, pltpu.roll(ends, sh, 1), 0.0)
    offs = ends - padded
    lane_f = lax.broadcasted_iota(jnp.int32, route.shape, 1).astype(F32)
    off1 = jnp.sum(jnp.where(lane_f == route[:, 0:1], offs, 0.0), axis=-1, keepdims=True)
    off2 = jnp.sum(jnp.where(lane_f == route[:, 1:2], offs, 0.0), axis=-1, keepdims=True)
    slots = jnp.where(lane_f == 0.0, off1 + route[:, 2:3], 0.0)
    slots = jnp.where(lane_f == 1.0, off2 + route[:, 3:4], slots)
    pos_ref[...] = slots.T[0:SUBLANES, :].astype(jnp.int32)


def _slots(route, counts):
    n = route.shape[0]
    tm = 1024
    return pl.pallas_call(
        _slot_kernel,
        grid=(n // tm,),
        in_specs=[pl.BlockSpec((tm, LANES), lambda i: (i, 0)),
                  pl.BlockSpec((SUBLANES, LANES), lambda i: (0, 0))],
        out_specs=pl.BlockSpec((SUBLANES, tm), lambda i: (0, i)),
        out_shape=jax.ShapeDtypeStruct((SUBLANES, n), jnp.int32),
        compiler_params=_cparams(("arbitrary",)),
        name="moe_slots",
    )(route, counts)


def _inverse_kernel(pos1_ref, pos2_ref, fill_ref, inv_ref, sem):
    i = pl.program_id(0)
    tm = pos1_ref.shape[0]

    @pl.when(i == 0)
    def _():
        fill = pltpu.make_async_copy(fill_ref, inv_ref, sem)
        fill.start()
        fill.wait()

    def scatter(r, carry):
        tok = i * tm + r
        inv_ref[pos1_ref[r]] = tok
        inv_ref[pos2_ref[r]] = tok
        return carry

    lax.fori_loop(0, tm, scatter, 0, unroll=8)


def _inverse_map(pos1, pos2, p_rows):
    n = pos1.shape[0]
    tm = 1024
    smem = pl.BlockSpec((tm,), lambda i: (i,), memory_space=pltpu.SMEM)
    fill = jnp.asarray(np.arange(p_rows, dtype=np.int32) % n)
    return pl.pallas_call(
        _inverse_kernel,
        grid=(n // tm,),
        in_specs=[smem, smem, pl.BlockSpec(memory_space=pl.ANY)],
        out_specs=pl.BlockSpec(memory_space=pltpu.SMEM),
        out_shape=jax.ShapeDtypeStruct((p_rows,), jnp.int32),
        scratch_shapes=[pltpu.SemaphoreType.DMA],
        compiler_params=_cparams(("arbitrary",)),
        name="moe_inverse_map",
    )(pos1, pos2, fill)


def _expert_ffn_kernel(t_max, tb_ref, tv_ref, inv_ref, x1_ref, wg_ref, wu_ref, wd_ref, ys_ref,
                       xbuf_ref, ybuf_ref, zbuf_ref, gsem, osem, zsem):
    e = pl.program_id(0)
    tm = TM_EXP
    g_first = tb_ref[e]
    g_end = tb_ref[e + 1]
    n_total = tb_ref[N_EXPERTS]

    def gathered_rows(g):
        return ((tv_ref[g] + (GATHER_GROUP - 1)) // GATHER_GROUP) * GATHER_GROUP

    def gather_rows(g, buf_slot):
        def issue(q, carry):
            for k in range(GATHER_GROUP):
                r = q * GATHER_GROUP + k
                pltpu.make_async_copy(x1_ref.at[pl.ds(inv_ref[g * tm + r], 1), :],
                                      xbuf_ref.at[buf_slot, q, pl.ds(k, 1), :], gsem.at[buf_slot]).start()
            return carry
        lax.fori_loop(0, gathered_rows(g) // GATHER_GROUP, issue, 0)

    def wait_rows(g, buf_slot):
        rows = gathered_rows(g)
        k = GATHER_GROUP
        while k <= tm:
            @pl.when((rows & k) != 0)
            def _(k=k):
                pltpu.make_async_copy(x1_ref.at[pl.ds(0, k), :], x1_ref.at[pl.ds(0, k), :],
                                      gsem.at[buf_slot]).wait()
            k *= 2

    def tile_out(g):
        return ys_ref.at[pl.ds(pl.multiple_of(g * tm, tm), tm), :]

    @pl.when(e == 0)
    def _():
        xbuf_ref[...] = jnp.zeros(xbuf_ref.shape, F32)
        gather_rows(0, 0)

        @pl.when(n_total > 1)
        def _():
            gather_rows(1, 1)
        zbuf_ref[...] = jnp.zeros(zbuf_ref.shape, F32)

        def fill(g, carry):
            pltpu.make_async_copy(zbuf_ref, tile_out(g), zsem).start()
            return carry
        lax.fori_loop(n_total, t_max, fill, 0)

    def tile(g, carry):
        slot = g % 2
        xslot = g % GATHER_SLOTS

        @pl.when(g + (GATHER_SLOTS - 1) < n_total)
        def _():
            gather_rows(g + (GATHER_SLOTS - 1), (g + (GATHER_SLOTS - 1)) % GATHER_SLOTS)

        wait_rows(g, xslot)

        @pl.when(g >= 2)
        def _():
            pltpu.make_async_copy(ybuf_ref.at[slot], tile_out(g), osem.at[slot]).wait()

        def ffn(rows):
            x = xbuf_ref[xslot, 0:rows // SUBLANES].reshape(rows, D_MODEL).astype(BF16)
            gate = _dot_f32_weight(x, wg_ref[0])
            up = _dot_f32_weight(x, wu_ref[0])
            hmid = (gate * jax.nn.sigmoid(gate) * up).astype(BF16)
            ybuf_ref[slot, 0:rows, :] = _dot_f32_weight(hmid, wd_ref[0])

        valid = tv_ref[g]
        lower = 0
        for rows in (tm // 2, tm):
            @pl.when((valid > lower) & (valid <= rows))
            def _(rows=rows):
                ffn(rows)
                if rows < tm:
                    ybuf_ref[slot, rows:tm, :] = jnp.zeros((tm - rows, D_MODEL), F32)
            lower = rows

        pltpu.make_async_copy(ybuf_ref.at[slot], tile_out(g), osem.at[slot]).start()
        return carry

    lax.fori_loop(g_first, g_end, tile, 0)

    @pl.when(e == pl.num_programs(0) - 1)
    def _():
        pltpu.make_async_copy(ybuf_ref.at[0], tile_out(0), osem.at[0]).wait()
        pltpu.make_async_copy(ybuf_ref.at[1], tile_out(0), osem.at[1]).wait()

        def drain(g, carry):
            pltpu.make_async_copy(zbuf_ref, tile_out(g), zsem).wait()
            return carry
        lax.fori_loop(n_total, t_max, drain, 0)


def _expert_ffn(tile_base, tile_valid, inv, x1, w_gate, w_up, w_down, t_max):
    tm = TM_EXP
    assert 2 * x1.shape[0] >= 2 * tm
    grid_spec = pltpu.PrefetchScalarGridSpec(
        num_scalar_prefetch=3,
        grid=(N_EXPERTS,),
        in_specs=[pl.BlockSpec(memory_space=pl.ANY),
                  pl.BlockSpec((1, D_MODEL, EXPERT_FF), lambda e, tb, tv, inv: (e, 0, 0)),
                  pl.BlockSpec((1, D_MODEL, EXPERT_FF), lambda e, tb, tv, inv: (e, 0, 0)),
                  pl.BlockSpec((1, EXPERT_FF, D_MODEL), lambda e, tb, tv, inv: (e, 0, 0))],
        out_specs=pl.BlockSpec(memory_space=pl.ANY),
        scratch_shapes=[pltpu.VMEM((GATHER_SLOTS, tm // SUBLANES, SUBLANES, D_MODEL), F32),
                        pltpu.VMEM((2, tm, D_MODEL), F32),
                        pltpu.VMEM((tm, D_MODEL), F32),
                        pltpu.SemaphoreType.DMA((GATHER_SLOTS,)),
                        pltpu.SemaphoreType.DMA((2,)),
                        pltpu.SemaphoreType.DMA],
    )
    return pl.pallas_call(
        functools.partial(_expert_ffn_kernel, t_max),
        grid_spec=grid_spec,
        out_shape=jax.ShapeDtypeStruct((t_max * tm, D_MODEL), F32),
        compiler_params=_cparams(("arbitrary",), VMEM_LIMIT),
        name="moe_expert_ffn",
    )(tile_base, tile_valid, inv, x1, w_gate, w_up, w_down)


def _combine_kernel(pos1_ref, pos2_ref, pos1n_ref, pos2n_ref, x1_ref, route_ref, ys_ref, p_ref, wpg_ref,
                    wpp_ref, g_ref, b_ref, o_ref, y1_ref, y2_ref, sem):
    i = pl.program_id(0)
    tm = route_ref.shape[0]
    slot = i % 2

    def gather_rows(p1_ref, p2_ref, buf_slot):
        def issue(q, carry):
            for k in range(SUBLANES):
                r = q * SUBLANES + k
                pltpu.make_async_copy(ys_ref.at[pl.ds(p1_ref[r], 1), :],
                                      y1_ref.at[buf_slot, q, pl.ds(k, 1), :], sem.at[buf_slot]).start()
                pltpu.make_async_copy(ys_ref.at[pl.ds(p2_ref[r], 1), :],
                                      y2_ref.at[buf_slot, q, pl.ds(k, 1), :], sem.at[buf_slot]).start()
            return carry
        lax.fori_loop(0, tm // SUBLANES, issue, 0)

    @pl.when(i == 0)
    def _():
        gather_rows(pos1_ref, pos2_ref, 0)

    @pl.when(i + 1 < pl.num_programs(0))
    def _():
        gather_rows(pos1n_ref, pos2n_ref, 1 - slot)

    full = ys_ref.at[pl.ds(0, tm), :]
    pltpu.make_async_copy(full, full, sem.at[slot]).wait()
    pltpu.make_async_copy(full, full, sem.at[slot]).wait()

    for s in range(tm // SUB_COMB):
        rows = pl.ds(s * SUB_COMB, SUB_COMB)
        tiles = pl.ds(s * (SUB_COMB // SUBLANES), SUB_COMB // SUBLANES)
        route = route_ref[rows, :]
        y1 = y1_ref[slot, tiles].reshape(SUB_COMB, D_MODEL)
        y2 = y2_ref[slot, tiles].reshape(SUB_COMB, D_MODEL)
        m = route[:, 4:5] * y1 + route[:, 5:6] * y2
        x2 = _layer_norm(DEEPNORM_ALPHA * x1_ref[rows, :] + m, g_ref[...], b_ref[...])
        gate = jax.nn.sigmoid(_dot_f32_weight(x2.astype(BF16), wpg_ref[...]))
        proj = _dot_f32_weight(p_ref[rows, :].astype(BF16), wpp_ref[...])
        o_ref[rows, :] = x2 + gate * proj


def _combine(pos1, pos2, x1, route, ys, p2, w_ple_gate_b, w_ple_proj_b, ln_g, ln_b):
    n = route.shape[0]
    tm = TM_ROW
    smem = pl.BlockSpec((tm,), lambda i: (i,), memory_space=pltpu.SMEM)
    smem_next = pl.BlockSpec((tm,), lambda i: (jnp.minimum(i + 1, n // tm - 1),), memory_space=pltpu.SMEM)
    row = lambda w: pl.BlockSpec((1, w), lambda i: (0, 0))
    return pl.pallas_call(
        _combine_kernel,
        grid=(n // tm,),
        in_specs=[smem, smem, smem_next, smem_next,
                  pl.BlockSpec((tm, D_MODEL), lambda i: (i, 0)),
                  pl.BlockSpec((tm, LANES), lambda i: (i, 0)),
                  pl.BlockSpec(memory_space=pl.ANY),
                  pl.BlockSpec((tm, PLE_DIM), lambda i: (i, 0)),
                  pl.BlockSpec((D_MODEL, D_MODEL), lambda i: (0, 0), pipeline_mode=pl.Buffered(1)),
                  pl.BlockSpec((PLE_DIM, D_MODEL), lambda i: (0, 0), pipeline_mode=pl.Buffered(1)),
                  row(D_MODEL), row(D_MODEL)],
        out_specs=pl.BlockSpec((tm, D_MODEL), lambda i: (i, 0)),
        out_shape=jax.ShapeDtypeStruct((n, D_MODEL), F32),
        scratch_shapes=[pltpu.VMEM((2, tm // SUBLANES, SUBLANES, D_MODEL), F32),
                        pltpu.VMEM((2, tm // SUBLANES, SUBLANES, D_MODEL), F32),
                        pltpu.SemaphoreType.DMA((2,))],
        compiler_params=_cparams(("arbitrary",), VMEM_LIMIT),
        name="moe_combine_ln2_ple",
    )(pos1, pos2, pos1, pos2, x1, route, ys, p2, w_ple_gate_b, w_ple_proj_b, ln_g, ln_b)


def _tile_plan(counts, t_max):
    tm = TM_EXP
    cnt = counts[0, :N_EXPERTS].astype(jnp.int32)
    tiles = (cnt + tm - 1) // tm
    tile_base = jnp.concatenate([jnp.zeros((1,), jnp.int32), jnp.cumsum(tiles).astype(jnp.int32)])
    g = jnp.arange(t_max, dtype=jnp.int32)
    in_e = (g[:, None] >= tile_base[None, :-1]) & (g[:, None] < tile_base[None, 1:])
    rows = cnt[None, :] - (g[:, None] - tile_base[None, :-1]) * tm
    tile_valid = jnp.sum(jnp.where(in_e, jnp.clip(rows, 0, tm), 0), axis=1).astype(jnp.int32)
    return tile_base, tile_valid


def kernel(x, p, positions, w_in, w_out, da_lambda_q1, da_lambda_k1, da_lambda_q2, da_lambda_k2, da_subln_w, ln1_g, ln1_b, w_router_group, b_router_group, w_router_expert, b_router_expert, w_exp_gate, w_exp_up, w_exp_down, ln2_g, ln2_b, w_ple_gate, w_ple_proj):
    batch, seq, d = x.shape
    n = batch * seq
    assert d == D_MODEL and w_in.shape[0] == DEPTH == 1
    l = 0
    lam_init = 0.8 - 0.6 * math.exp(-0.3 * l)
    x2 = x.reshape(n, d)

    qkv = _in_proj(x2, w_in[l], positions)
    oda, o_r = _mixer(qkv, da_lambda_q1[l], da_lambda_k1[l], da_lambda_q2[l], da_lambda_k2[l],
                      da_subln_w[l], lam_init, batch, seq)

    pad = LANES - N_GROUPS - N_EXPERTS
    w_router = jnp.concatenate([w_router_group[l], w_router_expert[l], jnp.zeros((d, pad), F32)], axis=1)
    b_router = jnp.concatenate([b_router_group[l], b_router_expert[l], jnp.zeros((pad,), F32)]).reshape(1, LANES)
    x1, route, counts = _out_proj(oda, o_r, x2, w_out[l],
                                  ln1_g[l].reshape(1, d), ln1_b[l].reshape(1, d), w_router, b_router)

    t_max = (2 * n) // TM_EXP + N_EXPERTS
    tile_base, tile_valid = _tile_plan(counts, t_max)
    pos = _slots(route, counts)
    pos1, pos2 = pos[0], pos[1]
    inv = _inverse_map(pos1, pos2, t_max * TM_EXP)
    ys = _expert_ffn(tile_base, tile_valid, inv, x1, w_exp_gate[l], w_exp_up[l], w_exp_down[l], t_max)
    out = _combine(pos1, pos2, x1, route, ys, p[l].reshape(n, PLE_DIM),
                   w_ple_gate[l].astype(BF16), w_ple_proj[l],
                   ln2_g[l].reshape(1, d), ln2_b[l].reshape(1, d))
    return out.reshape(batch, seq, d)
```

```python
import functools
import math

import numpy as np
import jax
import jax.numpy as jnp
from jax import lax
from jax.experimental import pallas as pl
from jax.experimental.pallas import tpu as pltpu

D_MODEL = 2048
DA_HEADS = 8
DA_HEAD_DIM = 64
DA_V_DIM = 128
RET_HEADS = 4
RET_K_DIM = 128
RET_V_DIM = 256
RET_CHUNK = 128
RET_SPLIT = DA_HEADS // RET_HEADS
ROPE_THETA = 10000.0
PLE_DIM = 256
N_GROUPS = 4
EXPERTS_PER_GROUP = 8
N_EXPERTS = 32
EXPERT_FF = 512
DEPTH = 1
DEEPNORM_ALPHA = (2 * DEPTH) ** 0.25
LN_EPS = 1e-5
IN_WIDTH = 6144
DA_WIDTH = 1024
RET_WIDTH = 1024

OFF_QDA, OFF_KDA, OFF_VDA, OFF_QR, OFF_KR, OFF_VR, OFF_GR = 0, 1024, 2048, 3072, 3584, 4096, 5120

LANES = 128
SUBLANES = 8
VMEM_LIMIT = 56 * 1024 * 1024

BF16 = jnp.bfloat16
F32 = jnp.float32

TM_PROJ = 512
TN_PROJ = 512
W_CHUNK = 256
TQ = 256
TM_OUT = 512
SUB_COMB = 256
TM_ROW = 512
TM_EXP = 256
GATHER_GROUP = SUBLANES
GATHER_SLOTS = 4


def _cparams(sem, vmem=None):
    return pltpu.CompilerParams(dimension_semantics=sem, vmem_limit_bytes=vmem)


def _rope_tables(pos, inv, sgn):
    ang = pos.astype(F32) * inv
    c = jnp.cos(ang)
    s = jnp.sin(ang)
    low = lax.broadcasted_iota(jnp.int32, ang.shape, 1) < 64
    cr = pltpu.roll(c, 64, 1)
    sr = pltpu.roll(s, 64, 1)
    cos64, sin64 = jnp.where(low, cr, c), jnp.where(low, sr, s) * sgn[0:1, :]
    cos128, sin128 = jnp.where(low, c, cr), jnp.where(low, s, sr) * sgn[1:2, :]
    return cos64, sin64, cos128, sin128


def _rope_constants():
    i128 = np.power(ROPE_THETA, -np.arange(0, 128, 2, dtype=np.float64) / 128)
    i64 = np.power(ROPE_THETA, -np.arange(0, 64, 2, dtype=np.float64) / 64)
    inv = np.concatenate([i128, i64, i64]).astype(np.float32)[None, :]
    lane = np.arange(128)
    sgn = np.stack([np.where(lane % 64 < 32, -1.0, 1.0), np.where(lane < 64, -1.0, 1.0)]).astype(np.float32)
    sgn = np.concatenate([sgn, np.zeros((6, 128), np.float32)])
    return jnp.asarray(inv), jnp.asarray(sgn)


def _rope_cols(acc, cos, sin, half, scale):
    outs = []
    lane = lax.broadcasted_iota(jnp.int32, (acc.shape[0], LANES), 1)
    for c in range(acc.shape[1] // LANES):
        t = acc[:, c * LANES:(c + 1) * LANES]
        if half == 64:
            partner = pltpu.roll(t, 64, 1)
        else:
            partner = jnp.where((lane & 32) == 0, pltpu.roll(t, 96, 1), pltpu.roll(t, 32, 1))
        o = t * cos + partner * sin
        if scale != 1.0:
            o = o * scale
        outs.append(o)
    return jnp.concatenate(outs, axis=1)


def _in_proj_kernel(x_ref, w_hbm_ref, pos_ref, inv_ref, sgn_ref, o_ref, w_ref, stage_ref, sem):
    n_tiles = IN_WIDTH // TN_PROJ
    plain = [j for j in range(n_tiles) if OFF_VDA <= j * TN_PROJ < OFF_QR or j * TN_PROJ >= OFF_VR]
    rotary = [j for j in range(n_tiles) if j not in plain]
    order = plain[:-2] + rotary + plain[-2:]
    chunk_cols = [j * TN_PROJ + h * W_CHUNK for j in order for h in range(TN_PROJ // W_CHUNK)]

    def fetch(c):
        return pltpu.make_async_copy(w_hbm_ref.at[:, chunk_cols[c]:chunk_cols[c] + W_CHUNK],
                                     stage_ref.at[c % 2], sem.at[c % 2])

    def row_tile(stream_weight):
        xb = x_ref[...].astype(BF16)
        cos64, sin64, cos128, sin128 = _rope_tables(pos_ref[...], inv_ref[...], sgn_ref[...])
        if stream_weight:
            fetch(0).start()
        for t, j in enumerate(order):
            lo = j * TN_PROJ
            if stream_weight:
                per_tile = TN_PROJ // W_CHUNK
                for c in range(t * per_tile, (t + 1) * per_tile):
                    if c + 1 < len(chunk_cols):
                        fetch(c + 1).start()
                    fetch(c).wait()
                    w_ref[:, chunk_cols[c]:chunk_cols[c] + W_CHUNK] = stage_ref[c % 2].astype(BF16)
            acc = jnp.dot(xb, w_ref[:, lo:lo + TN_PROJ], preferred_element_type=F32)
            if lo < OFF_VDA:
                acc = _rope_cols(acc, cos64, sin64, 32, 1.0)
            elif OFF_QR <= lo < OFF_KR:
                acc = _rope_cols(acc, cos128, sin128, 64, 1.0)
            elif OFF_KR <= lo < OFF_VR:
                acc = _rope_cols(acc, cos128, sin128, 64, RET_K_DIM ** -0.5)
            o_ref[:, lo:lo + TN_PROJ] = acc.astype(BF16)

    @pl.when(pl.program_id(0) == 0)
    def _():
        row_tile(True)

    @pl.when(pl.program_id(0) != 0)
    def _():
        row_tile(False)


def _in_proj(x2, w_in, positions):
    n = x2.shape[0]
    inv, sgn = _rope_constants()
    return pl.pallas_call(
        _in_proj_kernel,
        grid=(n // TM_PROJ,),
        in_specs=[pl.BlockSpec((TM_PROJ, D_MODEL), lambda i: (i, 0)),
                  pl.BlockSpec(memory_space=pl.ANY),
                  pl.BlockSpec((TM_PROJ, 1), lambda i: (i, 0)),
                  pl.BlockSpec((1, LANES), lambda i: (0, 0)),
                  pl.BlockSpec((SUBLANES, LANES), lambda i: (0, 0))],
        out_specs=pl.BlockSpec((TM_PROJ, IN_WIDTH), lambda i: (i, 0)),
        out_shape=jax.ShapeDtypeStruct((n, IN_WIDTH), BF16),
        scratch_shapes=[pltpu.VMEM((D_MODEL, IN_WIDTH), BF16),
                        pltpu.VMEM((2, D_MODEL, W_CHUNK), F32),
                        pltpu.SemaphoreType.DMA((2,))],
        compiler_params=_cparams(("arbitrary",), VMEM_LIMIT),
        name="in_proj",
    )(x2, w_in, positions.reshape(n, 1), inv, sgn)


def _mixer_kernel(lam_init, q_ref, k_ref, v_ref, lq1_ref, lk1_ref, lq2_ref, lk2_ref, sw_ref,
                  rq_ref, rk_ref, rv_ref, rg_ref, lg_ref,
                  o_ref, ro_ref, state_ref):
    seq = q_ref.shape[0]
    n_q = seq // TQ
    ret_chunks = rq_ref.shape[0] // RET_CHUNK
    ret_consts = _retention_consts(lg_ref[0, 0:1, :])

    @pl.when(pl.program_id(1) % RET_SPLIT == 0)
    def _():
        state_ref[...] = jnp.zeros(state_ref.shape, F32)

    lam = (jnp.exp(jnp.sum(lq1_ref[...] * lk1_ref[...], axis=-1, keepdims=True))
           - jnp.exp(jnp.sum(lq2_ref[...] * lk2_ref[...], axis=-1, keepdims=True)) + lam_init)
    scale = DA_HEAD_DIM ** -0.5 * math.log2(math.e)
    lane = lax.broadcasted_iota(jnp.int32, (TQ, LANES), 1)
    row_i = lax.broadcasted_iota(jnp.int32, (2 * TQ, TQ), 0)
    col_i = lax.broadcasted_iota(jnp.int32, (2 * TQ, TQ), 1)
    causal = col_i <= jnp.where(row_i >= TQ, row_i - TQ, row_i)

    def step(qs, state, start, width, masked):
        m_old, l_old, acc_old = state
        k = k_ref[pl.ds(start, width), :]
        v = v_ref[pl.ds(start, width), :]
        s = lax.dot_general(qs, k, (((1,), (1,)), ((), ())), preferred_element_type=F32)
        if masked:
            s = jnp.where(causal, s, jnp.finfo(F32).min)
        chunks = [s[:, c * LANES:(c + 1) * LANES] for c in range(width // LANES)]
        m_new = jnp.maximum(m_old, jnp.max(functools.reduce(jnp.maximum, chunks), axis=-1, keepdims=True))
        alpha = jnp.exp2(m_old - m_new)
        p = jnp.concatenate([jnp.exp2(c - m_new).astype(BF16) for c in chunks], axis=1)
        v_ext = jnp.concatenate([v, jnp.ones_like(v)], axis=1)
        pv = jnp.dot(p, v_ext, preferred_element_type=F32)
        return m_new, alpha * l_old + pv[:, LANES:2 * LANES], alpha * acc_old + pv[:, 0:LANES]

    for qi in range(n_q):
        for c in range(qi * ret_chunks // n_q, (qi + 1) * ret_chunks // n_q):
            _retention_chunk(rq_ref, rk_ref, rv_ref, rg_ref, ro_ref, state_ref, ret_consts, c * RET_CHUNK)
        q = q_ref[qi * TQ:(qi + 1) * TQ, :]
        zero = jnp.zeros_like(q)
        qs = jnp.concatenate([jnp.where(lane < 64, q, zero), jnp.where(lane >= 64, q, zero)], axis=0)
        qs = (qs.astype(F32) * scale).astype(BF16)
        state = (jnp.full((2 * TQ, LANES), -jnp.inf, F32), jnp.zeros((2 * TQ, LANES), F32),
                 jnp.zeros((2 * TQ, LANES), F32))
        for j in range(qi // 2):
            state = step(qs, state, j * (2 * TQ), 2 * TQ, False)
        if qi % 2:
            state = step(qs, state, (qi - 1) * TQ, TQ, False)
        _, l_fin, acc_fin = step(qs, state, qi * TQ, TQ, True)

        a = acc_fin / l_fin
        o = a[0:TQ, :] - lam * a[TQ:2 * TQ, :]
        o = o * lax.rsqrt(jnp.mean(o * o, axis=-1, keepdims=True) + LN_EPS)
        o = o * sw_ref[...] * (1.0 - lam_init)
        o_ref[qi * TQ:(qi + 1) * TQ, :] = o.astype(BF16)


def _mixer(qkv, lq1, lk1, lq2, lk2, subln_w, lam_init, batch, seq):
    n = qkv.shape[0]
    assert DA_HEADS == RET_HEADS * RET_SPLIT
    part = seq // RET_SPLIT
    qb0 = OFF_QDA // LANES
    kb0 = OFF_KDA // LANES
    vb0 = OFF_VDA // LANES
    rq0 = OFF_QR // RET_K_DIM
    rk0 = OFF_KR // RET_K_DIM
    rv0 = OFF_VR // RET_V_DIM
    rg0 = OFF_GR // RET_V_DIM
    lg = np.log(1.0 - np.power(2.0, -5.0 - np.arange(RET_HEADS, dtype=np.float64)))
    lg_tab = np.broadcast_to(lg[:, None, None], (RET_HEADS, SUBLANES, LANES)).astype(np.float32)
    vec = lambda a: a.reshape(1, -1).astype(F32)
    small = lambda w: pl.BlockSpec((1, w), lambda b, h: (0, 0))
    ret_rows = lambda b, h: b * RET_SPLIT + h % RET_SPLIT
    return pl.pallas_call(
        functools.partial(_mixer_kernel, lam_init),
        grid=(batch, DA_HEADS),
        in_specs=[pl.BlockSpec((seq, LANES), lambda b, h: (b, qb0 + h)),
                  pl.BlockSpec((seq, LANES), lambda b, h: (b, kb0 + h)),
                  pl.BlockSpec((seq, LANES), lambda b, h: (b, vb0 + h)),
                  small(64), small(64), small(64), small(64), small(128),
                  pl.BlockSpec((part, RET_K_DIM), lambda b, h: (ret_rows(b, h), rq0 + h // RET_SPLIT)),
                  pl.BlockSpec((part, RET_K_DIM), lambda b, h: (ret_rows(b, h), rk0 + h // RET_SPLIT)),
                  pl.BlockSpec((part, RET_V_DIM), lambda b, h: (ret_rows(b, h), rv0 + h // RET_SPLIT)),
                  pl.BlockSpec((part, RET_V_DIM), lambda b, h: (ret_rows(b, h), rg0 + h // RET_SPLIT)),
                  pl.BlockSpec((1, SUBLANES, LANES), lambda b, h: (h // RET_SPLIT, 0, 0))],
        out_specs=[pl.BlockSpec((seq, LANES), lambda b, h: (b, h)),
                   pl.BlockSpec((part, RET_V_DIM), lambda b, h: (ret_rows(b, h), h // RET_SPLIT))],
        out_shape=[jax.ShapeDtypeStruct((n, DA_WIDTH), BF16),
                   jax.ShapeDtypeStruct((n, RET_WIDTH), BF16)],
        scratch_shapes=[pltpu.VMEM((RET_K_DIM, RET_V_DIM), F32)],
        compiler_params=_cparams(("arbitrary", "arbitrary")),
        name="attn_retention_mixer",
    )(qkv, qkv, qkv, vec(lq1), vec(lk1), vec(lq2), vec(lk2), vec(subln_w),
      qkv, qkv, qkv, qkv, jnp.asarray(lg_tab))


def _retention_consts(lg):
    C = RET_CHUNK
    lg_col = lg[:, 0:1]
    n_i = lax.broadcasted_iota(jnp.int32, (C, C), 0)
    m_i = lax.broadcasted_iota(jnp.int32, (C, C), 1)
    rel = (n_i - m_i).astype(F32)
    decay = jnp.where(rel >= 0, jnp.exp(rel * lg_col), 0.0)
    n_col = lax.broadcasted_iota(jnp.int32, (C, 1), 0).astype(F32)
    zeta = jnp.exp((C - 1.0 - n_col) * lg_col)
    xi = jnp.exp((n_col + 1.0) * lg_col)
    chunk_decay = jnp.exp(C * lg_col)
    return decay, zeta, xi, chunk_decay


def _retention_chunk(q_ref, k_ref, v_ref, g_ref, o_ref, state_ref, consts, start):
    C = RET_CHUNK
    decay, zeta, xi, chunk_decay = consts
    q = q_ref[pl.ds(start, C), :]
    k = k_ref[pl.ds(start, C), :]
    v = v_ref[pl.ds(start, C), :]
    g = g_ref[pl.ds(start, C), :].astype(F32)
    qk = lax.dot_general(q, k, (((1,), (1,)), ((), ())), preferred_element_type=F32)
    inner = (qk * decay).astype(BF16)
    inner_o = jnp.dot(inner, v, preferred_element_type=F32)
    state = state_ref[...]
    cross = jnp.dot(q, state.astype(BF16), preferred_element_type=F32)
    y = inner_o + cross * xi
    kz_t = (k.astype(F32) * zeta).T.astype(BF16)
    kv = jnp.dot(kz_t, v, preferred_element_type=F32)
    state_ref[...] = chunk_decay * state + kv
    mu = jnp.mean(y, axis=-1, keepdims=True)
    yc = y - mu
    var = jnp.mean(yc * yc, axis=-1, keepdims=True)
    yn = yc * lax.rsqrt(var + LN_EPS)
    o_ref[pl.ds(start, C), :] = (g * jax.nn.sigmoid(g) * yn).astype(BF16)


def _layer_norm(z, g, b):
    mu = jnp.mean(z, axis=-1, keepdims=True)
    zc = z - mu
    var = jnp.mean(zc * zc, axis=-1, keepdims=True)
    return zc * lax.rsqrt(var + LN_EPS) * g + b


def _dot_f32_weight(a, w):
    return lax.dot_general(a, w, (((1,), (0,)), ((), ())), preferred_element_type=F32)


def _split_bf16(a):
    hi = a.astype(BF16)
    lo = (a - hi.astype(F32)).astype(BF16)
    return hi, lo


def _out_proj_kernel(oda_ref, or_ref, x_ref, w_ref, g_ref, b_ref, wr_ref, br_ref,
                     x1_ref, route_ref, cnt_ref, run_ref):
    i = pl.program_id(0)

    @pl.when(i == 0)
    def _():
        run_ref[...] = jnp.zeros(run_ref.shape, F32)

    tm = x_ref.shape[0]
    run = run_ref[...]
    h = _dot_f32_weight(oda_ref[...], w_ref[0:DA_WIDTH, :])
    h = h + _dot_f32_weight(or_ref[...], w_ref[DA_WIDTH:DA_WIDTH + RET_WIDTH, :])
    x1 = _layer_norm(DEEPNORM_ALPHA * x_ref[...] + h, g_ref[...], b_ref[...])
    x1_ref[...] = x1

    xh, xl = _split_bf16(x1)
    wh, wl = _split_bf16(wr_ref[...])
    hh_hl = jnp.dot(xh, jnp.concatenate([wh, wl], axis=1), preferred_element_type=F32)
    logits = (hh_hl[:, 0:LANES] + hh_hl[:, LANES:2 * LANES]
              + jnp.dot(xl, wh, preferred_element_type=F32)) + br_ref[...]
    lane = lax.broadcasted_iota(jnp.int32, logits.shape, 1).astype(F32)
    neg = jnp.float32(-jnp.inf)
    big = jnp.float32(1 << 20)
    is_g = lane < N_GROUPS
    gl = jnp.where(is_g, logits, neg)
    gmax = jnp.max(gl, axis=-1, keepdims=True)
    g_idx = jnp.min(jnp.where(is_g & (gl == gmax), lane, big), axis=-1, keepdims=True)
    g_w = 1.0 / jnp.sum(jnp.where(is_g, jnp.exp(gl - gmax), 0.0), axis=-1, keepdims=True)
    e_lane = lane - N_GROUPS
    in_grp = (e_lane >= g_idx * EXPERTS_PER_GROUP) & (e_lane < (g_idx + 1.0) * EXPERTS_PER_GROUP)
    el = jnp.where(in_grp, logits, neg)
    v1 = jnp.max(el, axis=-1, keepdims=True)
    l1 = jnp.min(jnp.where(in_grp & (el == v1), lane, big), axis=-1, keepdims=True)
    el2 = jnp.where(lane == l1, neg, el)
    v2 = jnp.max(el2, axis=-1, keepdims=True)
    l2 = jnp.min(jnp.where(in_grp & (lane != l1) & (el2 == v2), lane, big), axis=-1, keepdims=True)
    t = jnp.exp(v2 - v1)
    fw1 = g_w / (1.0 + t)
    fw2 = g_w * t / (1.0 + t)
    e1 = l1 - N_GROUPS
    e2 = l2 - N_GROUPS

    oh1 = (lane == e1).astype(F32)
    oh2 = (lane == e2).astype(F32)
    cnt = oh1 + oh2
    r_i = lax.broadcasted_iota(jnp.int32, (tm, tm), 0)
    c_i = lax.broadcasted_iota(jnp.int32, (tm, tm), 1)
    tri = (c_i < r_i).astype(BF16)
    before = jnp.dot(tri, cnt.astype(BF16), preferred_element_type=F32) + run
    rank1 = jnp.sum(oh1 * before, axis=-1, keepdims=True)
    rank2 = jnp.sum(oh2 * before, axis=-1, keepdims=True)

    route = jnp.where(lane == 0, e1, 0.0)
    route = jnp.where(lane == 1, e2, route)
    route = jnp.where(lane == 2, rank1, route)
    route = jnp.where(lane == 3, rank2, route)
    route = jnp.where(lane == 4, fw1, route)
    route = jnp.where(lane == 5, fw2, route)
    route_ref[...] = route
    run = run + jnp.sum(cnt, axis=0, keepdims=True)
    run_ref[...] = run
    cnt_ref[...] = jnp.broadcast_to(run, cnt_ref.shape)


def _out_proj(oda, o_r, x2, w_out_b, ln_g, ln_b, w_router, b_router):
    n = x2.shape[0]
    tm = TM_OUT
    row = lambda w: pl.BlockSpec((1, w), lambda i: (0, 0))
    return pl.pallas_call(
        _out_proj_kernel,
        grid=(n // tm,),
        in_specs=[pl.BlockSpec((tm, DA_WIDTH), lambda i: (i, 0)),
                  pl.BlockSpec((tm, RET_WIDTH), lambda i: (i, 0)),
                  pl.BlockSpec((tm, D_MODEL), lambda i: (i, 0)),
                  pl.BlockSpec((D_MODEL, D_MODEL), lambda i: (0, 0), pipeline_mode=pl.Buffered(1)),
                  row(D_MODEL), row(D_MODEL),
                  pl.BlockSpec((D_MODEL, LANES), lambda i: (0, 0)),
                  row(LANES)],
        out_specs=[pl.BlockSpec((tm, D_MODEL), lambda i: (i, 0)),
                   pl.BlockSpec((tm, LANES), lambda i: (i, 0)),
                   pl.BlockSpec((SUBLANES, LANES), lambda i: (0, 0))],
        out_shape=[jax.ShapeDtypeStruct((n, D_MODEL), F32),
                   jax.ShapeDtypeStruct((n, LANES), F32),
                   jax.ShapeDtypeStruct((SUBLANES, LANES), F32)],
        scratch_shapes=[pltpu.VMEM((1, LANES), F32)],
        compiler_params=_cparams(("arbitrary",), VMEM_LIMIT),
        name="out_proj_ln1_router",
    )(oda, o_r, x2, w_out_b, ln_g, ln_b, w_router, b_router)


def _slot_kernel(route_ref, cnt_ref, pos_ref):
    route = route_ref[...]
    lane = lax.broadcasted_iota(jnp.int32, (1, LANES), 1)
    cnt = jnp.where(lane < N_EXPERTS, cnt_ref[0:1, :], 0.0)
    padded = jnp.floor((cnt + (TM_EXP - 1)) * (1.0 / TM_EXP)) * TM_EXP
    ends = padded
    for sh in (1, 2, 4, 8, 16):
        ends = ends + jnp.where(lane >= sh, pltpu.roll(ends, sh, 1), 0.0)
    offs = ends - padded
    lane_f = lax.broadcasted_iota(jnp.int32, route.shape, 1).astype(F32)
    off1 = jnp.sum(jnp.where(lane_f == route[:, 0:1], offs, 0.0), axis=-1, keepdims=True)
    off2 = jnp.sum(jnp.where(lane_f == route[:, 1:2], offs, 0.0), axis=-1, keepdims=True)
    slots = jnp.where(lane_f == 0.0, off1 + route[:, 2:3], 0.0)
    slots = jnp.where(lane_f == 1.0, off2 + route[:, 3:4], slots)
    pos_ref[...] = slots.T[0:SUBLANES, :].astype(jnp.int32)


def _slots(route, counts):
    n = route.shape[0]
    tm = 1024
    return pl.pallas_call(
        _slot_kernel,
        grid=(n // tm,),
        in_specs=[pl.BlockSpec((tm, LANES), lambda i: (i, 0)),
                  pl.BlockSpec((SUBLANES, LANES), lambda i: (0, 0))],
        out_specs=pl.BlockSpec((SUBLANES, tm), lambda i: (0, i)),
        out_shape=jax.ShapeDtypeStruct((SUBLANES, n), jnp.int32),
        compiler_params=_cparams(("arbitrary",)),
        name="moe_slots",
    )(route, counts)


def _inverse_kernel(pos1_ref, pos2_ref, fill_ref, inv_ref, sem):
    i = pl.program_id(0)
    tm = pos1_ref.shape[0]

    @pl.when(i == 0)
    def _():
        fill = pltpu.make_async_copy(fill_ref, inv_ref, sem)
        fill.start()
        fill.wait()

    def scatter(r, carry):
        tok = i * tm + r
        inv_ref[pos1_ref[r]] = tok
        inv_ref[pos2_ref[r]] = tok
        return carry

    lax.fori_loop(0, tm, scatter, 0, unroll=8)


def _inverse_map(pos1, pos2, p_rows):
    n = pos1.shape[0]
    tm = 1024
    smem = pl.BlockSpec((tm,), lambda i: (i,), memory_space=pltpu.SMEM)
    fill = jnp.asarray(np.arange(p_rows, dtype=np.int32) % n)
    return pl.pallas_call(
        _inverse_kernel,
        grid=(n // tm,),
        in_specs=[smem, smem, pl.BlockSpec(memory_space=pl.ANY)],
        out_specs=pl.BlockSpec(memory_space=pltpu.SMEM),
        out_shape=jax.ShapeDtypeStruct((p_rows,), jnp.int32),
        scratch_shapes=[pltpu.SemaphoreType.DMA],
        compiler_params=_cparams(("arbitrary",)),
        name="moe_inverse_map",
    )(pos1, pos2, fill)


def _expert_ffn_kernel(t_max, tb_ref, tv_ref, inv_ref, x1_ref, wg_ref, wu_ref, wd_ref, ys_ref,
                       xbuf_ref, ybuf_ref, zbuf_ref, gsem, osem, zsem):
    e = pl.program_id(0)
    tm = TM_EXP
    g_first = tb_ref[e]
    g_end = tb_ref[e + 1]
    n_total = tb_ref[N_EXPERTS]

    def gathered_rows(g):
        return ((tv_ref[g] + (GATHER_GROUP - 1)) // GATHER_GROUP) * GATHER_GROUP

    def gather_rows(g, buf_slot):
        def issue(q, carry):
            for k in range(GATHER_GROUP):
                r = q * GATHER_GROUP + k
                pltpu.make_async_copy(x1_ref.at[pl.ds(inv_ref[g * tm + r], 1), :],
                                      xbuf_ref.at[buf_slot, q, pl.ds(k, 1), :], gsem.at[buf_slot]).start()
            return carry
        lax.fori_loop(0, gathered_rows(g) // GATHER_GROUP, issue, 0)

    def wait_rows(g, buf_slot):
        rows = gathered_rows(g)
        k = GATHER_GROUP
        while k <= tm:
            @pl.when((rows & k) != 0)
            def _(k=k):
                pltpu.make_async_copy(x1_ref.at[pl.ds(0, k), :], x1_ref.at[pl.ds(0, k), :],
                                      gsem.at[buf_slot]).wait()
            k *= 2

    def tile_out(g):
        return ys_ref.at[pl.ds(pl.multiple_of(g * tm, tm), tm), :]

    @pl.when(e == 0)
    def _():
        xbuf_ref[...] = jnp.zeros(xbuf_ref.shape, F32)
        gather_rows(0, 0)
        for t in range(1, GATHER_SLOTS - 1):
            @pl.when(n_total > t)
            def _(t=t):
                gather_rows(t, t)
        zbuf_ref[...] = jnp.zeros(zbuf_ref.shape, F32)

        def fill(g, carry):
            pltpu.make_async_copy(zbuf_ref, tile_out(g), zsem).start()
            return carry
        lax.fori_loop(n_total, t_max, fill, 0)

    def tile(g, carry):
        slot = g % 2
        xslot = g % GATHER_SLOTS

        @pl.when(g + (GATHER_SLOTS - 1) < n_total)
        def _():
            gather_rows(g + (GATHER_SLOTS - 1), (g + (GATHER_SLOTS - 1)) % GATHER_SLOTS)

        wait_rows(g, xslot)

        @pl.when(g >= 2)
        def _():
            pltpu.make_async_copy(ybuf_ref.at[slot], tile_out(g), osem.at[slot]).wait()

        def ffn(rows):
            x = xbuf_ref[xslot, 0:rows // SUBLANES].reshape(rows, D_MODEL).astype(BF16)
            gate = _dot_f32_weight(x, wg_ref[0])
            up = _dot_f32_weight(x, wu_ref[0])
            hmid = (gate * jax.nn.sigmoid(gate) * up).astype(BF16)
            ybuf_ref[slot, 0:rows, :] = _dot_f32_weight(hmid, wd_ref[0])

        valid = tv_ref[g]
        lower = 0
        for rows in (tm // 2, tm):
            @pl.when((valid > lower) & (valid <= rows))
            def _(rows=rows):
                ffn(rows)
                if rows < tm:
                    ybuf_ref[slot, rows:tm, :] = jnp.zeros((tm - rows, D_MODEL), F32)
            lower = rows

        pltpu.make_async_copy(ybuf_ref.at[slot], tile_out(g), osem.at[slot]).start()
        return carry

    lax.fori_loop(g_first, g_end, tile, 0)

    @pl.when(e == pl.num_programs(0) - 1)
    def _():
        pltpu.make_async_copy(ybuf_ref.at[0], tile_out(0), osem.at[0]).wait()
        pltpu.make_async_copy(ybuf_ref.at[1], tile_out(0), osem.at[1]).wait()

        def drain(g, carry):
            pltpu.make_async_copy(zbuf_ref, tile_out(g), zsem).wait()
            return carry
        lax.fori_loop(n_total, t_max, drain, 0)


def _expert_ffn(tile_base, tile_valid, inv, x1, w_gate, w_up, w_down, t_max):
    tm = TM_EXP
    assert 2 * x1.shape[0] >= 2 * tm
    grid_spec = pltpu.PrefetchScalarGridSpec(
        num_scalar_prefetch=3,
        grid=(N_EXPERTS,),
        in_specs=[pl.BlockSpec(memory_space=pl.ANY),
                  pl.BlockSpec((1, D_MODEL, EXPERT_FF), lambda e, tb, tv, inv: (e, 0, 0)),
                  pl.BlockSpec((1, D_MODEL, EXPERT_FF), lambda e, tb, tv, inv: (e, 0, 0)),
                  pl.BlockSpec((1, EXPERT_FF, D_MODEL), lambda e, tb, tv, inv: (e, 0, 0))],
        out_specs=pl.BlockSpec(memory_space=pl.ANY),
        scratch_shapes=[pltpu.VMEM((GATHER_SLOTS, tm // SUBLANES, SUBLANES, D_MODEL), F32),
                        pltpu.VMEM((2, tm, D_MODEL), F32),
                        pltpu.VMEM((tm, D_MODEL), F32),
                        pltpu.SemaphoreType.DMA((GATHER_SLOTS,)),
                        pltpu.SemaphoreType.DMA((2,)),
                        pltpu.SemaphoreType.DMA],
    )
    return pl.pallas_call(
        functools.partial(_expert_ffn_kernel, t_max),
        grid_spec=grid_spec,
        out_shape=jax.ShapeDtypeStruct((t_max * tm, D_MODEL), F32),
        compiler_params=_cparams(("arbitrary",), VMEM_LIMIT),
        name="moe_expert_ffn",
    )(tile_base, tile_valid, inv, x1, w_gate, w_up, w_down)


def _combine_kernel(pos1_ref, pos2_ref, pos1n_ref, pos2n_ref, x1_ref, route_ref, ys_ref, p_ref, wpg_ref,
                    wpp_ref, g_ref, b_ref, o_ref, y1_ref, y2_ref, sem):
    i = pl.program_id(0)
    tm = route_ref.shape[0]
    slot = i % 2

    def gather_rows(p1_ref, p2_ref, buf_slot):
        def issue(q, carry):
            for k in range(SUBLANES):
                r = q * SUBLANES + k
                pltpu.make_async_copy(ys_ref.at[pl.ds(p1_ref[r], 1), :],
                                      y1_ref.at[buf_slot, q, pl.ds(k, 1), :], sem.at[buf_slot]).start()
                pltpu.make_async_copy(ys_ref.at[pl.ds(p2_ref[r], 1), :],
                                      y2_ref.at[buf_slot, q, pl.ds(k, 1), :], sem.at[buf_slot]).start()
            return carry
        lax.fori_loop(0, tm // SUBLANES, issue, 0)

    @pl.when(i == 0)
    def _():
        gather_rows(pos1_ref, pos2_ref, 0)

    @pl.when(i + 1 < pl.num_programs(0))
    def _():
        gather_rows(pos1n_ref, pos2n_ref, 1 - slot)

    full = ys_ref.at[pl.ds(0, tm), :]
    pltpu.make_async_copy(full, full, sem.at[slot]).wait()
    pltpu.make_async_copy(full, full, sem.at[slot]).wait()

    for s in range(tm // SUB_COMB):
        rows = pl.ds(s * SUB_COMB, SUB_COMB)
        tiles = pl.ds(s * (SUB_COMB // SUBLANES), SUB_COMB // SUBLANES)
        route = route_ref[rows, :]
        y1 = y1_ref[slot, tiles].reshape(SUB_COMB, D_MODEL)
        y2 = y2_ref[slot, tiles].reshape(SUB_COMB, D_MODEL)
        m = route[:, 4:5] * y1 + route[:, 5:6] * y2
        x2 = _layer_norm(DEEPNORM_ALPHA * x1_ref[rows, :] + m, g_ref[...], b_ref[...])
        gate = jax.nn.sigmoid(_dot_f32_weight(x2.astype(BF16), wpg_ref[...]))
        proj = _dot_f32_weight(p_ref[rows, :].astype(BF16), wpp_ref[...])
        o_ref[rows, :] = x2 + gate * proj


def _combine(pos1, pos2, x1, route, ys, p2, w_ple_gate_b, w_ple_proj_b, ln_g, ln_b):
    n = route.shape[0]
    tm = TM_ROW
    smem = pl.BlockSpec((tm,), lambda i: (i,), memory_space=pltpu.SMEM)
    smem_next = pl.BlockSpec((tm,), lambda i: (jnp.minimum(i + 1, n // tm - 1),), memory_space=pltpu.SMEM)
    row = lambda w: pl.BlockSpec((1, w), lambda i: (0, 0))
    return pl.pallas_call(
        _combine_kernel,
        grid=(n // tm,),
        in_specs=[smem, smem, smem_next, smem_next,
                  pl.BlockSpec((tm, D_MODEL), lambda i: (i, 0)),
                  pl.BlockSpec((tm, LANES), lambda i: (i, 0)),
                  pl.BlockSpec(memory_space=pl.ANY),
                  pl.BlockSpec((tm, PLE_DIM), lambda i: (i, 0)),
                  pl.BlockSpec((D_MODEL, D_MODEL), lambda i: (0, 0), pipeline_mode=pl.Buffered(1)),
                  pl.BlockSpec((PLE_DIM, D_MODEL), lambda i: (0, 0), pipeline_mode=pl.Buffered(1)),
                  row(D_MODEL), row(D_MODEL)],
        out_specs=pl.BlockSpec((tm, D_MODEL), lambda i: (i, 0)),
        out_shape=jax.ShapeDtypeStruct((n, D_MODEL), F32),
        scratch_shapes=[pltpu.VMEM((2, tm // SUBLANES, SUBLANES, D_MODEL), F32),
                        pltpu.VMEM((2, tm // SUBLANES, SUBLANES, D_MODEL), F32),
                        pltpu.SemaphoreType.DMA((2,))],
        compiler_params=_cparams(("arbitrary",), VMEM_LIMIT),
        name="moe_combine_ln2_ple",
    )(pos1, pos2, pos1, pos2, x1, route, ys, p2, w_ple_gate_b, w_ple_proj_b, ln_g, ln_b)


def _tile_plan(counts, t_max):
    tm = TM_EXP
    cnt = counts[0, :N_EXPERTS].astype(jnp.int32)
    tiles = (cnt + tm - 1) // tm
    tile_base = jnp.concatenate([jnp.zeros((1,), jnp.int32), jnp.cumsum(tiles).astype(jnp.int32)])
    g = jnp.arange(t_max, dtype=jnp.int32)
    in_e = (g[:, None] >= tile_base[None, :-1]) & (g[:, None] < tile_base[None, 1:])
    rows = cnt[None, :] - (g[:, None] - tile_base[None, :-1]) * tm
    tile_valid = jnp.sum(jnp.where(in_e, jnp.clip(rows, 0, tm), 0), axis=1).astype(jnp.int32)
    return tile_base, tile_valid


def kernel(x, p, positions, w_in, w_out, da_lambda_q1, da_lambda_k1, da_lambda_q2, da_lambda_k2, da_subln_w, ln1_g, ln1_b, w_router_group, b_router_group, w_router_expert, b_router_expert, w_exp_gate, w_exp_up, w_exp_down, ln2_g, ln2_b, w_ple_gate, w_ple_proj):
    batch, seq, d = x.shape
    n = batch * seq
    assert d == D_MODEL and w_in.shape[0] == DEPTH == 1
    l = 0
    lam_init = 0.8 - 0.6 * math.exp(-0.3 * l)
    x2 = x.reshape(n, d)

    qkv = _in_proj(x2, w_in[l], positions)
    oda, o_r = _mixer(qkv, da_lambda_q1[l], da_lambda_k1[l], da_lambda_q2[l], da_lambda_k2[l],
                      da_subln_w[l], lam_init, batch, seq)

    pad = LANES - N_GROUPS - N_EXPERTS
    w_router = jnp.concatenate([w_router_group[l], w_router_expert[l], jnp.zeros((d, pad), F32)], axis=1)
    b_router = jnp.concatenate([b_router_group[l], b_router_expert[l], jnp.zeros((pad,), F32)]).reshape(1, LANES)
    x1, route, counts = _out_proj(oda, o_r, x2, w_out[l],
                                  ln1_g[l].reshape(1, d), ln1_b[l].reshape(1, d), w_router, b_router)

    t_max = (2 * n) // TM_EXP + N_EXPERTS
    tile_base, tile_valid = _tile_plan(counts, t_max)
    pos = _slots(route, counts)
    pos1, pos2 = pos[0], pos[1]
    inv = _inverse_map(pos1, pos2, t_max * TM_EXP)
    ys = _expert_ffn(tile_base, tile_valid, inv, x1, w_exp_gate[l], w_exp_up[l], w_exp_down[l], t_max)
    out = _combine(pos1, pos2, x1, route, ys, p[l].reshape(n, PLE_DIM),
                   w_ple_gate[l].astype(BF16), w_ple_proj[l],
                   ln2_g[l].reshape(1, d), ln2_b[l].reshape(1, d))
    return out.reshape(batch, seq, d)
```
